```python
import jax
import jax.numpy as jnp
from jax import lax
import numpy as np

D_MODEL = 1024
BATCH = 16
SEQ = 2048
DEPTH = 2

CTX_LEN = 256
GRID_W = 64
HEAD_DIM = 64

NA_HEADS = 6
NA_WIN_ROWS = 8
NA_WIN_COLS = 16

GLA_HEADS = 4
GLA_DK = 32
GLA_DV = 64
GLA_RANK = 16
GLA_GATE_NORM = 16.0
GLA_CHUNK = 64

GQA_Q_HEADS = 6
GQA_KV_HEADS = 2
GQA_REP = GQA_Q_HEADS // GQA_KV_HEADS
Q_BLOCK = 128
ROPE_THETA = 10000.0
ROPE_AXIS_PAIRS = HEAD_DIM // 4

FFN_HIDDEN = ((8 * D_MODEL + 3 * 256 - 1) // (3 * 256)) * 256

NA_W = NA_HEADS * HEAD_DIM
GLA_KW = GLA_HEADS * GLA_DK
GLA_VW = GLA_HEADS * GLA_DV
GQA_QW = GQA_Q_HEADS * HEAD_DIM
GQA_KVW = GQA_KV_HEADS * HEAD_DIM
MIX_W = NA_W + GLA_VW + GQA_QW
IN_WIDTHS = (NA_W, NA_W, NA_W, GLA_KW, GLA_KW, GLA_VW, GLA_VW, 2 * GLA_RANK, GQA_QW, GQA_KVW, GQA_KVW)
IN_COLS = sum(IN_WIDTHS)
IN_SPLITS = [int(s) for s in np.cumsum(IN_WIDTHS)[:-1]]

DEEPNORM_ALPHA = (2.0 * DEPTH) ** 0.25
DEEPNORM_BETA = (8.0 * DEPTH) ** -0.25
LN_EPS = 1e-5
RMS_EPS = 1e-6

kernel_name = 'hybrid_na_gla_gqa_dit_block'


def layer_norm(x, g, b):
    xf = x.astype(jnp.float32)
    mu = jnp.mean(xf, axis=-1, keepdims=True)
    var = jnp.mean(jnp.square(xf - mu), axis=-1, keepdims=True)
    return ((xf - mu) * lax.rsqrt(var + LN_EPS) * g + b).astype(x.dtype)


def rms_norm(x, w):
    xf = x.astype(jnp.float32)
    ms = jnp.mean(jnp.square(xf), axis=-1, keepdims=True)
    return (xf * lax.rsqrt(ms + RMS_EPS) * w).astype(x.dtype)


def modulate(x, shift, scale):
    return x * (1 + scale) + shift


def to_heads(t, n_heads):
    b, n, _ = t.shape
    return t.reshape(b, n, n_heads, -1).transpose(0, 2, 1, 3)


def merge_heads(t):
    b, h, n, d = t.shape
    return t.transpose(0, 2, 1, 3).reshape(b, n, h * d)


def flip_seq(t):
    return jnp.flip(t, axis=2)


def axial_rope_tables(n_tokens):
    t = jnp.arange(n_tokens)
    row = (t // GRID_W).astype(jnp.float32)
    col = (t % GRID_W).astype(jnp.float32)
    inv_freq = ROPE_THETA ** (-jnp.arange(ROPE_AXIS_PAIRS, dtype=jnp.float32) / ROPE_AXIS_PAIRS)
    ang_r = row[:, None] * inv_freq
    ang_c = col[:, None] * inv_freq
    ang = jnp.concatenate([ang_r, ang_r, ang_c, ang_c], axis=-1)
    return jnp.cos(ang), jnp.sin(ang)


def apply_axial_rope(x, cos, sin):
    xf = x.astype(jnp.float32)
    x1, x2, x3, x4 = jnp.split(xf, 4, axis=-1)
    rot = jnp.concatenate([-x2, x1, -x4, x3], axis=-1)
    return (xf * cos + rot * sin).astype(x.dtype)


def context_attention(q, k, v):
    s = jnp.einsum('bgrqd,bgkd->bgrqk', q, k, preferred_element_type=jnp.float32) * (q.shape[-1] ** -0.5)
    p = jax.nn.softmax(s, axis=-1).astype(v.dtype)
    return jnp.einsum('bgrqk,bgkd->bgrqd', p, v)


def neighborhood_attention(q, k, v, k_ctx, v_ctx, rpb):
    B, H, N, d = q.shape
    rows = N // GRID_W
    wr = min(NA_WIN_ROWS, rows)
    scale = d ** -0.5
    qg = q.reshape(B, H, rows, GRID_W, d)
    kg = k.reshape(B, H, rows, GRID_W, d)
    vg = v.reshape(B, H, rows, GRID_W, d)
    cols = jnp.arange(GRID_W)
    col_start = jnp.clip(cols - NA_WIN_COLS // 2, 0, GRID_W - NA_WIN_COLS)
    col_in = (cols[None, :] >= col_start[:, None]) & (cols[None, :] < col_start[:, None] + NA_WIN_COLS)
    dc_idx = jnp.clip(cols[None, :] - cols[:, None], -(NA_WIN_COLS - 1), NA_WIN_COLS - 1) + (NA_WIN_COLS - 1)

    def row_block(r):
        rs = jnp.clip(r - wr // 2, 0, rows - wr)
        q_r = lax.dynamic_index_in_dim(qg, r, axis=2, keepdims=False)
        k_s = lax.dynamic_slice_in_dim(kg, rs, wr, axis=2)
        v_s = lax.dynamic_slice_in_dim(vg, rs, wr, axis=2)
        dr_idx = rs + jnp.arange(wr) - r + (NA_WIN_ROWS - 1)
        bias = rpb[:, dr_idx[None, :, None], dc_idx[:, None, :]]
        s_lat = jnp.einsum('bhqd,bhrkd->bhqrk', q_r, k_s, preferred_element_type=jnp.float32) * scale + bias.astype(jnp.float32)
        s_lat = jnp.where(col_in[:, None, :], s_lat, -jnp.inf).reshape(B, H, GRID_W, wr * GRID_W)
        s_ctx = jnp.einsum('bhqd,bhcd->bhqc', q_r, k_ctx, preferred_element_type=jnp.float32) * scale
        p = jax.nn.softmax(jnp.concatenate([s_lat, s_ctx], axis=-1), axis=-1).astype(v.dtype)
        p_lat = p[..., : wr * GRID_W].reshape(B, H, GRID_W, wr, GRID_W)
        p_ctx = p[..., wr * GRID_W:]
        return jnp.einsum('bhqrk,bhrkd->bhqd', p_lat, v_s) + jnp.einsum('bhqc,bhcd->bhqd', p_ctx, v_ctx)

    o = lax.map(row_block, jnp.arange(rows))
    return o.transpose(1, 2, 0, 3, 4).reshape(B, H, N, d)


def gqa_latent_attention(q, k, v, k_ctx, v_ctx):
    B, G, R, N, d = q.shape
    nb = N // Q_BLOCK
    keys = jnp.concatenate([k, k_ctx], axis=2)
    vals = jnp.concatenate([v, v_ctx], axis=2)
    qb = q.reshape(B, G, R, nb, Q_BLOCK, d).transpose(3, 0, 1, 2, 4, 5)

    def block(q_blk):
        s = jnp.einsum('bgrqd,bgkd->bgrqk', q_blk, keys, preferred_element_type=jnp.float32) * (d ** -0.5)
        p = jax.nn.softmax(s, axis=-1).astype(vals.dtype)
        return jnp.einsum('bgrqk,bgkd->bgrqd', p, vals)

    o = lax.map(block, qb)
    return o.transpose(1, 2, 3, 0, 4, 5).reshape(B, G * R, N, d)


def gla_log_gates(lr, wa2, ba):
    B, N, _ = lr.shape
    z = jnp.einsum('bner,erk->ebnk', lr.reshape(B, N, 2, GLA_RANK), wa2) + ba[:, None, None, :]
    la = jax.nn.log_sigmoid(z.astype(jnp.float32)) / GLA_GATE_NORM
    la = la.reshape(2, B, N, GLA_HEADS, GLA_DK).transpose(0, 1, 3, 2, 4)
    return la[0], la[1]


def gla_chunked(q, k, v, log_a, s0):
    B, H, N, dk = q.shape
    dv = v.shape[-1]
    nc = N // GLA_CHUNK

    def to_chunks(t):
        return t.astype(jnp.float32).reshape(B, H, nc, GLA_CHUNK, t.shape[-1]).transpose(2, 0, 1, 3, 4)

    causal = jnp.tril(jnp.ones((GLA_CHUNK, GLA_CHUNK), dtype=bool))

    def step(S, xs):
        qi, ki, vi, gi = xs
        b = jnp.cumsum(gi, axis=-2)
        b_last = b[..., -1:, :]
        diff = b[..., :, None, :] - b[..., None, :, :]
        decay = jnp.exp(jnp.where(causal[:, :, None], diff, -jnp.inf))
        A = jnp.einsum('bhtd,bhsd,bhtsd->bhts', qi, ki, decay)
        o = jnp.einsum('bhts,bhsv->bhtv', A, vi) + jnp.einsum('bhtd,bhdv->bhtv', qi * jnp.exp(b), S)
        S = jnp.exp(b_last)[..., 0, :, None] * S + jnp.einsum('bhsd,bhsv->bhdv', ki * jnp.exp(b_last - b), vi)
        return S, o

    S, o = lax.scan(step, s0.astype(jnp.float32), (to_chunks(q), to_chunks(k), to_chunks(v), to_chunks(log_a)))
    return o.transpose(1, 2, 0, 3, 4).reshape(B, H, N, dv).astype(v.dtype), S


def gla_final_state(k, v, log_a):
    b = jnp.cumsum(log_a.astype(jnp.float32), axis=2)
    w = jnp.exp(b[:, :, -1:, :] - b)
    return jnp.einsum('bhsd,bhsv->bhdv', k.astype(jnp.float32) * w, v.astype(jnp.float32))


def gla_bidirectional(q, k, v, la_f, la_b, s_f, s_b):
    o_f, s_f_end = gla_chunked(q, k, v, la_f, s_f)
    o_b, s_b_end = gla_chunked(flip_seq(q), flip_seq(k), flip_seq(v), flip_seq(la_b), s_b)
    return o_f + flip_seq(o_b), s_f_end, s_b_end


def gla_output(o, norm_w, g):
    return merge_heads(rms_norm(o, norm_w)) * jax.nn.silu(g)


def token_mixers(u, uc, w_in, rpb, gla_wa2, gla_ba, gla_norm_w, q_norm_w, k_norm_w, rope_cos, rope_sin, ctx_out):
    B, N, _ = u.shape
    (na_q, na_k, na_v, gl_q, gl_k, gl_v, gl_g, gl_lr, ga_q, ga_k, ga_v) = jnp.split(u @ w_in, IN_SPLITS, axis=-1)
    (na_qc, na_kc, na_vc, gl_qc, gl_kc, gl_vc, gl_gc, gl_lrc, ga_qc, ga_kc, ga_vc) = jnp.split(uc @ w_in, IN_SPLITS, axis=-1)

    k_na_c = to_heads(na_kc, NA_HEADS)
    v_na_c = to_heads(na_vc, NA_HEADS)
    o_na = neighborhood_attention(to_heads(na_q, NA_HEADS), to_heads(na_k, NA_HEADS), to_heads(na_v, NA_HEADS), k_na_c, v_na_c, rpb)

    la_f, la_b = gla_log_gates(gl_lr, gla_wa2, gla_ba)
    la_fc, la_bc = gla_log_gates(gl_lrc, gla_wa2, gla_ba)
    k_gl_c = to_heads(gl_kc, GLA_HEADS)
    v_gl_c = to_heads(gl_vc, GLA_HEADS)
    if ctx_out:
        s0 = jnp.zeros((B, GLA_HEADS, GLA_DK, GLA_DV), jnp.float32)
        q_gl_c = to_heads(gl_qc, GLA_HEADS) * GLA_DK ** -0.5
        oc_gl, s_f, s_b = gla_bidirectional(q_gl_c, k_gl_c, v_gl_c, la_fc, la_bc, s0, s0)
    else:
        s_f = gla_final_state(k_gl_c, v_gl_c, la_fc)
        s_b = gla_final_state(flip_seq(k_gl_c), flip_seq(v_gl_c), flip_seq(la_bc))
    o_gl, _, _ = gla_bidirectional(to_heads(gl_q, GLA_HEADS) * GLA_DK ** -0.5, to_heads(gl_k, GLA_HEADS), to_heads(gl_v, GLA_HEADS), la_f, la_b, s_f, s_b)

    q_ga = apply_axial_rope(rms_norm(to_heads(ga_q, GQA_Q_HEADS), q_norm_w), rope_cos, rope_sin)
    k_ga = apply_axial_rope(rms_norm(to_heads(ga_k, GQA_KV_HEADS), k_norm_w), rope_cos, rope_sin)
    k_ga_c = rms_norm(to_heads(ga_kc, GQA_KV_HEADS), k_norm_w)
    v_ga_c = to_heads(ga_vc, GQA_KV_HEADS)
    o_ga = gqa_latent_attention(q_ga.reshape(B, GQA_KV_HEADS, GQA_REP, N, HEAD_DIM), k_ga, to_heads(ga_v, GQA_KV_HEADS), k_ga_c, v_ga_c)

    o = jnp.concatenate([merge_heads(o_na), gla_output(o_gl, gla_norm_w, gl_g), merge_heads(o_ga)], axis=-1)
    if not ctx_out:
        return o, None

    L = uc.shape[1]
    oc_na = context_attention(to_heads(na_qc, NA_HEADS)[:, :, None], k_na_c, v_na_c)[:, :, 0]
    q_ga_c = rms_norm(to_heads(ga_qc, GQA_Q_HEADS), q_norm_w).reshape(B, GQA_KV_HEADS, GQA_REP, L, HEAD_DIM)
    oc_ga = context_attention(q_ga_c, k_ga_c, v_ga_c).reshape(B, GQA_Q_HEADS, L, HEAD_DIM)
    oc = jnp.concatenate([merge_heads(oc_na), gla_output(oc_gl, gla_norm_w, gl_gc), merge_heads(oc_ga)], axis=-1)
    return o, oc


def swiglu(u, w_ffn_in, w_ffn_out):
    a, b = jnp.split(u @ w_ffn_in, 2, axis=-1)
    return (jax.nn.silu(a) * b) @ w_ffn_out


def setup_inputs(seed: int = 0) -> dict:
    key = jax.random.key(seed)
    ks = jax.random.split(key, 20)

    def nrm(k, shape, scale):
        return jax.random.normal(k, shape, jnp.float32) * scale

    L = DEPTH
    return {
        'x': nrm(ks[0], (BATCH, SEQ, D_MODEL), 1.0),
        'c': nrm(ks[1], (BATCH, D_MODEL), 1.0),
        'ctx': nrm(ks[2], (BATCH, CTX_LEN, D_MODEL), 1.0),
        'c_ctx': nrm(ks[3], (D_MODEL,), 1.0),
        'w_ada': nrm(ks[4], (L, D_MODEL, 6 * D_MODEL), 0.5 * D_MODEL ** -0.5),
        'b_ada': nrm(ks[5], (L, 6 * D_MODEL), 0.02),
        'w_in': nrm(ks[6], (L, D_MODEL, IN_COLS), D_MODEL ** -0.5),
        'na_rpb': nrm(ks[7], (L, NA_HEADS, 2 * NA_WIN_ROWS - 1, 2 * NA_WIN_COLS - 1), 0.1),
        'gla_wa2': nrm(ks[8], (L, 2, GLA_RANK, GLA_KW), GLA_RANK ** -0.5),
        'gla_ba': nrm(ks[9], (L, 2, GLA_KW), 0.02),
        'gla_norm_w': 1.0 + nrm(ks[10], (L, GLA_DV), 0.05),
        'gqa_qnorm_w': 1.0 + nrm(ks[11], (L, HEAD_DIM), 0.05),
        'gqa_knorm_w': 1.0 + nrm(ks[12], (L, HEAD_DIM), 0.05),
        'w_out': nrm(ks[13], (L, MIX_W, D_MODEL), DEEPNORM_BETA * MIX_W ** -0.5),
        'ln1_g': 1.0 + nrm(ks[14], (L, D_MODEL), 0.05),
        'ln1_b': nrm(ks[15], (L, D_MODEL), 0.02),
        'w_ffn_in': nrm(ks[16], (L, D_MODEL, 2 * FFN_HIDDEN), D_MODEL ** -0.5),
        'w_ffn_out': nrm(ks[17], (L, FFN_HIDDEN, D_MODEL), DEEPNORM_BETA * FFN_HIDDEN ** -0.5),
        'ln2_g': 1.0 + nrm(ks[18], (L, D_MODEL), 0.05),
        'ln2_b': nrm(ks[19], (L, D_MODEL), 0.02),
    }


def reference(x, c, ctx, c_ctx, w_ada, b_ada, w_in, na_rpb, gla_wa2, gla_ba, gla_norm_w, gqa_qnorm_w, gqa_knorm_w, w_out, ln1_g, ln1_b, w_ffn_in, w_ffn_out, ln2_g, ln2_b):
    rope_cos, rope_sin = axial_rope_tables(x.shape[1])
    silu_c = jax.nn.silu(c)
    silu_cc = jax.nn.silu(c_ctx)
    xc = ctx
    for l in range(DEPTH):
        ctx_out = l < DEPTH - 1
        mod = (silu_c @ w_ada[l] + b_ada[l])[:, None, :]
        mod_c = silu_cc @ w_ada[l] + b_ada[l]
        sh1, sc1, g1, sh2, sc2, g2 = jnp.split(mod, 6, axis=-1)
        sh1c, sc1c, g1c, sh2c, sc2c, g2c = jnp.split(mod_c, 6, axis=-1)

        o, oc = token_mixers(modulate(x, sh1, sc1), modulate(xc, sh1c, sc1c), w_in[l], na_rpb[l], gla_wa2[l], gla_ba[l], gla_norm_w[l], gqa_qnorm_w[l], gqa_knorm_w[l], rope_cos, rope_sin, ctx_out)
        x = layer_norm(DEEPNORM_ALPHA * x + g1 * (o @ w_out[l]), ln1_g[l], ln1_b[l])
        x = layer_norm(DEEPNORM_ALPHA * x + g2 * swiglu(modulate(x, sh2, sc2), w_ffn_in[l], w_ffn_out[l]), ln2_g[l], ln2_b[l])
        if ctx_out:
            xc = layer_norm(DEEPNORM_ALPHA * xc + g1c * (oc @ w_out[l]), ln1_g[l], ln1_b[l])
            xc = layer_norm(DEEPNORM_ALPHA * xc + g2c * swiglu(modulate(xc, sh2c, sc2c), w_ffn_in[l], w_ffn_out[l]), ln2_g[l], ln2_b[l])
    return x
```

```python
import functools

import numpy as np
import jax
import jax.numpy as jnp
from jax import lax
from jax.experimental import pallas as pl
from jax.experimental.pallas import tpu as pltpu

F32 = jnp.float32
BF16 = jnp.bfloat16

D_MODEL = 1024
DEPTH = 2
GRID_W = 64
HEAD_DIM = 64
LANES = 128

NA_HEADS = 6
NA_WIN_ROWS = 8
NA_WIN_COLS = 16
NA_W = NA_HEADS * HEAD_DIM
NA_PAIRS = NA_W // LANES
NA_DR = 2 * NA_WIN_ROWS - 1
NA_DC = 2 * NA_WIN_COLS - 1

GLA_HEADS = 4
GLA_DK = 32
GLA_DV = 64
GLA_RANK = 16
GLA_GATE_NORM = 16.0
GLA_CHUNK = 64
GLA_KW = GLA_HEADS * GLA_DK
GLA_VW = GLA_HEADS * GLA_DV
GLA_LEVELS = 6

GQA_Q_HEADS = 6
GQA_KV_HEADS = 2
GQA_REP = GQA_Q_HEADS // GQA_KV_HEADS
GQA_QW = GQA_Q_HEADS * HEAD_DIM
GQA_KVW = GQA_KV_HEADS * HEAD_DIM
GQA_PAIRS = GQA_QW // LANES
ROPE_THETA = 10000.0
ROPE_AXIS_PAIRS = HEAD_DIM // 4

FFN_HIDDEN = 2816
MIX_W = NA_W + GLA_VW + GQA_QW

DEEPNORM_ALPHA = (2.0 * DEPTH) ** 0.25
LN_EPS = 1e-5
RMS_EPS = 1e-6
NEG_BIG = -1e30

_O_NAQ, _O_NAK, _O_NAV = 0, 384, 768
_O_GLQ, _O_GLK, _O_GLV, _O_GLG, _O_GLLR = 1152, 1280, 1408, 1664, 1920
_O_GAQ, _O_GAK, _O_GAV, _O_END = 1952, 2336, 2464, 2592
_C_NAQ, _C_NAK, _C_NAV, _C_GAQ = 0, 384, 768, 1152
_C_GLQ, _C_GLK, _C_GLV, _C_GLG, _C_Z = 1536, 1664, 1792, 2048, 2304
_C_GAK, _C_GAV, _C_END = 2560, 2688, 2816

VMEM_LIMIT = 48 * 1024 * 1024


def _cparams(n_axes):
    return pltpu.CompilerParams(dimension_semantics=("arbitrary",) * n_axes,
                                vmem_limit_bytes=VMEM_LIMIT)


def _dot(a, b):
    return jnp.dot(a, b, preferred_element_type=F32)


def _dot_nt(a, b):
    return lax.dot_general(a, b, (((1,), (1,)), ((), ())), preferred_element_type=F32)


def _dot_tn(a, b):
    return lax.dot_general(a, b, (((0,), (0,)), ((), ())), preferred_element_type=F32)


def _silu(x):
    return x * jax.nn.sigmoid(x)


def _layer_norm(y, g, b):
    mu = jnp.mean(y, axis=-1, keepdims=True)
    d = y - mu
    var = jnp.mean(d * d, axis=-1, keepdims=True)
    return d * lax.rsqrt(var + LN_EPS) * g + b


def _half_mean_square(y, lo):
    s = y * y
    s_lo = jnp.sum(jnp.where(lo, s, 0.0), axis=-1, keepdims=True)
    s_hi = jnp.sum(jnp.where(lo, 0.0, s), axis=-1, keepdims=True)
    return jnp.where(lo, s_lo, s_hi) * (1.0 / HEAD_DIM)


_ADA_TN = 1024


def _ada_kernel(c_ref, w_ref, b_ref, o_ref):
    s = _silu(c_ref[...])
    o_ref[0] = jnp.dot(s, w_ref[0], precision=lax.Precision.HIGHEST,
                       preferred_element_type=F32) + b_ref[0]


def _ada(cvec, w_ada, b_ada):
    rows = cvec.shape[0]
    depth, d, n6 = w_ada.shape
    return pl.pallas_call(
        _ada_kernel,
        grid=(depth, n6 // _ADA_TN),
        in_specs=[pl.BlockSpec((rows, d), lambda l, j: (0, 0)),
                  pl.BlockSpec((1, d, _ADA_TN), lambda l, j: (l, 0, j)),
                  pl.BlockSpec((1, 1, _ADA_TN), lambda l, j: (l, 0, j))],
        out_specs=pl.BlockSpec((1, rows, _ADA_TN), lambda l, j: (l, 0, j)),
        out_shape=jax.ShapeDtypeStruct((depth, rows, n6), F32),
        compiler_params=_cparams(2),
        name="ada",
    )(cvec, w_ada, b_ada.reshape(depth, 1, n6))


def _fold_kernel(wlr_ref, wa2_ref, o_ref):
    for e in range(2):
        o_ref[0, :, e * GLA_KW:(e + 1) * GLA_KW] = jnp.dot(
            wlr_ref[0, e], wa2_ref[0, e], precision=lax.Precision.HIGHEST,
            preferred_element_type=F32)


def _fold_gate_weights(w_lr, wa2):
    depth = w_lr.shape[0]
    return pl.pallas_call(
        _fold_kernel,
        grid=(depth,),
        in_specs=[pl.BlockSpec((1, 2, D_MODEL, GLA_RANK), lambda l: (l, 0, 0, 0)),
                  pl.BlockSpec((1, 2, GLA_RANK, GLA_KW), lambda l: (l, 0, 0, 0))],
        out_specs=pl.BlockSpec((1, D_MODEL, 2 * GLA_KW), lambda l: (l, 0, 0)),
        out_shape=jax.ShapeDtypeStruct((depth, D_MODEL, 2 * GLA_KW), F32),
        compiler_params=_cparams(1),
        name="fold_gate",
    )(w_lr, wa2)


def _inproj_kernel(x_ref, sc_ref, sh_ref, w_ref, zb_ref, qnw_ref, knw_ref, cos_ref, sin_ref,
                   naq_ref, nak_ref, nav_ref, gaq_ref, glq_ref, glk_ref, glv_ref, glg_ref,
                   gla_ref, gak_ref, gav_ref):
    u = (x_ref[0] * (1.0 + sc_ref[0]) + sh_ref[0]).astype(BF16)
    tm = u.shape[0]
    lane = lax.broadcasted_iota(jnp.int32, (tm, LANES), 1)
    lo = lane < HEAD_DIM
    first16 = (lane & 31) < 16
    cos = cos_ref[...]
    sin = sin_ref[...]

    def seg(a, b):
        return _dot(u, w_ref[:, a:b])

    def norm_rope(y, w):
        yn = y * lax.rsqrt(_half_mean_square(y, lo) + RMS_EPS) * w
        rot = jnp.where(first16, pltpu.roll(yn, LANES - 16, 1), pltpu.roll(yn, 16, 1))
        return yn * cos + rot * sin

    y = seg(_C_NAQ, _C_NAV)
    naq_ref[0] = (y[:, :NA_W] * HEAD_DIM ** -0.5).astype(BF16)
    nak_ref[0] = y[:, NA_W:].astype(BF16)

    y = seg(_C_NAV, _C_GLQ)
    nav_ref[0] = y[:, :NA_W].astype(BF16)
    qnw = qnw_ref[...]
    for r in range(GQA_PAIRS):
        t = norm_rope(y[:, NA_W + r * LANES:NA_W + (r + 1) * LANES], qnw)
        gaq_ref[0, :, r * LANES:(r + 1) * LANES] = (t * HEAD_DIM ** -0.5).astype(BF16)

    y = seg(_C_GLQ, _C_GLG)
    glq_ref[0] = y[:, :GLA_KW] * GLA_DK ** -0.5
    glk_ref[0] = y[:, GLA_KW:2 * GLA_KW]
    glv_ref[0] = y[:, 2 * GLA_KW:].astype(BF16)

    y = seg(_C_GLG, _C_GAK)
    glg_ref[0] = y[:, :GLA_VW]
    z = y[:, GLA_VW:] + zb_ref[...]
    log_sig = jnp.minimum(z, 0.0) - jnp.log1p(jnp.exp(-jnp.abs(z)))
    gla_ref[0] = log_sig * (1.0 / GLA_GATE_NORM)

    y = seg(_C_GAK, _C_END)
    gak_ref[0] = norm_rope(y[:, :LANES], knw_ref[...]).astype(BF16)
    gav_ref[0] = y[:, LANES:].astype(BF16)


def _inproj(x, sc, sh, w, zb, qnw, knw, cos, sin, tm):
    B, N, D = x.shape
    tok = lambda width, dt: jax.ShapeDtypeStruct((B, N, width), dt)
    tspec = lambda width: pl.BlockSpec((1, tm, width), lambda b, i: (b, i, 0))
    vec = lambda width: pl.BlockSpec((1, width), lambda b, i: (0, 0))
    out_shapes = [tok(NA_W, BF16), tok(NA_W, BF16), tok(NA_W, BF16), tok(GQA_QW, BF16),
                  tok(GLA_KW, F32), tok(GLA_KW, F32), tok(GLA_VW, BF16), tok(GLA_VW, F32),
                  tok(2 * GLA_KW, F32), tok(GQA_KVW, BF16), tok(GQA_KVW, BF16)]
    return pl.pallas_call(
        _inproj_kernel,
        grid=(B, N // tm),
        in_specs=[tspec(D),
                  pl.BlockSpec((1, 1, D), lambda b, i: (b, 0, 0)),
                  pl.BlockSpec((1, 1, D), lambda b, i: (b, 0, 0)),
                  pl.BlockSpec((D, _C_END), lambda b, i: (0, 0)),
                  vec(2 * GLA_KW), vec(LANES), vec(LANES),
                  pl.BlockSpec((tm, LANES), lambda b, i: (i, 0)),
                  pl.BlockSpec((tm, LANES), lambda b, i: (i, 0))],
        out_specs=[tspec(s.shape[-1]) for s in out_shapes],
        out_shape=out_shapes,
        compiler_params=_cparams(2),
        name="inproj",
    )(x, sc, sh, w, zb, qnw, knw, cos, sin)


def _nabias_kernel(rpb_ref, o_ref):
    h = pl.program_id(0)
    qc = lax.broadcasted_iota(jnp.int32, (GRID_W, LANES), 0)
    lane = lax.broadcasted_iota(jnp.int32, (GRID_W, LANES), 1)
    kc = lane & (GRID_W - 1)
    hi = lane >= GRID_W
    idx = jnp.clip(kc - qc, -(NA_WIN_COLS - 1), NA_WIN_COLS - 1) + (NA_WIN_COLS - 1)
    start = jnp.clip(qc - NA_WIN_COLS // 2, 0, GRID_W - NA_WIN_COLS)
    col_in = jnp.logical_and(kc >= start, kc < start + NA_WIN_COLS)
    base = h * (NA_DR * NA_DC)

    pair_tables = []
    for dr0 in range(NA_DR - 1):

        def body(j, t, dr0=dr0):
            s0 = rpb_ref[base + dr0 * NA_DC + j]
            s1 = rpb_ref[base + (dr0 + 1) * NA_DC + j]
            return jnp.where(idx == j, jnp.where(hi, s1, s0), t)

        t = lax.fori_loop(0, NA_DC, body, jnp.zeros((GRID_W, LANES), F32))
        pair_tables.append(jnp.where(col_in, t, NEG_BIG))
    for oi in range(NA_WIN_ROWS):
        for i in range(NA_WIN_ROWS // 2):
            o_ref[0, oi, :, i * LANES:(i + 1) * LANES] = pair_tables[oi + 2 * i]


def _na_bias(rpb_flat):
    return pl.pallas_call(
        _nabias_kernel,
        grid=(NA_HEADS,),
        in_specs=[pl.BlockSpec(memory_space=pltpu.SMEM)],
        out_specs=pl.BlockSpec((1, NA_WIN_ROWS, GRID_W, NA_WIN_ROWS * GRID_W),
                               lambda h: (h, 0, 0, 0)),
        out_shape=jax.ShapeDtypeStruct((NA_HEADS, NA_WIN_ROWS, GRID_W, NA_WIN_ROWS * GRID_W), F32),
        compiler_params=_cparams(1),
        name="na_bias",
    )(rpb_flat)


def _na_kernel(q_ref, k_ref, v_ref, kc_ref, vc_ref, b_ref, o_ref):
    n = q_ref.shape[1]
    rows = n // GRID_W
    win = NA_WIN_ROWS * GRID_W
    lane = lax.broadcasted_iota(jnp.int32, (GRID_W, LANES), 1)
    lo = lane < HEAD_DIM
    kc = kc_ref[0]
    vc = vc_ref[0]
    zero = jnp.zeros((GRID_W, LANES), BF16)

    def body(r, carry):
        rs = jnp.clip(r - NA_WIN_ROWS // 2, 0, rows - NA_WIN_ROWS)
        oi = rs - r + (NA_WIN_ROWS - 1)
        q0 = pl.multiple_of(r * GRID_W, GRID_W)
        k0 = pl.multiple_of(rs * GRID_W, GRID_W)
        q = q_ref[0, pl.ds(q0, GRID_W), :]
        kw = k_ref[0, pl.ds(k0, win), :]
        vw = v_ref[0, pl.ds(k0, win), :]
        outs = []
        for h in range(2):
            qm = jnp.where(lo, q, zero) if h == 0 else jnp.where(lo, zero, q)
            s_lat = _dot_nt(qm, kw) + b_ref[h, oi]
            s_ctx = _dot_nt(qm, kc)
            m = jnp.maximum(jnp.max(s_lat, axis=-1, keepdims=True),
                            jnp.max(s_ctx, axis=-1, keepdims=True))
            p_lat = jnp.exp(s_lat - m)
            p_ctx = jnp.exp(s_ctx - m)
            den = jnp.sum(p_lat, axis=-1, keepdims=True) + jnp.sum(p_ctx, axis=-1, keepdims=True)
            o = _dot(p_lat.astype(BF16), vw) + _dot(p_ctx.astype(BF16), vc)
            outs.append(o / den)
        o_ref[0, pl.ds(q0, GRID_W), :] = jnp.where(lo, outs[0], outs[1]).astype(BF16)
        return carry

    lax.fori_loop(0, rows, body, 0)


def _na_attention(q, k, v, kc, vc, bias):
    B, N, _ = q.shape
    L = kc.shape[1]
    lat = pl.BlockSpec((1, N, LANES), lambda b, p: (b, 0, p))
    cx = pl.BlockSpec((1, L, LANES), lambda b, p: (b, 0, p))
    return pl.pallas_call(
        _na_kernel,
        grid=(B, NA_PAIRS),
        in_specs=[lat, lat, lat, cx, cx,
                  pl.BlockSpec((2, NA_WIN_ROWS, GRID_W, NA_WIN_ROWS * GRID_W),
                               lambda b, p: (p, 0, 0, 0))],
        out_specs=lat,
        out_shape=jax.ShapeDtypeStruct((B, N, NA_W), BF16),
        compiler_params=_cparams(2),
        name="na_attn",
    )(q, k, v, kc, vc, bias)


def _pair_attn_kernel(*refs, two_sources):
    if two_sources:
        q_ref, ka_ref, va_ref, kb_ref, vb_ref, o_ref = refs
    else:
        q_ref, ka_ref, va_ref, o_ref = refs
    q = q_ref[0]
    tq = q.shape[0]
    lane = lax.broadcasted_iota(jnp.int32, (tq, LANES), 1)
    lo = lane < HEAD_DIM
    zero = jnp.zeros((tq, LANES), BF16)
    ka = ka_ref[0]
    va = va_ref[0]
    outs = []
    for h in range(2):
        qm = jnp.where(lo, q, zero) if h == 0 else jnp.where(lo, zero, q)
        s_a = _dot_nt(qm, ka)
        m = jnp.max(s_a, axis=-1, keepdims=True)
        if two_sources:
            s_b = _dot_nt(qm, kb_ref[0])
            m = jnp.maximum(m, jnp.max(s_b, axis=-1, keepdims=True))
        p_a = jnp.exp(s_a - m)
        den = jnp.sum(p_a, axis=-1, keepdims=True)
        o = _dot(p_a.astype(BF16), va)
        if two_sources:
            p_b = jnp.exp(s_b - m)
            den = den + jnp.sum(p_b, axis=-1, keepdims=True)
            o = o + _dot(p_b.astype(BF16), vb_ref[0])
        outs.append(o / den)
    o_ref[0] = jnp.where(lo, outs[0], outs[1]).astype(BF16)


def _pair_attention(q, ka, va, kb=None, vb=None, *, shared_kv, tq):
    B, Nq, W = q.shape
    pairs = W // LANES
    kv_map = (lambda b, p, i: (b, 0, 0)) if shared_kv else (lambda b, p, i: (b, 0, p))
    qspec = pl.BlockSpec((1, tq, LANES), lambda b, p, i: (b, i, p))
    in_specs = [qspec,
                pl.BlockSpec((1, ka.shape[1], LANES), kv_map),
                pl.BlockSpec((1, ka.shape[1], LANES), kv_map)]
    args = [q, ka, va]
    if kb is not None:
        in_specs += [pl.BlockSpec((1, kb.shape[1], LANES), kv_map),
                     pl.BlockSpec((1, kb.shape[1], LANES), kv_map)]
        args += [kb, vb]
    return pl.pallas_call(
        functools.partial(_pair_attn_kernel, two_sources=kb is not None),
        grid=(B, pairs, Nq // tq),
        in_specs=in_specs,
        out_specs=qspec,
        out_shape=jax.ShapeDtypeStruct((B, Nq, W), BF16),
        compiler_params=_cparams(3),
        name="pair_attn",
    )(*args)


def _gla_constants():
    c = GLA_CHUNK
    t = np.arange(c)[:, None]
    i = np.arange(c)[None, :]
    mcat = np.zeros((2, 2 * GLA_LEVELS * c, c), np.float32)
    masks = np.zeros((2, GLA_LEVELS + 1, c, c), np.float32)
    for d in range(2):
        for l in range(1, GLA_LEVELS + 1):
            m = 2 ** l
            same = (t // m) == (i // m)
            q_side = same & ((i <= t) if d == 0 else (i >= t))
            k_side = same & ((i > t) if d == 0 else (i < t))
            mcat[d, (l - 1) * c:l * c] = q_side
            mcat[d, (GLA_LEVELS + l - 1) * c:(GLA_LEVELS + l) * c] = k_side
        masks[d, 0] = (t == i)
        for l in range(GLA_LEVELS):
            m = 2 ** l
            tb, sb = t // m, i // m
            if d == 0:
                masks[d, l + 1] = (tb % 2 == 1) & (sb == tb - 1)
            else:
                masks[d, l + 1] = (tb % 2 == 0) & (sb == tb + 1)
    masks = np.tile(masks, (1, 1, 1, GLA_HEADS))
    rows = np.arange(GLA_VW)[:, None] // GLA_DV
    cols = np.arange(GLA_KW)[None, :] // GLA_DK
    stmask = (rows == cols).astype(np.float32)
    return jnp.asarray(mcat, BF16), jnp.asarray(masks, F32), jnp.asarray(stmask, F32)


def _gla_chunk(c, d, q_ref, k_ref, v_ref, la_ref, acc_ref, st_ref, mcat_ref, lmask_ref, stmask_ref,
               accumulate):
    ck = GLA_CHUNK
    r0 = pl.multiple_of(c * ck, ck)
    q = q_ref[0, pl.ds(r0, ck), :]
    k = k_ref[0, pl.ds(r0, ck), :]
    v = v_ref[0, pl.ds(r0, ck), :]
    g = la_ref[0, pl.ds(r0, ck), d * GLA_KW:(d + 1) * GLA_KW]

    g1 = g.astype(BF16)
    r1 = g - g1.astype(F32)
    g2 = r1.astype(BF16)
    g3 = (r1 - g2.astype(F32)).astype(BF16)
    mc = mcat_ref[d]
    e = _dot(mc, g1) + _dot(mc, g2) + _dot(mc, g3)

    def q_exp(l):
        return g if l == 0 else e[(l - 1) * ck:l * ck]

    def k_exp(l):
        return e[(GLA_LEVELS + l - 1) * ck:(GLA_LEVELS + l) * ck]

    lane_k = lax.broadcasted_iota(jnp.int32, (1, GLA_KW), 1) // GLA_DK
    lane_v = lax.broadcasted_iota(jnp.int32, (1, GLA_VW), 1) // GLA_DV

    def block_rows(x, lane_head):
        return jnp.concatenate([jnp.where(lane_head == h, x, 0.0) for h in range(GLA_HEADS)], axis=0)

    def level_scores(qt, kt, idx):
        kb = block_rows(kt, lane_k).astype(BF16)
        return lmask_ref[d, idx] * _dot_nt(qt.astype(BF16), kb)

    a = level_scores(q, k, 0)
    for l in range(GLA_LEVELS):
        kt = k if l == 0 else k * jnp.exp(k_exp(l))
        a = a + level_scores(q * jnp.exp(q_exp(l)), kt, l + 1)

    vbd = block_rows(v.astype(F32), lane_v).astype(BF16)
    o = _dot(a.astype(BF16), vbd)

    st = st_ref[...]
    o = o + _dot_nt((q * jnp.exp(q_exp(GLA_LEVELS))).astype(BF16), st.astype(BF16))

    ks = (k * jnp.exp(k_exp(GLA_LEVELS))).astype(BF16)
    last = (GLA_LEVELS - 1) * ck + (ck - 1 if d == 0 else 0)
    total = e[last:last + 1]
    st_ref[...] = st * jnp.exp(total) + stmask_ref[...] * _dot_tn(v, ks)

    if accumulate:
        acc_ref[pl.ds(r0, ck), :] += o
    else:
        acc_ref[pl.ds(r0, ck), :] = o


def _gla_finish(acc_ref, g_ref, nw_ref, o_ref, tile):
    n = acc_ref.shape[0]
    lane = lax.broadcasted_iota(jnp.int32, (tile, LANES), 1)
    lo = lane < GLA_DV
    nw = nw_ref[...]

    def body(i, carry):
        r0 = pl.multiple_of(i * tile, tile)
        for j in range(GLA_VW // LANES):
            o = acc_ref[pl.ds(r0, tile), j * LANES:(j + 1) * LANES]
            on = o * lax.rsqrt(_half_mean_square(o, lo) + RMS_EPS) * nw
            gate = _silu(g_ref[0, pl.ds(r0, tile), j * LANES:(j + 1) * LANES])
            o_ref[0, pl.ds(r0, tile), j * LANES:(j + 1) * LANES] = (on * gate).astype(BF16)
        return carry

    lax.fori_loop(0, n // tile, body, 0)


def _gla_kernel(q_ref, k_ref, v_ref, la_ref, g_ref, qc_ref, kc_ref, vc_ref, lac_ref, gc_ref,
                nw_ref, mcat_ref, lmask_ref, stmask_ref, o_ref, oc_ref, acc_ref, accc_ref, st_ref):
    n_chunks = q_ref.shape[1] // GLA_CHUNK
    c_chunks = qc_ref.shape[1] // GLA_CHUNK
    consts = (mcat_ref, lmask_ref, stmask_ref)
    for d in range(2):
        st_ref[...] = jnp.zeros_like(st_ref)
        acc = d == 1

        def ctx_step(i, carry, d=d, acc=acc):
            c = i if d == 0 else c_chunks - 1 - i
            _gla_chunk(c, d, qc_ref, kc_ref, vc_ref, lac_ref, accc_ref, st_ref, *consts, accumulate=acc)
            return carry

        def lat_step(i, carry, d=d, acc=acc):
            c = i if d == 0 else n_chunks - 1 - i
            _gla_chunk(c, d, q_ref, k_ref, v_ref, la_ref, acc_ref, st_ref, *consts, accumulate=acc)
            return carry

        lax.fori_loop(0, c_chunks, ctx_step, 0)
        lax.fori_loop(0, n_chunks, lat_step, 0)
    _gla_finish(acc_ref, g_ref, nw_ref, o_ref, 256)
    _gla_finish(accc_ref, gc_ref, nw_ref, oc_ref, 256)


def _gla(q, k, v, la, g, qc, kc, vc, lac, gc, nw, consts):
    B, N, _ = q.shape
    L = qc.shape[1]
    mcat, lmask, stmask = consts
    tok = lambda n, w: pl.BlockSpec((1, n, w), lambda b: (b, 0, 0))
    full = lambda a: pl.BlockSpec(a.shape, lambda b: (0,) * a.ndim)
    return pl.pallas_call(
        _gla_kernel,
        grid=(B,),
        in_specs=[tok(N, GLA_KW), tok(N, GLA_KW), tok(N, GLA_VW), tok(N, 2 * GLA_KW), tok(N, GLA_VW),
                  tok(L, GLA_KW), tok(L, GLA_KW), tok(L, GLA_VW), tok(L, 2 * GLA_KW), tok(L, GLA_VW),
                  full(nw), full(mcat), full(lmask), full(stmask)],
        out_specs=[tok(N, GLA_VW), tok(L, GLA_VW)],
        out_shape=[jax.ShapeDtypeStruct((B, N, GLA_VW), BF16), jax.ShapeDtypeStruct((B, L, GLA_VW), BF16)],
        scratch_shapes=[pltpu.VMEM((N, GLA_VW), F32), pltpu.VMEM((L, GLA_VW), F32),
                        pltpu.VMEM((GLA_VW, GLA_KW), F32)],
        compiler_params=_cparams(1),
        name="gla",
    )(q, k, v, la, g, qc, kc, vc, lac, gc, nw, mcat, lmask, stmask)


def _outproj_kernel(x_ref, na_ref, gl_ref, ga_ref, w_ref, g1_ref, lg_ref, lb_ref, o_ref):
    o = jnp.concatenate([na_ref[0], gl_ref[0], ga_ref[0]], axis=-1)
    y = DEEPNORM_ALPHA * x_ref[0] + g1_ref[0] * _dot(o, w_ref[...])
    o_ref[0] = _layer_norm(y, lg_ref[...], lb_ref[...])


def _outproj(x, o_na, o_gl, o_ga, w, g1, lg, lb, tm):
    B, N, D = x.shape
    tspec = lambda width: pl.BlockSpec((1, tm, width), lambda b, i: (b, i, 0))
    vec = pl.BlockSpec((1, D), lambda b, i: (0, 0))
    return pl.pallas_call(
        _outproj_kernel,
        grid=(B, N // tm),
        in_specs=[tspec(D), tspec(NA_W), tspec(GLA_VW), tspec(GQA_QW),
                  pl.BlockSpec((MIX_W, D), lambda b, i: (0, 0)),
                  pl.BlockSpec((1, 1, D), lambda b, i: (b, 0, 0)), vec, vec],
        out_specs=tspec(D),
        out_shape=jax.ShapeDtypeStruct((B, N, D), F32),
        compiler_params=_cparams(2),
        name="outproj",
    )(x, o_na, o_gl, o_ga, w, g1, lg, lb)


_FFN_CHUNKS = ((0, 1024), (1024, 2048), (2048, FFN_HIDDEN))


def _ffn_kernel(x_ref, sc_ref, sh_ref, g2_ref, wi_ref, wo_ref, lg_ref, lb_ref, o_ref):
    x = x_ref[0]
    u = (x * (1.0 + sc_ref[0]) + sh_ref[0]).astype(BF16)
    acc = None
    for a0, a1 in _FFN_CHUNKS:
        ha = _dot(u, wi_ref[:, a0:a1])
        hb = _dot(u, wi_ref[:, FFN_HIDDEN + a0:FFN_HIDDEN + a1])
        part = _dot((_silu(ha) * hb).astype(BF16), wo_ref[a0:a1, :])
        acc = part if acc is None else acc + part
    y = DEEPNORM_ALPHA * x + g2_ref[0] * acc
    o_ref[0] = _layer_norm(y, lg_ref[...], lb_ref[...])


def _ffn(x, sc, sh, g2, wi, wo, lg, lb, tm):
    B, N, D = x.shape
    tspec = pl.BlockSpec((1, tm, D), lambda b, i: (b, i, 0))
    mod = pl.BlockSpec((1, 1, D), lambda b, i: (b, 0, 0))
    vec = pl.BlockSpec((1, D), lambda b, i: (0, 0))
    resident = lambda a: pl.BlockSpec(a.shape, lambda b, i: (0, 0), pipeline_mode=pl.Buffered(1))
    return pl.pallas_call(
        _ffn_kernel,
        grid=(B, N // tm),
        in_specs=[tspec, mod, mod, mod, resident(wi), resident(wo), vec, vec],
        out_specs=tspec,
        out_shape=jax.ShapeDtypeStruct((B, N, D), F32),
        compiler_params=_cparams(2),
        name="ffn",
    )(x, sc, sh, g2, wi, wo, lg, lb)


def _rope_tables(n):
    t = jnp.arange(n)
    row = (t // GRID_W).astype(F32)
    col = (t % GRID_W).astype(F32)
    inv_freq = ROPE_THETA ** (-jnp.arange(ROPE_AXIS_PAIRS, dtype=F32) / ROPE_AXIS_PAIRS)
    ang_r = row[:, None] * inv_freq
    ang_c = col[:, None] * inv_freq
    ang = jnp.concatenate([ang_r, ang_r, ang_c, ang_c], axis=-1)
    sign = jnp.where((jnp.arange(HEAD_DIM) % 32) < 16, -1.0, 1.0).astype(F32)
    cos = jnp.tile(jnp.cos(ang), (1, 2))
    sin = jnp.tile(jnp.sin(ang) * sign, (1, 2))
    return cos, sin


def _pair_major(w, axis):
    shape = w.shape
    lead, tail = shape[:axis], shape[axis + 1:]
    w = w.reshape(lead + (GQA_KV_HEADS, GQA_REP, HEAD_DIM) + tail)
    w = jnp.swapaxes(w, axis, axis + 1)
    return w.reshape(shape)


def kernel(x, c, ctx, c_ctx, w_ada, b_ada, w_in, na_rpb, gla_wa2, gla_ba, gla_norm_w, gqa_qnorm_w,
           gqa_knorm_w, w_out, ln1_g, ln1_b, w_ffn_in, w_ffn_out, ln2_g, ln2_b):
    B, N, D = x.shape
    L = ctx.shape[1]
    depth = w_in.shape[0]
    tm = 512
    tmc = min(L, 512)

    pad = (-(B + 1)) % 8
    cvec = jnp.concatenate([c, c_ctx[None, :], jnp.zeros((pad, D), F32)], axis=0)
    mods = _ada(cvec, w_ada, b_ada)

    w_lr = w_in[:, :, _O_GLLR:_O_GAQ].reshape(depth, D, 2, GLA_RANK).transpose(0, 2, 1, 3)
    w_z = _fold_gate_weights(w_lr, gla_wa2)

    cos, sin = _rope_tables(N)
    cos_c = jnp.ones((L, LANES), F32)
    sin_c = jnp.zeros((L, LANES), F32)
    gla_consts = _gla_constants()

    xc = ctx
    for l in range(depth):
        ctx_out = l < depth - 1
        m_lat = mods[l, :B].reshape(B, 6, 1, D)
        m_ctx = jnp.broadcast_to(mods[l, B].reshape(1, 6, 1, D), (B, 6, 1, D))
        sh1, sc1, g1, sh2, sc2, g2 = (m_lat[:, i] for i in range(6))
        sh1c, sc1c, g1c, sh2c, sc2c, g2c = (m_ctx[:, i] for i in range(6))

        wl = w_in[l]
        w_proj = jnp.concatenate(
            [wl[:, _O_NAQ:_O_GLQ], _pair_major(wl[:, _O_GAQ:_O_GAK], 1), wl[:, _O_GLQ:_O_GLLR],
             w_z[l], wl[:, _O_GAK:_O_END]], axis=1).astype(BF16)
        zb = gla_ba[l].reshape(1, 2 * GLA_KW)
        qnw = jnp.tile(gqa_qnorm_w[l], 2).reshape(1, LANES)
        knw = jnp.tile(gqa_knorm_w[l], 2).reshape(1, LANES)
        glnw = jnp.tile(gla_norm_w[l], 2).reshape(1, LANES)
        wo_l = w_out[l]
        w_o = jnp.concatenate([wo_l[:NA_W + GLA_VW], _pair_major(wo_l[NA_W + GLA_VW:], 0)], axis=0).astype(BF16)
        lg1, lb1 = ln1_g[l].reshape(1, D), ln1_b[l].reshape(1, D)
        lg2, lb2 = ln2_g[l].reshape(1, D), ln2_b[l].reshape(1, D)
        wi = w_ffn_in[l].astype(BF16)
        wo = w_ffn_out[l].astype(BF16)

        (na_q, na_k, na_v, ga_q, gl_q, gl_k, gl_v, gl_g, gl_la, ga_k, ga_v) = _inproj(
            x, sc1, sh1, w_proj, zb, qnw, knw, cos, sin, tm)
        (na_qc, na_kc, na_vc, ga_qc, gl_qc, gl_kc, gl_vc, gl_gc, gl_lac, ga_kc, ga_vc) = _inproj(
            xc, sc1c, sh1c, w_proj, zb, qnw, knw, cos_c, sin_c, tmc)

        bias = _na_bias(na_rpb[l].reshape(-1))
        o_na = _na_attention(na_q, na_k, na_v, na_kc, na_vc, bias)
        o_gl, oc_gl = _gla(gl_q, gl_k, gl_v, gl_la, gl_g, gl_qc, gl_kc, gl_vc, gl_lac, gl_gc, glnw, gla_consts)
        o_ga = _pair_attention(ga_q, ga_k, ga_v, ga_kc, ga_vc, shared_kv=True, tq=256)

        x = _outproj(x, o_na, o_gl, o_ga, w_o, g1, lg1, lb1, tm)
        x = _ffn(x, sc2, sh2, g2, wi, wo, lg2, lb2, tm)

        if ctx_out:
            oc_na = _pair_attention(na_qc, na_kc, na_vc, shared_kv=False, tq=L)
            oc_ga = _pair_attention(ga_qc, ga_kc, ga_vc, shared_kv=True, tq=L)
            xc = _outproj(xc, oc_na, oc_gl, oc_ga, w_o, g1c, lg1, lb1, tmc)
            xc = _ffn(xc, sc2c, sh2c, g2c, wi, wo, lg2, lb2, tmc)
    return x
```

```python
import functools

import numpy as np
import jax
import jax.numpy as jnp
from jax import lax
from jax.experimental import pallas as pl
from jax.experimental.pallas import tpu as pltpu

F32 = jnp.float32
BF16 = jnp.bfloat16

D_MODEL = 1024
DEPTH = 2
GRID_W = 64
HEAD_DIM = 64
LANES = 128

NA_HEADS = 6
NA_WIN_ROWS = 8
NA_WIN_COLS = 16
NA_W = NA_HEADS * HEAD_DIM
NA_PAIRS = NA_W // LANES
NA_DR = 2 * NA_WIN_ROWS - 1
NA_DC = 2 * NA_WIN_COLS - 1
NA_ROWS_PER_STEP = 8

GLA_HEADS = 4
GLA_DK = 32
GLA_DV = 64
GLA_RANK = 16
GLA_GATE_NORM = 16.0
GLA_CHUNK = 64
GLA_KW = GLA_HEADS * GLA_DK
GLA_VW = GLA_HEADS * GLA_DV
GLA_LEVELS = 6
GLA_CHUNKS_PER_STEP = 2

GQA_Q_HEADS = 6
GQA_KV_HEADS = 2
GQA_REP = GQA_Q_HEADS // GQA_KV_HEADS
GQA_QW = GQA_Q_HEADS * HEAD_DIM
GQA_KVW = GQA_KV_HEADS * HEAD_DIM
GQA_PAIRS = GQA_QW // LANES
PAIR_ATTN_SUB_ROWS = 128
ROPE_THETA = 10000.0
ROPE_AXIS_PAIRS = HEAD_DIM // 4

FFN_HIDDEN = 2816
MIX_W = NA_W + GLA_VW + GQA_QW

DEEPNORM_ALPHA = (2.0 * DEPTH) ** 0.25
LN_EPS = 1e-5
RMS_EPS = 1e-6
NEG_BIG = -1e30
LOG2_E = 1.4426950408889634
Q_SCALE = HEAD_DIM ** -0.5 * LOG2_E

_O_NAQ, _O_NAK, _O_NAV = 0, 384, 768
_O_GLQ, _O_GLK, _O_GLV, _O_GLG, _O_GLLR = 1152, 1280, 1408, 1664, 1920
_O_GAQ, _O_GAK, _O_GAV, _O_END = 1952, 2336, 2464, 2592
_C_NAQ, _C_NAK, _C_NAV, _C_GAQ = 0, 384, 768, 1152
_C_GLQ, _C_GLK, _C_GLV, _C_GLG, _C_Z = 1536, 1664, 1792, 2048, 2304
_C_GAK, _C_GAV, _C_END = 2560, 2688, 2816

VMEM_LIMIT = 48 * 1024 * 1024


def _cparams(n_axes):
    return pltpu.CompilerParams(dimension_semantics=("arbitrary",) * n_axes,
                                vmem_limit_bytes=VMEM_LIMIT)


def _dot(a, b):
    return jnp.dot(a, b, preferred_element_type=F32)


def _dot_nt(a, b):
    return lax.dot_general(a, b, (((1,), (1,)), ((), ())), preferred_element_type=F32)


def _dot_tn(a, b):
    return lax.dot_general(a, b, (((0,), (0,)), ((), ())), preferred_element_type=F32)


def _silu(x):
    return x * jax.nn.sigmoid(x)


def _layer_norm(y, g, b):
    mu = jnp.mean(y, axis=-1, keepdims=True)
    d = y - mu
    var = jnp.mean(d * d, axis=-1, keepdims=True)
    return d * lax.rsqrt(var + LN_EPS) * g + b


def _half_mean_square(y, lo):
    s = y * y
    s_lo = jnp.sum(jnp.where(lo, s, 0.0), axis=-1, keepdims=True)
    s_hi = jnp.sum(jnp.where(lo, 0.0, s), axis=-1, keepdims=True)
    return jnp.where(lo, s_lo, s_hi) * (1.0 / HEAD_DIM)


_ADA_TN = 1024


def _ada_kernel(c_ref, w_ref, b_ref, o_ref):
    s = _silu(c_ref[...])
    o_ref[0] = jnp.dot(s, w_ref[0], precision=lax.Precision.HIGHEST,
                       preferred_element_type=F32) + b_ref[0]


def _ada(cvec, w_ada, b_ada):
    rows = cvec.shape[0]
    depth, d, n6 = w_ada.shape
    return pl.pallas_call(
        _ada_kernel,
        grid=(depth, n6 // _ADA_TN),
        in_specs=[pl.BlockSpec((rows, d), lambda l, j: (0, 0)),
                  pl.BlockSpec((1, d, _ADA_TN), lambda l, j: (l, 0, j)),
                  pl.BlockSpec((1, 1, _ADA_TN), lambda l, j: (l, 0, j))],
        out_specs=pl.BlockSpec((1, rows, _ADA_TN), lambda l, j: (l, 0, j)),
        out_shape=jax.ShapeDtypeStruct((depth, rows, n6), F32),
        compiler_params=_cparams(2),
        name="ada",
    )(cvec, w_ada, b_ada.reshape(depth, 1, n6))


def _fold_kernel(wlr_ref, wa2_ref, o_ref):
    for e in range(2):
        o_ref[0, :, e * GLA_KW:(e + 1) * GLA_KW] = jnp.dot(
            wlr_ref[0, e], wa2_ref[0, e], precision=lax.Precision.HIGHEST,
            preferred_element_type=F32)


def _fold_gate_weights(w_lr, wa2):
    depth = w_lr.shape[0]
    return pl.pallas_call(
        _fold_kernel,
        grid=(depth,),
        in_specs=[pl.BlockSpec((1, 2, D_MODEL, GLA_RANK), lambda l: (l, 0, 0, 0)),
                  pl.BlockSpec((1, 2, GLA_RANK, GLA_KW), lambda l: (l, 0, 0, 0))],
        out_specs=pl.BlockSpec((1, D_MODEL, 2 * GLA_KW), lambda l: (l, 0, 0)),
        out_shape=jax.ShapeDtypeStruct((depth, D_MODEL, 2 * GLA_KW), F32),
        compiler_params=_cparams(1),
        name="fold_gate",
    )(w_lr, wa2)


def _inproj_kernel(x_ref, sc_ref, sh_ref, w_ref, zb_ref, qnw_ref, knw_ref, cos_ref, sin_ref,
                   naq_ref, nak_ref, nav_ref, gaq_ref, glq_ref, glk_ref, glv_ref, glg_ref,
                   gla_ref, gak_ref, gav_ref):
    u = (x_ref[0] * (1.0 + sc_ref[0]) + sh_ref[0]).astype(BF16)
    tm = u.shape[0]
    lane = lax.broadcasted_iota(jnp.int32, (tm, LANES), 1)
    lo = lane < HEAD_DIM
    first16 = (lane & 31) < 16
    cos = cos_ref[...]
    sin = sin_ref[...]

    def seg(a, b):
        return _dot(u, w_ref[:, a:b])

    def norm_rope(y, w):
        yn = y * lax.rsqrt(_half_mean_square(y, lo) + RMS_EPS) * w
        rot = jnp.where(first16, pltpu.roll(yn, LANES - 16, 1), pltpu.roll(yn, 16, 1))
        return yn * cos + rot * sin

    y = seg(_C_NAQ, _C_NAV)
    naq_ref[0] = (y[:, :NA_W] * Q_SCALE).astype(BF16)
    nak_ref[0] = y[:, NA_W:].astype(BF16)

    y = seg(_C_NAV, _C_GLQ)
    nav_ref[0] = y[:, :NA_W].astype(BF16)
    qnw = qnw_ref[...]
    for r in range(GQA_PAIRS):
        t = norm_rope(y[:, NA_W + r * LANES:NA_W + (r + 1) * LANES], qnw)
        gaq_ref[0, :, r * LANES:(r + 1) * LANES] = (t * Q_SCALE).astype(BF16)

    y = seg(_C_GLQ, _C_GLG)
    glq_ref[0] = y[:, :GLA_KW] * GLA_DK ** -0.5
    glk_ref[0] = y[:, GLA_KW:2 * GLA_KW]
    glv_ref[0] = y[:, 2 * GLA_KW:].astype(BF16)

    y = seg(_C_GLG, _C_GAK)
    glg_ref[0] = y[:, :GLA_VW]
    z = y[:, GLA_VW:] + zb_ref[...]
    log_sig = jnp.minimum(z, 0.0) - jnp.log1p(jnp.exp(-jnp.abs(z)))
    gla_ref[0] = log_sig * (1.0 / GLA_GATE_NORM)

    y = seg(_C_GAK, _C_END)
    gak_ref[0] = norm_rope(y[:, :LANES], knw_ref[...]).astype(BF16)
    gav_ref[0] = y[:, LANES:].astype(BF16)


def _inproj(x, sc, sh, w, zb, qnw, knw, cos, sin, tm):
    B, N, D = x.shape
    tok = lambda width, dt: jax.ShapeDtypeStruct((B, N, width), dt)
    tspec = lambda width: pl.BlockSpec((1, tm, width), lambda b, i: (b, i, 0))
    vec = lambda width: pl.BlockSpec((1, width), lambda b, i: (0, 0))
    out_shapes = [tok(NA_W, BF16), tok(NA_W, BF16), tok(NA_W, BF16), tok(GQA_QW, BF16),
                  tok(GLA_KW, F32), tok(GLA_KW, F32), tok(GLA_VW, BF16), tok(GLA_VW, F32),
                  tok(2 * GLA_KW, F32), tok(GQA_KVW, BF16), tok(GQA_KVW, BF16)]
    return pl.pallas_call(
        _inproj_kernel,
        grid=(B, N // tm),
        in_specs=[tspec(D),
                  pl.BlockSpec((1, 1, D), lambda b, i: (b, 0, 0)),
                  pl.BlockSpec((1, 1, D), lambda b, i: (b, 0, 0)),
                  pl.BlockSpec((D, _C_END), lambda b, i: (0, 0)),
                  vec(2 * GLA_KW), vec(LANES), vec(LANES),
                  pl.BlockSpec((tm, LANES), lambda b, i: (i, 0)),
                  pl.BlockSpec((tm, LANES), lambda b, i: (i, 0))],
        out_specs=[tspec(s.shape[-1]) for s in out_shapes],
        out_shape=out_shapes,
        compiler_params=_cparams(2),
        name="inproj",
    )(x, sc, sh, w, zb, qnw, knw, cos, sin)


def _nabias_kernel(rpb_ref, o_ref):
    h = pl.program_id(0)
    qc = lax.broadcasted_iota(jnp.int32, (GRID_W, LANES), 0)
    lane = lax.broadcasted_iota(jnp.int32, (GRID_W, LANES), 1)
    kc = lane & (GRID_W - 1)
    hi = lane >= GRID_W
    idx = jnp.clip(kc - qc, -(NA_WIN_COLS - 1), NA_WIN_COLS - 1) + (NA_WIN_COLS - 1)
    start = jnp.clip(qc - NA_WIN_COLS // 2, 0, GRID_W - NA_WIN_COLS)
    col_in = jnp.logical_and(kc >= start, kc < start + NA_WIN_COLS)
    base = h * (NA_DR * NA_DC)

    pair_tables = []
    for dr0 in range(NA_DR - 1):

        def body(j, t, dr0=dr0):
            s0 = rpb_ref[base + dr0 * NA_DC + j]
            s1 = rpb_ref[base + (dr0 + 1) * NA_DC + j]
            return jnp.where(idx == j, jnp.where(hi, s1, s0), t)

        t = lax.fori_loop(0, NA_DC, body, jnp.zeros((GRID_W, LANES), F32))
        pair_tables.append(jnp.where(col_in, t * LOG2_E, NEG_BIG))
    for oi in range(NA_WIN_ROWS):
        for i in range(NA_WIN_ROWS // 2):
            o_ref[0, oi, :, i * LANES:(i + 1) * LANES] = pair_tables[oi + 2 * i]


def _na_bias(rpb_flat):
    return pl.pallas_call(
        _nabias_kernel,
        grid=(NA_HEADS,),
        in_specs=[pl.BlockSpec(memory_space=pltpu.SMEM)],
        out_specs=pl.BlockSpec((1, NA_WIN_ROWS, GRID_W, NA_WIN_ROWS * GRID_W),
                               lambda h: (h, 0, 0, 0)),
        out_shape=jax.ShapeDtypeStruct((NA_HEADS, NA_WIN_ROWS, GRID_W, NA_WIN_ROWS * GRID_W), F32),
        compiler_params=_cparams(1),
        name="na_bias",
    )(rpb_flat)


def _na_kernel(q_ref, k_ref, v_ref, kc_ref, vc_ref, b_ref, o_ref):
    n = q_ref.shape[1]
    rows = n // GRID_W
    win = NA_WIN_ROWS * GRID_W
    lane = lax.broadcasted_iota(jnp.int32, (GRID_W, LANES), 1)
    lo = lane < HEAD_DIM
    kc = kc_ref[0]
    vc = vc_ref[0]
    zero = jnp.zeros((GRID_W, LANES), BF16)

    def one_row(r):
        rs = jnp.clip(r - NA_WIN_ROWS // 2, 0, rows - NA_WIN_ROWS)
        oi = rs - r + (NA_WIN_ROWS - 1)
        q0 = pl.multiple_of(r * GRID_W, GRID_W)
        k0 = pl.multiple_of(rs * GRID_W, GRID_W)
        q = q_ref[0, pl.ds(q0, GRID_W), :]
        kw = k_ref[0, pl.ds(k0, win), :]
        vw = v_ref[0, pl.ds(k0, win), :]
        qs = jnp.concatenate([jnp.where(lo, q, zero), jnp.where(lo, zero, q)], axis=0)
        s_lat = _dot_nt(qs, kw) + jnp.concatenate([b_ref[0, oi], b_ref[1, oi]], axis=0)
        s_ctx = _dot_nt(qs, kc)
        m = jnp.maximum(jnp.max(s_lat, axis=-1, keepdims=True),
                        jnp.max(s_ctx, axis=-1, keepdims=True))
        p_lat = jnp.exp2(s_lat - m)
        p_ctx = jnp.exp2(s_ctx - m)
        den = jnp.sum(p_lat, axis=-1, keepdims=True) + jnp.sum(p_ctx, axis=-1, keepdims=True)
        o = (_dot(p_lat.astype(BF16), vw) + _dot(p_ctx.astype(BF16), vc)) / den
        o_ref[0, pl.ds(q0, GRID_W), :] = jnp.where(lo, o[:GRID_W], o[GRID_W:]).astype(BF16)

    def body(i, carry):
        for j in range(NA_ROWS_PER_STEP):
            one_row(i * NA_ROWS_PER_STEP + j)
        return carry

    lax.fori_loop(0, rows // NA_ROWS_PER_STEP, body, 0)


def _na_attention(q, k, v, kc, vc, bias):
    B, N, _ = q.shape
    L = kc.shape[1]
    lat = pl.BlockSpec((1, N, LANES), lambda b, p: (b, 0, p))
    cx = pl.BlockSpec((1, L, LANES), lambda b, p: (b, 0, p))
    return pl.pallas_call(
        _na_kernel,
        grid=(B, NA_PAIRS),
        in_specs=[lat, lat, lat, cx, cx,
                  pl.BlockSpec((2, NA_WIN_ROWS, GRID_W, NA_WIN_ROWS * GRID_W),
                               lambda b, p: (p, 0, 0, 0))],
        out_specs=lat,
        out_shape=jax.ShapeDtypeStruct((B, N, NA_W), BF16),
        compiler_params=_cparams(2),
        name="na_attn",
    )(q, k, v, kc, vc, bias)


def _pair_attn_kernel(*refs, two_sources):
    if two_sources:
        q_ref, ka_ref, va_ref, kb_ref, vb_ref, o_ref = refs
    else:
        q_ref, ka_ref, va_ref, o_ref = refs
    tq = q_ref.shape[1]
    ts = min(tq, PAIR_ATTN_SUB_ROWS)
    lane = lax.broadcasted_iota(jnp.int32, (ts, LANES), 1)
    lo = lane < HEAD_DIM
    zero = jnp.zeros((ts, LANES), BF16)

    def with_ones(v):
        return jnp.concatenate([v, jnp.ones_like(v)], axis=1)

    ka = ka_ref[0]
    va = with_ones(va_ref[0])
    if two_sources:
        kb = kb_ref[0]
        vb = with_ones(vb_ref[0])
    for sub in range(tq // ts):
        q = q_ref[0, sub * ts:(sub + 1) * ts, :]
        outs = []
        for h in range(2):
            qm = jnp.where(lo, q, zero) if h == 0 else jnp.where(lo, zero, q)
            s_a = _dot_nt(qm, ka)
            m = jnp.max(s_a, axis=-1, keepdims=True)
            if two_sources:
                s_b = _dot_nt(qm, kb)
                m = jnp.maximum(m, jnp.max(s_b, axis=-1, keepdims=True))
            o = _dot(jnp.exp2(s_a - m).astype(BF16), va)
            if two_sources:
                o = o + _dot(jnp.exp2(s_b - m).astype(BF16), vb)
            outs.append(o[:, :LANES] / o[:, LANES:])
        o_ref[0, sub * ts:(sub + 1) * ts, :] = jnp.where(lo, outs[0], outs[1]).astype(BF16)


def _pair_attention(q, ka, va, kb=None, vb=None, *, shared_kv, tq):
    B, Nq, W = q.shape
    pairs = W // LANES
    kv_map = (lambda b, p, i: (b, 0, 0)) if shared_kv else (lambda b, p, i: (b, 0, p))
    qspec = pl.BlockSpec((1, tq, LANES), lambda b, p, i: (b, i, p))
    in_specs = [qspec,
                pl.BlockSpec((1, ka.shape[1], LANES), kv_map),
                pl.BlockSpec((1, ka.shape[1], LANES), kv_map)]
    args = [q, ka, va]
    if kb is not None:
        in_specs += [pl.BlockSpec((1, kb.shape[1], LANES), kv_map),
                     pl.BlockSpec((1, kb.shape[1], LANES), kv_map)]
        args += [kb, vb]
    return pl.pallas_call(
        functools.partial(_pair_attn_kernel, two_sources=kb is not None),
        grid=(B, pairs, Nq // tq),
        in_specs=in_specs,
        out_specs=qspec,
        out_shape=jax.ShapeDtypeStruct((B, Nq, W), BF16),
        compiler_params=_cparams(3),
        name="pair_attn",
    )(*args)


def _gla_constants():
    c = GLA_CHUNK
    t = np.arange(c)[:, None]
    i = np.arange(c)[None, :]
    mcat = np.zeros((2, (GLA_LEVELS + 1) * c, c), np.float32)
    masks = np.zeros((2, GLA_LEVELS + 1, c, c), np.float32)
    for d in range(2):
        for l in range(1, GLA_LEVELS + 1):
            m = 2 ** l
            same = (t // m) == (i // m)
            q_side = same & ((i <= t) if d == 0 else (i >= t))
            k_side = same & ((i > t) if d == 0 else (i < t))
            if l < GLA_LEVELS:
                is_q_row = ((t // m) % 2 == 1) if d == 0 else ((t // m) % 2 == 0)
                mcat[d, (l - 1) * c:l * c] = np.where(is_q_row, q_side, k_side)
            else:
                mcat[d, (l - 1) * c:l * c] = q_side
                mcat[d, l * c:(l + 1) * c] = k_side
        masks[d, 0] = (t == i)
        for l in range(GLA_LEVELS):
            m = 2 ** l
            tb, sb = t // m, i // m
            if d == 0:
                masks[d, l + 1] = (tb % 2 == 1) & (sb == tb - 1)
            else:
                masks[d, l + 1] = (tb % 2 == 0) & (sb == tb + 1)
    masks = np.tile(masks, (1, 1, 1, GLA_HEADS))
    rows = np.arange(GLA_VW)[:, None] // GLA_DV
    cols = np.arange(GLA_KW)[None, :] // GLA_DK
    stmask = (rows == cols).astype(np.float32)
    return jnp.asarray(mcat, BF16), jnp.asarray(masks, F32), jnp.asarray(stmask, F32)


def _gla_chunk(c, d, q_ref, k_ref, v_ref, la_ref, acc_ref, st_ref, mcat_ref, lmask_ref, stmask_ref):
    ck = GLA_CHUNK
    r0 = pl.multiple_of(c * ck, ck)
    q = q_ref[0, pl.ds(r0, ck), :]
    k = k_ref[0, pl.ds(r0, ck), :]
    v = v_ref[0, pl.ds(r0, ck), :]
    g = la_ref[0, pl.ds(r0, ck), d * GLA_KW:(d + 1) * GLA_KW]

    g1 = g.astype(BF16)
    g2 = (g - g1.astype(F32)).astype(BF16)
    mc = mcat_ref[d]
    e = _dot(mc, g1) + _dot(mc, g2)

    row = lax.broadcasted_iota(jnp.int32, (ck, GLA_KW), 0)
    q_row0 = (row & 1) == (1 - d)

    def level_decay(l):
        return jnp.exp(jnp.where(q_row0, g, 0.0) if l == 0 else e[(l - 1) * ck:l * ck])

    lane_k = lax.broadcasted_iota(jnp.int32, (1, GLA_KW), 1) // GLA_DK
    lane_v = lax.broadcasted_iota(jnp.int32, (1, GLA_VW), 1) // GLA_DV

    def block_rows(x, lane_head):
        return jnp.concatenate([jnp.where(lane_head == h, x, 0.0) for h in range(GLA_HEADS)], axis=0)

    def level_scores(qt, kt, idx):
        kb = block_rows(kt, lane_k).astype(BF16)
        return lmask_ref[d, idx] * _dot_nt(qt.astype(BF16), kb)

    a = level_scores(q, k, 0)
    for l in range(GLA_LEVELS):
        dec = level_decay(l)
        a = a + level_scores(q * dec, k * dec, l + 1)

    vbd = block_rows(v.astype(F32), lane_v).astype(BF16)
    o = _dot(a.astype(BF16), vbd)

    st = st_ref[d]
    q_chunk = e[(GLA_LEVELS - 1) * ck:GLA_LEVELS * ck]
    k_chunk = e[GLA_LEVELS * ck:(GLA_LEVELS + 1) * ck]
    o = o + _dot_nt((q * jnp.exp(q_chunk)).astype(BF16), st.astype(BF16))

    ks = (k * jnp.exp(k_chunk)).astype(BF16)
    last = (GLA_LEVELS - 1) * ck + (ck - 1 if d == 0 else 0)
    total = e[last:last + 1]
    st_ref[d] = st * jnp.exp(total) + stmask_ref[...] * _dot_tn(v, ks)
    acc_ref[d, pl.ds(r0, ck), :] = o


def _gla_finish(acc_ref, g_ref, nw_ref, o_ref, tile):
    n = acc_ref.shape[1]
    lane = lax.broadcasted_iota(jnp.int32, (tile, LANES), 1)
    lo = lane < GLA_DV
    nw = nw_ref[...]

    def body(i, carry):
        r0 = pl.multiple_of(i * tile, tile)
        for j in range(GLA_VW // LANES):
            cols = slice(j * LANES, (j + 1) * LANES)
            o = acc_ref[0, pl.ds(r0, tile), cols] + acc_ref[1, pl.ds(r0, tile), cols]
            on = o * lax.rsqrt(_half_mean_square(o, lo) + RMS_EPS) * nw
            gate = _silu(g_ref[0, pl.ds(r0, tile), cols])
            o_ref[0, pl.ds(r0, tile), cols] = (on * gate).astype(BF16)
        return carry

    lax.fori_loop(0, n // tile, body, 0)


def _gla_kernel(q_ref, k_ref, v_ref, la_ref, g_ref, qc_ref, kc_ref, vc_ref, lac_ref, gc_ref,
                nw_ref, mcat_ref, lmask_ref, stmask_ref, o_ref, oc_ref, acc_ref, accc_ref, st_ref):
    n_chunks = q_ref.shape[1] // GLA_CHUNK
    c_chunks = qc_ref.shape[1] // GLA_CHUNK
    consts = (mcat_ref, lmask_ref, stmask_ref)
    st_ref[...] = jnp.zeros_like(st_ref)

    def steps(refs, acc, n):
        def body(i, carry):
            for j in range(GLA_CHUNKS_PER_STEP):
                c = i * GLA_CHUNKS_PER_STEP + j
                _gla_chunk(c, 0, *refs, acc, st_ref, *consts)
                _gla_chunk(n - 1 - c, 1, *refs, acc, st_ref, *consts)
            return carry

        lax.fori_loop(0, n // GLA_CHUNKS_PER_STEP, body, 0)

    steps((qc_ref, kc_ref, vc_ref, lac_ref), accc_ref, c_chunks)
    steps((q_ref, k_ref, v_ref, la_ref), acc_ref, n_chunks)
    _gla_finish(acc_ref, g_ref, nw_ref, o_ref, 256)
    _gla_finish(accc_ref, gc_ref, nw_ref, oc_ref, 256)


def _gla(q, k, v, la, g, qc, kc, vc, lac, gc, nw, consts):
    B, N, _ = q.shape
    L = qc.shape[1]
    mcat, lmask, stmask = consts
    tok = lambda n, w: pl.BlockSpec((1, n, w), lambda b: (b, 0, 0))
    full = lambda a: pl.BlockSpec(a.shape, lambda b: (0,) * a.ndim)
    return pl.pallas_call(
        _gla_kernel,
        grid=(B,),
        in_specs=[tok(N, GLA_KW), tok(N, GLA_KW), tok(N, GLA_VW), tok(N, 2 * GLA_KW), tok(N, GLA_VW),
                  tok(L, GLA_KW), tok(L, GLA_KW), tok(L, GLA_VW), tok(L, 2 * GLA_KW), tok(L, GLA_VW),
                  full(nw), full(mcat), full(lmask), full(stmask)],
        out_specs=[tok(N, GLA_VW), tok(L, GLA_VW)],
        out_shape=[jax.ShapeDtypeStruct((B, N, GLA_VW), BF16), jax.ShapeDtypeStruct((B, L, GLA_VW), BF16)],
        scratch_shapes=[pltpu.VMEM((2, N, GLA_VW), F32), pltpu.VMEM((2, L, GLA_VW), F32),
                        pltpu.VMEM((2, GLA_VW, GLA_KW), F32)],
        compiler_params=_cparams(1),
        name="gla",
    )(q, k, v, la, g, qc, kc, vc, lac, gc, nw, mcat, lmask, stmask)


def _outproj_kernel(x_ref, na_ref, gl_ref, ga_ref, w_ref, g1_ref, lg_ref, lb_ref, o_ref):
    o = jnp.concatenate([na_ref[0], gl_ref[0], ga_ref[0]], axis=-1)
    y = DEEPNORM_ALPHA * x_ref[0] + g1_ref[0] * _dot(o, w_ref[...])
    o_ref[0] = _layer_norm(y, lg_ref[...], lb_ref[...])


def _outproj(x, o_na, o_gl, o_ga, w, g1, lg, lb, tm):
    B, N, D = x.shape
    tspec = lambda width: pl.BlockSpec((1, tm, width), lambda b, i: (b, i, 0))
    vec = pl.BlockSpec((1, D), lambda b, i: (0, 0))
    return pl.pallas_call(
        _outproj_kernel,
        grid=(B, N // tm),
        in_specs=[tspec(D), tspec(NA_W), tspec(GLA_VW), tspec(GQA_QW),
                  pl.BlockSpec((MIX_W, D), lambda b, i: (0, 0)),
                  pl.BlockSpec((1, 1, D), lambda b, i: (b, 0, 0)), vec, vec],
        out_specs=tspec(D),
        out_shape=jax.ShapeDtypeStruct((B, N, D), F32),
        compiler_params=_cparams(2),
        name="outproj",
    )(x, o_na, o_gl, o_ga, w, g1, lg, lb)


_FFN_CHUNKS = ((0, 1024), (1024, 2048), (2048, FFN_HIDDEN))


def _ffn_kernel(x_ref, sc_ref, sh_ref, g2_ref, wi_ref, wo_ref, lg_ref, lb_ref, o_ref):
    x = x_ref[0]
    u = (x * (1.0 + sc_ref[0]) + sh_ref[0]).astype(BF16)
    acc = None
    for a0, a1 in _FFN_CHUNKS:
        ha = _dot(u, wi_ref[:, a0:a1])
        hb = _dot(u, wi_ref[:, FFN_HIDDEN + a0:FFN_HIDDEN + a1])
        part = _dot((_silu(ha) * hb).astype(BF16), wo_ref[a0:a1, :])
        acc = part if acc is None else acc + part
    y = DEEPNORM_ALPHA * x + g2_ref[0] * acc
    o_ref[0] = _layer_norm(y, lg_ref[...], lb_ref[...])


def _ffn(x, sc, sh, g2, wi, wo, lg, lb, tm):
    B, N, D = x.shape
    tspec = pl.BlockSpec((1, tm, D), lambda b, i: (b, i, 0))
    mod = pl.BlockSpec((1, 1, D), lambda b, i: (b, 0, 0))
    vec = pl.BlockSpec((1, D), lambda b, i: (0, 0))
    resident = lambda a: pl.BlockSpec(a.shape, lambda b, i: (0, 0), pipeline_mode=pl.Buffered(1))
    return pl.pallas_call(
        _ffn_kernel,
        grid=(B, N // tm),
        in_specs=[tspec, mod, mod, mod, resident(wi), resident(wo), vec, vec],
        out_specs=tspec,
        out_shape=jax.ShapeDtypeStruct((B, N, D), F32),
        compiler_params=_cparams(2),
        name="ffn",
    )(x, sc, sh, g2, wi, wo, lg, lb)


def _rope_tables(n):
    t = jnp.arange(n)
    row = (t // GRID_W).astype(F32)
    col = (t % GRID_W).astype(F32)
    inv_freq = ROPE_THETA ** (-jnp.arange(ROPE_AXIS_PAIRS, dtype=F32) / ROPE_AXIS_PAIRS)
    ang_r = row[:, None] * inv_freq
    ang_c = col[:, None] * inv_freq
    ang = jnp.concatenate([ang_r, ang_r, ang_c, ang_c], axis=-1)
    sign = jnp.where((jnp.arange(HEAD_DIM) % 32) < 16, -1.0, 1.0).astype(F32)
    cos = jnp.tile(jnp.cos(ang), (1, 2))
    sin = jnp.tile(jnp.sin(ang) * sign, (1, 2))
    return cos, sin


def _pair_major(w, axis):
    shape = w.shape
    lead, tail = shape[:axis], shape[axis + 1:]
    w = w.reshape(lead + (GQA_KV_HEADS, GQA_REP, HEAD_DIM) + tail)
    w = jnp.swapaxes(w, axis, axis + 1)
    return w.reshape(shape)


def kernel(x, c, ctx, c_ctx, w_ada, b_ada, w_in, na_rpb, gla_wa2, gla_ba, gla_norm_w, gqa_qnorm_w,
           gqa_knorm_w, w_out, ln1_g, ln1_b, w_ffn_in, w_ffn_out, ln2_g, ln2_b):
    B, N, D = x.shape
    L = ctx.shape[1]
    depth = w_in.shape[0]
    tm = 512
    tmc = min(L, 512)

    pad = (-(B + 1)) % 8
    cvec = jnp.concatenate([c, c_ctx[None, :], jnp.zeros((pad, D), F32)], axis=0)
    mods = _ada(cvec, w_ada, b_ada)

    w_lr = w_in[:, :, _O_GLLR:_O_GAQ].reshape(depth, D, 2, GLA_RANK).transpose(0, 2, 1, 3)
    w_z = _fold_gate_weights(w_lr, gla_wa2)

    cos, sin = _rope_tables(N)
    cos_c = jnp.ones((L, LANES), F32)
    sin_c = jnp.zeros((L, LANES), F32)
    gla_consts = _gla_constants()

    xc = ctx
    for l in range(depth):
        ctx_out = l < depth - 1
        m_lat = mods[l, :B].reshape(B, 6, 1, D)
        m_ctx = jnp.broadcast_to(mods[l, B].reshape(1, 6, 1, D), (B, 6, 1, D))
        sh1, sc1, g1, sh2, sc2, g2 = (m_lat[:, i] for i in range(6))
        sh1c, sc1c, g1c, sh2c, sc2c, g2c = (m_ctx[:, i] for i in range(6))

        wl = w_in[l]
        w_proj = jnp.concatenate(
            [wl[:, _O_NAQ:_O_GLQ], _pair_major(wl[:, _O_GAQ:_O_GAK], 1), wl[:, _O_GLQ:_O_GLLR],
             w_z[l], wl[:, _O_GAK:_O_END]], axis=1).astype(BF16)
        zb = gla_ba[l].reshape(1, 2 * GLA_KW)
        qnw = jnp.tile(gqa_qnorm_w[l], 2).reshape(1, LANES)
        knw = jnp.tile(gqa_knorm_w[l], 2).reshape(1, LANES)
        glnw = jnp.tile(gla_norm_w[l], 2).reshape(1, LANES)
        wo_l = w_out[l]
        w_o = jnp.concatenate([wo_l[:NA_W + GLA_VW], _pair_major(wo_l[NA_W + GLA_VW:], 0)], axis=0).astype(BF16)
        lg1, lb1 = ln1_g[l].reshape(1, D), ln1_b[l].reshape(1, D)
        lg2, lb2 = ln2_g[l].reshape(1, D), ln2_b[l].reshape(1, D)
        wi = w_ffn_in[l].astype(BF16)
        wo = w_ffn_out[l].astype(BF16)

        (na_q, na_k, na_v, ga_q, gl_q, gl_k, gl_v, gl_g, gl_la, ga_k, ga_v) = _inproj(
            x, sc1, sh1, w_proj, zb, qnw, knw, cos, sin, tm)
        (na_qc, na_kc, na_vc, ga_qc, gl_qc, gl_kc, gl_vc, gl_gc, gl_lac, ga_kc, ga_vc) = _inproj(
            xc, sc1c, sh1c, w_proj, zb, qnw, knw, cos_c, sin_c, tmc)

        bias = _na_bias(na_rpb[l].reshape(-1))
        o_na = _na_attention(na_q, na_k, na_v, na_kc, na_vc, bias)
        o_gl, oc_gl = _gla(gl_q, gl_k, gl_v, gl_la, gl_g, gl_qc, gl_kc, gl_vc, gl_lac, gl_gc, glnw, gla_consts)
        o_ga = _pair_attention(ga_q, ga_k, ga_v, ga_kc, ga_vc, shared_kv=True, tq=256)

        x = _outproj(x, o_na, o_gl, o_ga, w_o, g1, lg1, lb1, tm)
        x = _ffn(x, sc2, sh2, g2, wi, wo, lg2, lb2, tm)

        if ctx_out:
            oc_na = _pair_attention(na_qc, na_kc, na_vc, shared_kv=False, tq=L)
            oc_ga = _pair_attention(ga_qc, ga_kc, ga_vc, shared_kv=True, tq=L)
            xc = _outproj(xc, oc_na, oc_gl, oc_ga, w_o, g1c, lg1, lb1, tmc)
            xc = _ffn(xc, sc2c, sh2c, g2c, wi, wo, lg2, lb2, tmc)
    return x
```

```python
import functools

import numpy as np
import jax
import jax.numpy as jnp
from jax import lax
from jax.experimental import pallas as pl
from jax.experimental.pallas import tpu as pltpu

F32 = jnp.float32
BF16 = jnp.bfloat16

D_MODEL = 1024
DEPTH = 2
GRID_W = 64
HEAD_DIM = 64
LANES = 128

NA_HEADS = 6
NA_WIN_ROWS = 8
NA_WIN_COLS = 16
NA_W = NA_HEADS * HEAD_DIM
NA_PAIRS = NA_W // LANES
NA_DR = 2 * NA_WIN_ROWS - 1
NA_DC = 2 * NA_WIN_COLS - 1
NA_ROWS_PER_STEP = 8

GLA_HEADS = 4
GLA_DK = 32
GLA_DV = 64
GLA_RANK = 16
GLA_GATE_NORM = 16.0
GLA_CHUNK = 64
GLA_KW = GLA_HEADS * GLA_DK
GLA_VW = GLA_HEADS * GLA_DV
GLA_LEVELS = 6
GLA_CHUNKS_PER_STEP = 4

GQA_Q_HEADS = 6
GQA_KV_HEADS = 2
GQA_REP = GQA_Q_HEADS // GQA_KV_HEADS
GQA_QW = GQA_Q_HEADS * HEAD_DIM
GQA_KVW = GQA_KV_HEADS * HEAD_DIM
GQA_PAIRS = GQA_QW // LANES
PAIR_ATTN_SUB_ROWS = 128
ROPE_THETA = 10000.0
ROPE_AXIS_PAIRS = HEAD_DIM // 4

FFN_HIDDEN = 2816
MIX_W = NA_W + GLA_VW + GQA_QW

DEEPNORM_ALPHA = (2.0 * DEPTH) ** 0.25
LN_EPS = 1e-5
RMS_EPS = 1e-6
NEG_BIG = -1e30
LOG2_E = 1.4426950408889634
Q_SCALE = HEAD_DIM ** -0.5 * LOG2_E

_O_NAQ, _O_NAK, _O_NAV = 0, 384, 768
_O_GLQ, _O_GLK, _O_GLV, _O_GLG, _O_GLLR = 1152, 1280, 1408, 1664, 1920
_O_GAQ, _O_GAK, _O_GAV, _O_END = 1952, 2336, 2464, 2592
_C_NAQ, _C_NAK, _C_NAV, _C_GAQ = 0, 384, 768, 1152
_C_GLQ, _C_GLK, _C_GLV, _C_GLG, _C_Z = 1536, 1664, 1792, 2048, 2304
_C_GAK, _C_GAV, _C_END = 2560, 2688, 2816

VMEM_LIMIT = 48 * 1024 * 1024


def _cparams(n_axes):
    return pltpu.CompilerParams(dimension_semantics=("arbitrary",) * n_axes,
                                vmem_limit_bytes=VMEM_LIMIT)


def _dot(a, b):
    return jnp.dot(a, b, preferred_element_type=F32)


def _dot_nt(a, b):
    return lax.dot_general(a, b, (((1,), (1,)), ((), ())), preferred_element_type=F32)


def _dot_tn(a, b):
    return lax.dot_general(a, b, (((0,), (0,)), ((), ())), preferred_element_type=F32)


def _silu(x):
    return x * jax.nn.sigmoid(x)


def _layer_norm(y, g, b):
    mu = jnp.mean(y, axis=-1, keepdims=True)
    d = y - mu
    var = jnp.mean(d * d, axis=-1, keepdims=True)
    return d * lax.rsqrt(var + LN_EPS) * g + b


def _skewed(stages, items):
    n, k = len(items), len(stages)
    live = {}
    for j in range(n + k - 1):
        for s in range(k):
            idx = j - s
            if 0 <= idx < n:
                live[s, idx] = stages[s](items[idx] if s == 0 else live.pop((s - 1, idx)))


def _half_mean_square(y, lo):
    s = y * y
    s_lo = jnp.sum(jnp.where(lo, s, 0.0), axis=-1, keepdims=True)
    s_hi = jnp.sum(jnp.where(lo, 0.0, s), axis=-1, keepdims=True)
    return jnp.where(lo, s_lo, s_hi) * (1.0 / HEAD_DIM)


_ADA_TN = 1024


def _ada_kernel(c_ref, w_ref, b_ref, o_ref):
    s = _silu(c_ref[...])
    o_ref[0] = jnp.dot(s, w_ref[0], precision=lax.Precision.HIGHEST,
                       preferred_element_type=F32) + b_ref[0]


def _ada(cvec, w_ada, b_ada):
    rows = cvec.shape[0]
    depth, d, n6 = w_ada.shape
    return pl.pallas_call(
        _ada_kernel,
        grid=(depth, n6 // _ADA_TN),
        in_specs=[pl.BlockSpec((rows, d), lambda l, j: (0, 0)),
                  pl.BlockSpec((1, d, _ADA_TN), lambda l, j: (l, 0, j)),
                  pl.BlockSpec((1, 1, _ADA_TN), lambda l, j: (l, 0, j))],
        out_specs=pl.BlockSpec((1, rows, _ADA_TN), lambda l, j: (l, 0, j)),
        out_shape=jax.ShapeDtypeStruct((depth, rows, n6), F32),
        compiler_params=_cparams(2),
        name="ada",
    )(cvec, w_ada, b_ada.reshape(depth, 1, n6))


def _fold_kernel(wlr_ref, wa2_ref, o_ref):
    for e in range(2):
        o_ref[0, :, e * GLA_KW:(e + 1) * GLA_KW] = jnp.dot(
            wlr_ref[0, e], wa2_ref[0, e], precision=lax.Precision.HIGHEST,
            preferred_element_type=F32)


def _fold_gate_weights(w_lr, wa2):
    depth = w_lr.shape[0]
    return pl.pallas_call(
        _fold_kernel,
        grid=(depth,),
        in_specs=[pl.BlockSpec((1, 2, D_MODEL, GLA_RANK), lambda l: (l, 0, 0, 0)),
                  pl.BlockSpec((1, 2, GLA_RANK, GLA_KW), lambda l: (l, 0, 0, 0))],
        out_specs=pl.BlockSpec((1, D_MODEL, 2 * GLA_KW), lambda l: (l, 0, 0)),
        out_shape=jax.ShapeDtypeStruct((depth, D_MODEL, 2 * GLA_KW), F32),
        compiler_params=_cparams(1),
        name="fold_gate",
    )(w_lr, wa2)


def _inproj_kernel(x_ref, sc_ref, sh_ref, w_ref, zb_ref, qnw_ref, knw_ref, cos_ref, sin_ref,
                   naq_ref, nak_ref, nav_ref, gaq_ref, glq_ref, glk_ref, glv_ref, glg_ref,
                   gla_ref, gak_ref, gav_ref):
    u = (x_ref[0] * (1.0 + sc_ref[0]) + sh_ref[0]).astype(BF16)
    tm = u.shape[0]
    lane = lax.broadcasted_iota(jnp.int32, (tm, LANES), 1)
    lo = lane < HEAD_DIM
    first16 = (lane & 31) < 16
    cos = cos_ref[...]
    sin = sin_ref[...]

    def seg(a, b):
        return _dot(u, w_ref[:, a:b])

    def norm_rope(y, w):
        yn = y * lax.rsqrt(_half_mean_square(y, lo) + RMS_EPS) * w
        rot = jnp.where(first16, pltpu.roll(yn, LANES - 16, 1), pltpu.roll(yn, 16, 1))
        return yn * cos + rot * sin

    y = seg(_C_NAQ, _C_NAV)
    naq_ref[0] = (y[:, :NA_W] * Q_SCALE).astype(BF16)
    nak_ref[0] = y[:, NA_W:].astype(BF16)

    y = seg(_C_NAV, _C_GLQ)
    nav_ref[0] = y[:, :NA_W].astype(BF16)
    qnw = qnw_ref[...]
    for r in range(GQA_PAIRS):
        t = norm_rope(y[:, NA_W + r * LANES:NA_W + (r + 1) * LANES], qnw)
        gaq_ref[0, :, r * LANES:(r + 1) * LANES] = (t * Q_SCALE).astype(BF16)

    y = seg(_C_GLQ, _C_GLG)
    glq_ref[0] = y[:, :GLA_KW] * GLA_DK ** -0.5
    glk_ref[0] = y[:, GLA_KW:2 * GLA_KW]
    glv_ref[0] = y[:, 2 * GLA_KW:].astype(BF16)

    y = seg(_C_GLG, _C_GAK)
    glg_ref[0] = y[:, :GLA_VW]
    z = y[:, GLA_VW:] + zb_ref[...]
    log_sig = jnp.minimum(z, 0.0) - jnp.log1p(jnp.exp(-jnp.abs(z)))
    gla_ref[0] = log_sig * (1.0 / GLA_GATE_NORM)

    y = seg(_C_GAK, _C_END)
    gak_ref[0] = norm_rope(y[:, :LANES], knw_ref[...]).astype(BF16)
    gav_ref[0] = y[:, LANES:].astype(BF16)


def _inproj(x, sc, sh, w, zb, qnw, knw, cos, sin, tm):
    B, N, D = x.shape
    tok = lambda width, dt: jax.ShapeDtypeStruct((B, N, width), dt)
    tspec = lambda width: pl.BlockSpec((1, tm, width), lambda b, i: (b, i, 0))
    vec = lambda width: pl.BlockSpec((1, width), lambda b, i: (0, 0))
    out_shapes = [tok(NA_W, BF16), tok(NA_W, BF16), tok(NA_W, BF16), tok(GQA_QW, BF16),
                  tok(GLA_KW, F32), tok(GLA_KW, F32), tok(GLA_VW, BF16), tok(GLA_VW, F32),
                  tok(2 * GLA_KW, F32), tok(GQA_KVW, BF16), tok(GQA_KVW, BF16)]
    return pl.pallas_call(
        _inproj_kernel,
        grid=(B, N // tm),
        in_specs=[tspec(D),
                  pl.BlockSpec((1, 1, D), lambda b, i: (b, 0, 0)),
                  pl.BlockSpec((1, 1, D), lambda b, i: (b, 0, 0)),
                  pl.BlockSpec((D, _C_END), lambda b, i: (0, 0)),
                  vec(2 * GLA_KW), vec(LANES), vec(LANES),
                  pl.BlockSpec((tm, LANES), lambda b, i: (i, 0)),
                  pl.BlockSpec((tm, LANES), lambda b, i: (i, 0))],
        out_specs=[tspec(s.shape[-1]) for s in out_shapes],
        out_shape=out_shapes,
        compiler_params=_cparams(2),
        name="inproj",
    )(x, sc, sh, w, zb, qnw, knw, cos, sin)


def _nabias_kernel(rpb_ref, o_ref):
    h = pl.program_id(0)
    qc = lax.broadcasted_iota(jnp.int32, (GRID_W, LANES), 0)
    lane = lax.broadcasted_iota(jnp.int32, (GRID_W, LANES), 1)
    kc = lane & (GRID_W - 1)
    hi = lane >= GRID_W
    idx = jnp.clip(kc - qc, -(NA_WIN_COLS - 1), NA_WIN_COLS - 1) + (NA_WIN_COLS - 1)
    start = jnp.clip(qc - NA_WIN_COLS // 2, 0, GRID_W - NA_WIN_COLS)
    col_in = jnp.logical_and(kc >= start, kc < start + NA_WIN_COLS)
    base = h * (NA_DR * NA_DC)

    pair_tables = []
    for dr0 in range(NA_DR - 1):

        def body(j, t, dr0=dr0):
            s0 = rpb_ref[base + dr0 * NA_DC + j]
            s1 = rpb_ref[base + (dr0 + 1) * NA_DC + j]
            return jnp.where(idx == j, jnp.where(hi, s1, s0), t)

        t = lax.fori_loop(0, NA_DC, body, jnp.zeros((GRID_W, LANES), F32))
        pair_tables.append(jnp.where(col_in, t * LOG2_E, NEG_BIG))
    for oi in range(NA_WIN_ROWS):
        for i in range(NA_WIN_ROWS // 2):
            o_ref[0, oi, :, i * LANES:(i + 1) * LANES] = pair_tables[oi + 2 * i]


def _na_bias(rpb_flat):
    return pl.pallas_call(
        _nabias_kernel,
        grid=(NA_HEADS,),
        in_specs=[pl.BlockSpec(memory_space=pltpu.SMEM)],
        out_specs=pl.BlockSpec((1, NA_WIN_ROWS, GRID_W, NA_WIN_ROWS * GRID_W),
                               lambda h: (h, 0, 0, 0)),
        out_shape=jax.ShapeDtypeStruct((NA_HEADS, NA_WIN_ROWS, GRID_W, NA_WIN_ROWS * GRID_W), F32),
        compiler_params=_cparams(1),
        name="na_bias",
    )(rpb_flat)


def _na_kernel(q_ref, k_ref, v_ref, kc_ref, vc_ref, b_ref, o_ref):
    n = q_ref.shape[1]
    rows = n // GRID_W
    win = NA_WIN_ROWS * GRID_W
    lane = lax.broadcasted_iota(jnp.int32, (GRID_W, LANES), 1)
    lo = lane < HEAD_DIM
    kc = kc_ref[0]
    vc = vc_ref[0]
    zero = jnp.zeros((GRID_W, LANES), BF16)

    def scores(r):
        rs = jnp.clip(r - NA_WIN_ROWS // 2, 0, rows - NA_WIN_ROWS)
        oi = rs - r + (NA_WIN_ROWS - 1)
        q0 = pl.multiple_of(r * GRID_W, GRID_W)
        k0 = pl.multiple_of(rs * GRID_W, GRID_W)
        q = q_ref[0, pl.ds(q0, GRID_W), :]
        kw = k_ref[0, pl.ds(k0, win), :]
        qs = jnp.concatenate([jnp.where(lo, q, zero), jnp.where(lo, zero, q)], axis=0)
        s_lat = _dot_nt(qs, kw) + jnp.concatenate([b_ref[0, oi], b_ref[1, oi]], axis=0)
        s_ctx = _dot_nt(qs, kc)
        return q0, k0, s_lat, s_ctx

    def softmax(st):
        q0, k0, s_lat, s_ctx = st
        m = jnp.maximum(jnp.max(s_lat, axis=-1, keepdims=True),
                        jnp.max(s_ctx, axis=-1, keepdims=True))
        p_lat = jnp.exp2(s_lat - m)
        p_ctx = jnp.exp2(s_ctx - m)
        den = jnp.sum(p_lat, axis=-1, keepdims=True) + jnp.sum(p_ctx, axis=-1, keepdims=True)
        return q0, k0, p_lat.astype(BF16), p_ctx.astype(BF16), den

    def values(st):
        q0, k0, p_lat, p_ctx, den = st
        vw = v_ref[0, pl.ds(k0, win), :]
        o = (_dot(p_lat, vw) + _dot(p_ctx, vc)) / den
        o_ref[0, pl.ds(q0, GRID_W), :] = jnp.where(lo, o[:GRID_W], o[GRID_W:]).astype(BF16)

    def body(i, carry):
        _skewed((scores, softmax, values),
                [i * NA_ROWS_PER_STEP + j for j in range(NA_ROWS_PER_STEP)])
        return carry

    lax.fori_loop(0, rows // NA_ROWS_PER_STEP, body, 0)


def _na_attention(q, k, v, kc, vc, bias):
    B, N, _ = q.shape
    L = kc.shape[1]
    lat = pl.BlockSpec((1, N, LANES), lambda b, p: (b, 0, p))
    cx = pl.BlockSpec((1, L, LANES), lambda b, p: (b, 0, p))
    return pl.pallas_call(
        _na_kernel,
        grid=(B, NA_PAIRS),
        in_specs=[lat, lat, lat, cx, cx,
                  pl.BlockSpec((2, NA_WIN_ROWS, GRID_W, NA_WIN_ROWS * GRID_W),
                               lambda b, p: (p, 0, 0, 0))],
        out_specs=lat,
        out_shape=jax.ShapeDtypeStruct((B, N, NA_W), BF16),
        compiler_params=_cparams(2),
        name="na_attn",
    )(q, k, v, kc, vc, bias)


def _pair_attn_kernel(*refs, two_sources):
    if two_sources:
        q_ref, ka_ref, va_ref, kb_ref, vb_ref, o_ref = refs
    else:
        q_ref, ka_ref, va_ref, o_ref = refs
    tq = q_ref.shape[1]
    ts = min(tq, PAIR_ATTN_SUB_ROWS)
    lane = lax.broadcasted_iota(jnp.int32, (ts, LANES), 1)
    lo = lane < HEAD_DIM
    zero = jnp.zeros((ts, LANES), BF16)

    def with_ones(v):
        return jnp.concatenate([v, jnp.ones_like(v)], axis=1)

    ka = ka_ref[0]
    va = with_ones(va_ref[0])
    if two_sources:
        kb = kb_ref[0]
        vb = with_ones(vb_ref[0])
    def scores(item):
        sub, h = item
        q = q_ref[0, sub * ts:(sub + 1) * ts, :]
        qm = jnp.where(lo, q, zero) if h == 0 else jnp.where(lo, zero, q)
        s_a = _dot_nt(qm, ka)
        s_b = _dot_nt(qm, kb) if two_sources else None
        return sub, h, s_a, s_b

    def softmax(st):
        sub, h, s_a, s_b = st
        m = jnp.max(s_a, axis=-1, keepdims=True)
        if two_sources:
            m = jnp.maximum(m, jnp.max(s_b, axis=-1, keepdims=True))
        p_a = jnp.exp2(s_a - m).astype(BF16)
        p_b = jnp.exp2(s_b - m).astype(BF16) if two_sources else None
        return sub, h, p_a, p_b

    done = {}

    def values(st):
        sub, h, p_a, p_b = st
        o = _dot(p_a, va)
        if two_sources:
            o = o + _dot(p_b, vb)
        done[sub, h] = o[:, :LANES] / o[:, LANES:]
        if h == 1:
            o_ref[0, sub * ts:(sub + 1) * ts, :] = jnp.where(
                lo, done.pop((sub, 0)), done.pop((sub, 1))).astype(BF16)

    _skewed((scores, softmax, values), [(sub, h) for sub in range(tq // ts) for h in range(2)])


def _pair_attention(q, ka, va, kb=None, vb=None, *, shared_kv, tq):
    B, Nq, W = q.shape
    pairs = W // LANES
    kv_map = (lambda b, p, i: (b, 0, 0)) if shared_kv else (lambda b, p, i: (b, 0, p))
    qspec = pl.BlockSpec((1, tq, LANES), lambda b, p, i: (b, i, p))
    in_specs = [qspec,
                pl.BlockSpec((1, ka.shape[1], LANES), kv_map),
                pl.BlockSpec((1, ka.shape[1], LANES), kv_map)]
    args = [q, ka, va]
    if kb is not None:
        in_specs += [pl.BlockSpec((1, kb.shape[1], LANES), kv_map),
                     pl.BlockSpec((1, kb.shape[1], LANES), kv_map)]
        args += [kb, vb]
    return pl.pallas_call(
        functools.partial(_pair_attn_kernel, two_sources=kb is not None),
        grid=(B, pairs, Nq // tq),
        in_specs=in_specs,
        out_specs=qspec,
        out_shape=jax.ShapeDtypeStruct((B, Nq, W), BF16),
        compiler_params=_cparams(3),
        name="pair_attn",
    )(*args)


def _gla_constants():
    c = GLA_CHUNK
    t = np.arange(c)[:, None]
    i = np.arange(c)[None, :]
    mcat = np.zeros((2, (GLA_LEVELS + 1) * c, c), np.float32)
    masks = np.zeros((2, GLA_LEVELS + 1, c, c), np.float32)
    for d in range(2):
        for l in range(1, GLA_LEVELS + 1):
            m = 2 ** l
            same = (t // m) == (i // m)
            q_side = same & ((i <= t) if d == 0 else (i >= t))
            k_side = same & ((i > t) if d == 0 else (i < t))
            if l < GLA_LEVELS:
                is_q_row = ((t // m) % 2 == 1) if d == 0 else ((t // m) % 2 == 0)
                mcat[d, (l - 1) * c:l * c] = np.where(is_q_row, q_side, k_side)
            else:
                mcat[d, (l - 1) * c:l * c] = q_side
                mcat[d, l * c:(l + 1) * c] = k_side
        masks[d, 0] = (t == i)
        for l in range(GLA_LEVELS):
            m = 2 ** l
            tb, sb = t // m, i // m
            if d == 0:
                masks[d, l + 1] = (tb % 2 == 1) & (sb == tb - 1)
            else:
                masks[d, l + 1] = (tb % 2 == 0) & (sb == tb + 1)
    masks = np.tile(masks, (1, 1, 1, GLA_HEADS))
    rows = np.arange(GLA_VW)[:, None] // GLA_DV
    cols = np.arange(GLA_KW)[None, :] // GLA_DK
    stmask = (rows == cols).astype(np.float32)
    return jnp.asarray(mcat, BF16), jnp.asarray(masks, F32), jnp.asarray(stmask, F32)


def _gla_stages(q_ref, k_ref, v_ref, la_ref, acc_ref, st_ref, mcat_ref, lmask_ref, stmask_ref):
    ck = GLA_CHUNK
    lane_k = lax.broadcasted_iota(jnp.int32, (1, GLA_KW), 1) // GLA_DK
    lane_v = lax.broadcasted_iota(jnp.int32, (1, GLA_VW), 1) // GLA_DV
    row = lax.broadcasted_iota(jnp.int32, (ck, GLA_KW), 0)

    def block_rows(x, lane_head):
        return jnp.concatenate([jnp.where(lane_head == h, x, 0.0) for h in range(GLA_HEADS)], axis=0)

    def exponents(item):
        c, d = item
        r0 = pl.multiple_of(c * ck, ck)
        g = la_ref[0, pl.ds(r0, ck), d * GLA_KW:(d + 1) * GLA_KW]
        g1 = g.astype(BF16)
        g2 = (g - g1.astype(F32)).astype(BF16)
        mc = mcat_ref[d]
        e = _dot(mc, g1) + _dot(mc, g2)
        return r0, d, g, e

    def in_chunk(st):
        r0, d, g, e = st
        q = q_ref[0, pl.ds(r0, ck), :]
        k = k_ref[0, pl.ds(r0, ck), :]
        q_row0 = (row & 1) == (1 - d)

        def level_scores(qt, kt, idx):
            kb = block_rows(kt, lane_k).astype(BF16)
            return lmask_ref[d, idx] * _dot_nt(qt.astype(BF16), kb)

        a = level_scores(q, k, 0)
        for l in range(GLA_LEVELS):
            dec = jnp.exp(jnp.where(q_row0, g, 0.0) if l == 0 else e[(l - 1) * ck:l * ck])
            a = a + level_scores(q * dec, k * dec, l + 1)
        q_chunk = e[(GLA_LEVELS - 1) * ck:GLA_LEVELS * ck]
        k_chunk = e[GLA_LEVELS * ck:(GLA_LEVELS + 1) * ck]
        last = ck - 1 if d == 0 else 0
        total = q_chunk[last:last + 1]
        return (r0, d, a.astype(BF16), (q * jnp.exp(q_chunk)).astype(BF16),
                (k * jnp.exp(k_chunk)).astype(BF16), jnp.exp(total))

    def state(st):
        r0, d, a, q_in, k_out, decay = st
        v = v_ref[0, pl.ds(r0, ck), :]
        vbd = block_rows(v.astype(F32), lane_v).astype(BF16)
        s_prev = st_ref[d]
        acc_ref[d, pl.ds(r0, ck), :] = _dot(a, vbd) + _dot_nt(q_in, s_prev.astype(BF16))
        st_ref[d] = s_prev * decay + stmask_ref[...] * _dot_tn(v, k_out)

    return exponents, in_chunk, state


def _gla_finish(acc_ref, g_ref, nw_ref, o_ref, tile):
    n = acc_ref.shape[1]
    lane = lax.broadcasted_iota(jnp.int32, (tile, LANES), 1)
    lo = lane < GLA_DV
    nw = nw_ref[...]

    def body(i, carry):
        r0 = pl.multiple_of(i * tile, tile)
        for j in range(GLA_VW // LANES):
            cols = slice(j * LANES, (j + 1) * LANES)
            o = acc_ref[0, pl.ds(r0, tile), cols] + acc_ref[1, pl.ds(r0, tile), cols]
            on = o * lax.rsqrt(_half_mean_square(o, lo) + RMS_EPS) * nw
            gate = _silu(g_ref[0, pl.ds(r0, tile), cols])
            o_ref[0, pl.ds(r0, tile), cols] = (on * gate).astype(BF16)
        return carry

    lax.fori_loop(0, n // tile, body, 0)


def _gla_kernel(q_ref, k_ref, v_ref, la_ref, g_ref, qc_ref, kc_ref, vc_ref, lac_ref, gc_ref,
                nw_ref, mcat_ref, lmask_ref, stmask_ref, o_ref, oc_ref, acc_ref, accc_ref, st_ref):
    n_chunks = q_ref.shape[1] // GLA_CHUNK
    c_chunks = qc_ref.shape[1] // GLA_CHUNK
    consts = (mcat_ref, lmask_ref, stmask_ref)
    st_ref[...] = jnp.zeros_like(st_ref)

    def steps(refs, acc, n):
        stages = _gla_stages(*refs, acc, st_ref, *consts)

        def body(i, carry):
            items = []
            for j in range(GLA_CHUNKS_PER_STEP):
                c = i * GLA_CHUNKS_PER_STEP + j
                items += [(c, 0), (n - 1 - c, 1)]
            _skewed(stages, items)
            return carry

        lax.fori_loop(0, n // GLA_CHUNKS_PER_STEP, body, 0)

    steps((qc_ref, kc_ref, vc_ref, lac_ref), accc_ref, c_chunks)
    steps((q_ref, k_ref, v_ref, la_ref), acc_ref, n_chunks)
    _gla_finish(acc_ref, g_ref, nw_ref, o_ref, 256)
    _gla_finish(accc_ref, gc_ref, nw_ref, oc_ref, 256)


def _gla(q, k, v, la, g, qc, kc, vc, lac, gc, nw, consts):
    B, N, _ = q.shape
    L = qc.shape[1]
    mcat, lmask, stmask = consts
    tok = lambda n, w: pl.BlockSpec((1, n, w), lambda b: (b, 0, 0))
    full = lambda a: pl.BlockSpec(a.shape, lambda b: (0,) * a.ndim)
    return pl.pallas_call(
        _gla_kernel,
        grid=(B,),
        in_specs=[tok(N, GLA_KW), tok(N, GLA_KW), tok(N, GLA_VW), tok(N, 2 * GLA_KW), tok(N, GLA_VW),
                  tok(L, GLA_KW), tok(L, GLA_KW), tok(L, GLA_VW), tok(L, 2 * GLA_KW), tok(L, GLA_VW),
                  full(nw), full(mcat), full(lmask), full(stmask)],
        out_specs=[tok(N, GLA_VW), tok(L, GLA_VW)],
        out_shape=[jax.ShapeDtypeStruct((B, N, GLA_VW), BF16), jax.ShapeDtypeStruct((B, L, GLA_VW), BF16)],
        scratch_shapes=[pltpu.VMEM((2, N, GLA_VW), F32), pltpu.VMEM((2, L, GLA_VW), F32),
                        pltpu.VMEM((2, GLA_VW, GLA_KW), F32)],
        compiler_params=_cparams(1),
        name="gla",
    )(q, k, v, la, g, qc, kc, vc, lac, gc, nw, mcat, lmask, stmask)


def _outproj_kernel(x_ref, na_ref, gl_ref, ga_ref, w_ref, g1_ref, lg_ref, lb_ref, o_ref):
    o = jnp.concatenate([na_ref[0], gl_ref[0], ga_ref[0]], axis=-1)
    y = DEEPNORM_ALPHA * x_ref[0] + g1_ref[0] * _dot(o, w_ref[...])
    o_ref[0] = _layer_norm(y, lg_ref[...], lb_ref[...])


def _outproj(x, o_na, o_gl, o_ga, w, g1, lg, lb, tm):
    B, N, D = x.shape
    tspec = lambda width: pl.BlockSpec((1, tm, width), lambda b, i: (b, i, 0))
    vec = pl.BlockSpec((1, D), lambda b, i: (0, 0))
    return pl.pallas_call(
        _outproj_kernel,
        grid=(B, N // tm),
        in_specs=[tspec(D), tspec(NA_W), tspec(GLA_VW), tspec(GQA_QW),
                  pl.BlockSpec((MIX_W, D), lambda b, i: (0, 0)),
                  pl.BlockSpec((1, 1, D), lambda b, i: (b, 0, 0)), vec, vec],
        out_specs=tspec(D),
        out_shape=jax.ShapeDtypeStruct((B, N, D), F32),
        compiler_params=_cparams(2),
        name="outproj",
    )(x, o_na, o_gl, o_ga, w, g1, lg, lb)


_FFN_CHUNKS = ((0, 1024), (1024, 2048), (2048, FFN_HIDDEN))


def _ffn_kernel(x_ref, sc_ref, sh_ref, g2_ref, wi_ref, wo_ref, lg_ref, lb_ref, o_ref):
    x = x_ref[0]
    u = (x * (1.0 + sc_ref[0]) + sh_ref[0]).astype(BF16)
    acc = None
    for a0, a1 in _FFN_CHUNKS:
        ha = _dot(u, wi_ref[:, a0:a1])
        hb = _dot(u, wi_ref[:, FFN_HIDDEN + a0:FFN_HIDDEN + a1])
        part = _dot((_silu(ha) * hb).astype(BF16), wo_ref[a0:a1, :])
        acc = part if acc is None else acc + part
    y = DEEPNORM_ALPHA * x + g2_ref[0] * acc
    o_ref[0] = _layer_norm(y, lg_ref[...], lb_ref[...])


def _ffn(x, sc, sh, g2, wi, wo, lg, lb, tm):
    B, N, D = x.shape
    tspec = pl.BlockSpec((1, tm, D), lambda b, i: (b, i, 0))
    mod = pl.BlockSpec((1, 1, D), lambda b, i: (b, 0, 0))
    vec = pl.BlockSpec((1, D), lambda b, i: (0, 0))
    resident = lambda a: pl.BlockSpec(a.shape, lambda b, i: (0, 0), pipeline_mode=pl.Buffered(1))
    return pl.pallas_call(
        _ffn_kernel,
        grid=(B, N // tm),
        in_specs=[tspec, mod, mod, mod, resident(wi), resident(wo), vec, vec],
        out_specs=tspec,
        out_shape=jax.ShapeDtypeStruct((B, N, D), F32),
        compiler_params=_cparams(2),
        name="ffn",
    )(x, sc, sh, g2, wi, wo, lg, lb)


def _rope_tables(n):
    t = jnp.arange(n)
    row = (t // GRID_W).astype(F32)
    col = (t % GRID_W).astype(F32)
    inv_freq = ROPE_THETA ** (-jnp.arange(ROPE_AXIS_PAIRS, dtype=F32) / ROPE_AXIS_PAIRS)
    ang_r = row[:, None] * inv_freq
    ang_c = col[:, None] * inv_freq
    ang = jnp.concatenate([ang_r, ang_r, ang_c, ang_c], axis=-1)
    sign = jnp.where((jnp.arange(HEAD_DIM) % 32) < 16, -1.0, 1.0).astype(F32)
    cos = jnp.tile(jnp.cos(ang), (1, 2))
    sin = jnp.tile(jnp.sin(ang) * sign, (1, 2))
    return cos, sin


def _pair_major(w, axis):
    shape = w.shape
    lead, tail = shape[:axis], shape[axis + 1:]
    w = w.reshape(lead + (GQA_KV_HEADS, GQA_REP, HEAD_DIM) + tail)
    w = jnp.swapaxes(w, axis, axis + 1)
    return w.reshape(shape)


def kernel(x, c, ctx, c_ctx, w_ada, b_ada, w_in, na_rpb, gla_wa2, gla_ba, gla_norm_w, gqa_qnorm_w,
           gqa_knorm_w, w_out, ln1_g, ln1_b, w_ffn_in, w_ffn_out, ln2_g, ln2_b):
    B, N, D = x.shape
    L = ctx.shape[1]
    depth = w_in.shape[0]
    tm = 512
    tmc = min(L, 512)

    pad = (-(B + 1)) % 8
    cvec = jnp.concatenate([c, c_ctx[None, :], jnp.zeros((pad, D), F32)], axis=0)
    mods = _ada(cvec, w_ada, b_ada)

    w_lr = w_in[:, :, _O_GLLR:_O_GAQ].reshape(depth, D, 2, GLA_RANK).transpose(0, 2, 1, 3)
    w_z = _fold_gate_weights(w_lr, gla_wa2)

    cos, sin = _rope_tables(N)
    cos_c = jnp.ones((L, LANES), F32)
    sin_c = jnp.zeros((L, LANES), F32)
    gla_consts = _gla_constants()

    xc = ctx
    for l in range(depth):
        ctx_out = l < depth - 1
        m_lat = mods[l, :B].reshape(B, 6, 1, D)
        m_ctx = jnp.broadcast_to(mods[l, B].reshape(1, 6, 1, D), (B, 6, 1, D))
        sh1, sc1, g1, sh2, sc2, g2 = (m_lat[:, i] for i in range(6))
        sh1c, sc1c, g1c, sh2c, sc2c, g2c = (m_ctx[:, i] for i in range(6))

        wl = w_in[l]
        w_proj = jnp.concatenate(
            [wl[:, _O_NAQ:_O_GLQ], _pair_major(wl[:, _O_GAQ:_O_GAK], 1), wl[:, _O_GLQ:_O_GLLR],
             w_z[l], wl[:, _O_GAK:_O_END]], axis=1).astype(BF16)
        zb = gla_ba[l].reshape(1, 2 * GLA_KW)
        qnw = jnp.tile(gqa_qnorm_w[l], 2).reshape(1, LANES)
        knw = jnp.tile(gqa_knorm_w[l], 2).reshape(1, LANES)
        glnw = jnp.tile(gla_norm_w[l], 2).reshape(1, LANES)
        wo_l = w_out[l]
        w_o = jnp.concatenate([wo_l[:NA_W + GLA_VW], _pair_major(wo_l[NA_W + GLA_VW:], 0)], axis=0).astype(BF16)
        lg1, lb1 = ln1_g[l].reshape(1, D), ln1_b[l].reshape(1, D)
        lg2, lb2 = ln2_g[l].reshape(1, D), ln2_b[l].reshape(1, D)
        wi = w_ffn_in[l].astype(BF16)
        wo = w_ffn_out[l].astype(BF16)

        (na_q, na_k, na_v, ga_q, gl_q, gl_k, gl_v, gl_g, gl_la, ga_k, ga_v) = _inproj(
            x, sc1, sh1, w_proj, zb, qnw, knw, cos, sin, tm)
        (na_qc, na_kc, na_vc, ga_qc, gl_qc, gl_kc, gl_vc, gl_gc, gl_lac, ga_kc, ga_vc) = _inproj(
            xc, sc1c, sh1c, w_proj, zb, qnw, knw, cos_c, sin_c, tmc)

        bias = _na_bias(na_rpb[l].reshape(-1))
        o_na = _na_attention(na_q, na_k, na_v, na_kc, na_vc, bias)
        o_gl, oc_gl = _gla(gl_q, gl_k, gl_v, gl_la, gl_g, gl_qc, gl_kc, gl_vc, gl_lac, gl_gc, glnw, gla_consts)
        o_ga = _pair_attention(ga_q, ga_k, ga_v, ga_kc, ga_vc, shared_kv=True, tq=512)

        x = _outproj(x, o_na, o_gl, o_ga, w_o, g1, lg1, lb1, tm)
        x = _ffn(x, sc2, sh2, g2, wi, wo, lg2, lb2, tm)

        if ctx_out:
            oc_na = _pair_attention(na_qc, na_kc, na_vc, shared_kv=False, tq=L)
            oc_ga = _pair_attention(ga_qc, ga_kc, ga_vc, shared_kv=True, tq=L)
            xc = _outproj(xc, oc_na, oc_gl, oc_ga, w_o, g1c, lg1, lb1, tmc)
            xc = _ffn(xc, sc2c, sh2c, g2c, wi, wo, lg2, lb2, tmc)
    return x
```

```python
import functools

import numpy as np
import jax
import jax.numpy as jnp
from jax import lax
from jax.experimental import pallas as pl
from jax.experimental.pallas import tpu as pltpu

F32 = jnp.float32
BF16 = jnp.bfloat16

D_MODEL = 1024
DEPTH = 2
GRID_W = 64
HEAD_DIM = 64
LANES = 128

NA_HEADS = 6
NA_WIN_ROWS = 8
NA_WIN_COLS = 16
NA_W = NA_HEADS * HEAD_DIM
NA_PAIRS = NA_W // LANES
NA_DR = 2 * NA_WIN_ROWS - 1
NA_DC = 2 * NA_WIN_COLS - 1
NA_ROWS_PER_STEP = 8

GLA_HEADS = 4
GLA_DK = 32
GLA_DV = 64
GLA_RANK = 16
GLA_GATE_NORM = 16.0
GLA_CHUNK = 64
GLA_KW = GLA_HEADS * GLA_DK
GLA_VW = GLA_HEADS * GLA_DV
GLA_LEVELS = 6
GLA_CHUNKS_PER_STEP = 4

GQA_Q_HEADS = 6
GQA_KV_HEADS = 2
GQA_REP = GQA_Q_HEADS // GQA_KV_HEADS
GQA_QW = GQA_Q_HEADS * HEAD_DIM
GQA_KVW = GQA_KV_HEADS * HEAD_DIM
GQA_PAIRS = GQA_QW // LANES
ROW_SUBTILES = 2
PAIR_ATTN_SUB_ROWS = 128
ROPE_THETA = 10000.0
ROPE_AXIS_PAIRS = HEAD_DIM // 4

FFN_HIDDEN = 2816
MIX_W = NA_W + GLA_VW + GQA_QW

DEEPNORM_ALPHA = (2.0 * DEPTH) ** 0.25
LN_EPS = 1e-5
RMS_EPS = 1e-6
NEG_BIG = -1e30
LOG2_E = 1.4426950408889634
Q_SCALE = HEAD_DIM ** -0.5 * LOG2_E

_O_NAQ, _O_NAK, _O_NAV = 0, 384, 768
_O_GLQ, _O_GLK, _O_GLV, _O_GLG, _O_GLLR = 1152, 1280, 1408, 1664, 1920
_O_GAQ, _O_GAK, _O_GAV, _O_END = 1952, 2336, 2464, 2592
_C_NAQ, _C_NAK, _C_NAV, _C_GAQ = 0, 384, 768, 1152
_C_GLQ, _C_GLK, _C_GLV, _C_GLG, _C_Z = 1536, 1664, 1792, 2048, 2304
_C_GAK, _C_GAV, _C_END = 2560, 2688, 2816

VMEM_LIMIT = 48 * 1024 * 1024


def _cparams(n_axes):
    return pltpu.CompilerParams(dimension_semantics=("arbitrary",) * n_axes,
                                vmem_limit_bytes=VMEM_LIMIT)


def _dot(a, b):
    return jnp.dot(a, b, preferred_element_type=F32)


def _dot_nt(a, b):
    return lax.dot_general(a, b, (((1,), (1,)), ((), ())), preferred_element_type=F32)


def _dot_tn(a, b):
    return lax.dot_general(a, b, (((0,), (0,)), ((), ())), preferred_element_type=F32)


def _silu(x):
    return x * jax.nn.sigmoid(x)


def _layer_norm(y, g, b):
    mu = jnp.mean(y, axis=-1, keepdims=True)
    d = y - mu
    var = jnp.mean(d * d, axis=-1, keepdims=True)
    return d * lax.rsqrt(var + LN_EPS) * g + b


def _skewed(stages, items):
    n, k = len(items), len(stages)
    live = {}
    for j in range(n + k - 1):
        for s in range(k):
            idx = j - s
            if 0 <= idx < n:
                live[s, idx] = stages[s](items[idx] if s == 0 else live.pop((s - 1, idx)))


def _half_mean_square(y, lo):
    s = y * y
    s_lo = jnp.sum(jnp.where(lo, s, 0.0), axis=-1, keepdims=True)
    s_hi = jnp.sum(jnp.where(lo, 0.0, s), axis=-1, keepdims=True)
    return jnp.where(lo, s_lo, s_hi) * (1.0 / HEAD_DIM)


_ADA_TN = 1024


def _ada_kernel(c_ref, w_ref, b_ref, o_ref):
    s = _silu(c_ref[...])
    o_ref[0] = jnp.dot(s, w_ref[0], precision=lax.Precision.HIGHEST,
                       preferred_element_type=F32) + b_ref[0]


def _ada(cvec, w_ada, b_ada):
    rows = cvec.shape[0]
    depth, d, n6 = w_ada.shape
    return pl.pallas_call(
        _ada_kernel,
        grid=(depth, n6 // _ADA_TN),
        in_specs=[pl.BlockSpec((rows, d), lambda l, j: (0, 0)),
                  pl.BlockSpec((1, d, _ADA_TN), lambda l, j: (l, 0, j)),
                  pl.BlockSpec((1, 1, _ADA_TN), lambda l, j: (l, 0, j))],
        out_specs=pl.BlockSpec((1, rows, _ADA_TN), lambda l, j: (l, 0, j)),
        out_shape=jax.ShapeDtypeStruct((depth, rows, n6), F32),
        compiler_params=_cparams(2),
        name="ada",
    )(cvec, w_ada, b_ada.reshape(depth, 1, n6))


def _fold_kernel(wlr_ref, wa2_ref, o_ref):
    for e in range(2):
        o_ref[0, :, e * GLA_KW:(e + 1) * GLA_KW] = jnp.dot(
            wlr_ref[0, e], wa2_ref[0, e], precision=lax.Precision.HIGHEST,
            preferred_element_type=F32)


def _fold_gate_weights(w_lr, wa2):
    depth = w_lr.shape[0]
    return pl.pallas_call(
        _fold_kernel,
        grid=(depth,),
        in_specs=[pl.BlockSpec((1, 2, D_MODEL, GLA_RANK), lambda l: (l, 0, 0, 0)),
                  pl.BlockSpec((1, 2, GLA_RANK, GLA_KW), lambda l: (l, 0, 0, 0))],
        out_specs=pl.BlockSpec((1, D_MODEL, 2 * GLA_KW), lambda l: (l, 0, 0)),
        out_shape=jax.ShapeDtypeStruct((depth, D_MODEL, 2 * GLA_KW), F32),
        compiler_params=_cparams(1),
        name="fold_gate",
    )(w_lr, wa2)


def _inproj_kernel(x_ref, sc_ref, sh_ref, w_ref, zb_ref, qnw_ref, knw_ref, cos_ref, sin_ref,
                   naq_ref, nak_ref, nav_ref, gaq_ref, glq_ref, glk_ref, glv_ref, glg_ref,
                   gla_ref, gak_ref, gav_ref):
    ts = x_ref.shape[1] // ROW_SUBTILES
    lane = lax.broadcasted_iota(jnp.int32, (ts, LANES), 1)
    lo = lane < HEAD_DIM
    first16 = (lane & 31) < 16
    u_of = {}

    def rows(sub):
        return slice(sub * ts, (sub + 1) * ts)

    def matmul(item):
        sub, (a, b, epilogue) = item
        if sub not in u_of:
            u_of[sub] = (x_ref[0, rows(sub), :] * (1.0 + sc_ref[0]) + sh_ref[0]).astype(BF16)
        return sub, epilogue, _dot(u_of[sub], w_ref[:, a:b])

    def norm_rope(y, w, r):
        yn = y * lax.rsqrt(_half_mean_square(y, lo) + RMS_EPS) * w
        rot = jnp.where(first16, pltpu.roll(yn, LANES - 16, 1), pltpu.roll(yn, 16, 1))
        return yn * cos_ref[r, :] + rot * sin_ref[r, :]

    def na_qk(r, y):
        naq_ref[0, r, :] = (y[:, :NA_W] * Q_SCALE).astype(BF16)
        nak_ref[0, r, :] = y[:, NA_W:].astype(BF16)

    def na_v_gqa_q(r, y):
        nav_ref[0, r, :] = y[:, :NA_W].astype(BF16)
        qnw = qnw_ref[...]
        for p in range(GQA_PAIRS):
            t = norm_rope(y[:, NA_W + p * LANES:NA_W + (p + 1) * LANES], qnw, r)
            gaq_ref[0, r, p * LANES:(p + 1) * LANES] = (t * Q_SCALE).astype(BF16)

    def gla_qkv(r, y):
        glq_ref[0, r, :] = y[:, :GLA_KW] * GLA_DK ** -0.5
        glk_ref[0, r, :] = y[:, GLA_KW:2 * GLA_KW]
        glv_ref[0, r, :] = y[:, 2 * GLA_KW:].astype(BF16)

    def gla_gates(r, y):
        glg_ref[0, r, :] = y[:, :GLA_VW]
        z = y[:, GLA_VW:] + zb_ref[...]
        log_sig = jnp.minimum(z, 0.0) - jnp.log1p(jnp.exp(-jnp.abs(z)))
        gla_ref[0, r, :] = log_sig * (1.0 / GLA_GATE_NORM)

    def gqa_kv(r, y):
        gak_ref[0, r, :] = norm_rope(y[:, :LANES], knw_ref[...], r).astype(BF16)
        gav_ref[0, r, :] = y[:, LANES:].astype(BF16)

    segments = [(_C_NAQ, _C_NAV, na_qk), (_C_NAV, _C_GLQ, na_v_gqa_q), (_C_GLQ, _C_GLG, gla_qkv),
                (_C_GLG, _C_GAK, gla_gates), (_C_GAK, _C_END, gqa_kv)]
    _skewed((matmul, lambda st: st[1](rows(st[0]), st[2])),
            [(sub, sg) for sub in range(ROW_SUBTILES) for sg in segments])


def _inproj(x, sc, sh, w, zb, qnw, knw, cos, sin, tm):
    B, N, D = x.shape
    tok = lambda width, dt: jax.ShapeDtypeStruct((B, N, width), dt)
    tspec = lambda width: pl.BlockSpec((1, tm, width), lambda b, i: (b, i, 0))
    vec = lambda width: pl.BlockSpec((1, width), lambda b, i: (0, 0))
    out_shapes = [tok(NA_W, BF16), tok(NA_W, BF16), tok(NA_W, BF16), tok(GQA_QW, BF16),
                  tok(GLA_KW, F32), tok(GLA_KW, F32), tok(GLA_VW, BF16), tok(GLA_VW, F32),
                  tok(2 * GLA_KW, F32), tok(GQA_KVW, BF16), tok(GQA_KVW, BF16)]
    return pl.pallas_call(
        _inproj_kernel,
        grid=(B, N // tm),
        in_specs=[tspec(D),
                  pl.BlockSpec((1, 1, D), lambda b, i: (b, 0, 0)),
                  pl.BlockSpec((1, 1, D), lambda b, i: (b, 0, 0)),
                  pl.BlockSpec((D, _C_END), lambda b, i: (0, 0)),
                  vec(2 * GLA_KW), vec(LANES), vec(LANES),
                  pl.BlockSpec((tm, LANES), lambda b, i: (i, 0)),
                  pl.BlockSpec((tm, LANES), lambda b, i: (i, 0))],
        out_specs=[tspec(s.shape[-1]) for s in out_shapes],
        out_shape=out_shapes,
        compiler_params=_cparams(2),
        name="inproj",
    )(x, sc, sh, w, zb, qnw, knw, cos, sin)


def _nabias_kernel(rpb_ref, o_ref):
    h = pl.program_id(0)
    qc = lax.broadcasted_iota(jnp.int32, (GRID_W, LANES), 0)
    lane = lax.broadcasted_iota(jnp.int32, (GRID_W, LANES), 1)
    kc = lane & (GRID_W - 1)
    hi = lane >= GRID_W
    idx = jnp.clip(kc - qc, -(NA_WIN_COLS - 1), NA_WIN_COLS - 1) + (NA_WIN_COLS - 1)
    start = jnp.clip(qc - NA_WIN_COLS // 2, 0, GRID_W - NA_WIN_COLS)
    col_in = jnp.logical_and(kc >= start, kc < start + NA_WIN_COLS)
    base = h * (NA_DR * NA_DC)

    pair_tables = []
    for dr0 in range(NA_DR - 1):

        def body(j, t, dr0=dr0):
            s0 = rpb_ref[base + dr0 * NA_DC + j]
            s1 = rpb_ref[base + (dr0 + 1) * NA_DC + j]
            return jnp.where(idx == j, jnp.where(hi, s1, s0), t)

        t = lax.fori_loop(0, NA_DC, body, jnp.zeros((GRID_W, LANES), F32))
        pair_tables.append(jnp.where(col_in, t * LOG2_E, NEG_BIG))
    for oi in range(NA_WIN_ROWS):
        for i in range(NA_WIN_ROWS // 2):
            o_ref[0, oi, :, i * LANES:(i + 1) * LANES] = pair_tables[oi + 2 * i]


def _na_bias(rpb_flat):
    return pl.pallas_call(
        _nabias_kernel,
        grid=(NA_HEADS,),
        in_specs=[pl.BlockSpec(memory_space=pltpu.SMEM)],
        out_specs=pl.BlockSpec((1, NA_WIN_ROWS, GRID_W, NA_WIN_ROWS * GRID_W),
                               lambda h: (h, 0, 0, 0)),
        out_shape=jax.ShapeDtypeStruct((NA_HEADS, NA_WIN_ROWS, GRID_W, NA_WIN_ROWS * GRID_W), F32),
        compiler_params=_cparams(1),
        name="na_bias",
    )(rpb_flat)


def _na_kernel(q_ref, k_ref, v_ref, kc_ref, vc_ref, b_ref, o_ref):
    n = q_ref.shape[1]
    rows = n // GRID_W
    win = NA_WIN_ROWS * GRID_W
    lane = lax.broadcasted_iota(jnp.int32, (GRID_W, LANES), 1)
    lo = lane < HEAD_DIM
    kc = kc_ref[0]
    vc = vc_ref[0]
    zero = jnp.zeros((GRID_W, LANES), BF16)

    def scores(r):
        rs = jnp.clip(r - NA_WIN_ROWS // 2, 0, rows - NA_WIN_ROWS)
        oi = rs - r + (NA_WIN_ROWS - 1)
        q0 = pl.multiple_of(r * GRID_W, GRID_W)
        k0 = pl.multiple_of(rs * GRID_W, GRID_W)
        q = q_ref[0, pl.ds(q0, GRID_W), :]
        kw = k_ref[0, pl.ds(k0, win), :]
        qs = jnp.concatenate([jnp.where(lo, q, zero), jnp.where(lo, zero, q)], axis=0)
        s_lat = _dot_nt(qs, kw) + jnp.concatenate([b_ref[0, oi], b_ref[1, oi]], axis=0)
        s_ctx = _dot_nt(qs, kc)
        return q0, k0, s_lat, s_ctx

    def softmax(st):
        q0, k0, s_lat, s_ctx = st
        m = jnp.maximum(jnp.max(s_lat, axis=-1, keepdims=True),
                        jnp.max(s_ctx, axis=-1, keepdims=True))
        p_lat = jnp.exp2(s_lat - m)
        p_ctx = jnp.exp2(s_ctx - m)
        den = jnp.sum(p_lat, axis=-1, keepdims=True) + jnp.sum(p_ctx, axis=-1, keepdims=True)
        return q0, k0, p_lat.astype(BF16), p_ctx.astype(BF16), den

    def values(st):
        q0, k0, p_lat, p_ctx, den = st
        vw = v_ref[0, pl.ds(k0, win), :]
        o = (_dot(p_lat, vw) + _dot(p_ctx, vc)) / den
        o_ref[0, pl.ds(q0, GRID_W), :] = jnp.where(lo, o[:GRID_W], o[GRID_W:]).astype(BF16)

    def body(i, carry):
        _skewed((scores, softmax, values),
                [i * NA_ROWS_PER_STEP + j for j in range(NA_ROWS_PER_STEP)])
        return carry

    lax.fori_loop(0, rows // NA_ROWS_PER_STEP, body, 0)


def _na_attention(q, k, v, kc, vc, bias):
    B, N, _ = q.shape
    L = kc.shape[1]
    lat = pl.BlockSpec((1, N, LANES), lambda b, p: (b, 0, p))
    cx = pl.BlockSpec((1, L, LANES), lambda b, p: (b, 0, p))
    return pl.pallas_call(
        _na_kernel,
        grid=(B, NA_PAIRS),
        in_specs=[lat, lat, lat, cx, cx,
                  pl.BlockSpec((2, NA_WIN_ROWS, GRID_W, NA_WIN_ROWS * GRID_W),
                               lambda b, p: (p, 0, 0, 0))],
        out_specs=lat,
        out_shape=jax.ShapeDtypeStruct((B, N, NA_W), BF16),
        compiler_params=_cparams(2),
        name="na_attn",
    )(q, k, v, kc, vc, bias)


def _pair_attn_kernel(*refs, two_sources):
    if two_sources:
        q_ref, ka_ref, va_ref, kb_ref, vb_ref, o_ref = refs
    else:
        q_ref, ka_ref, va_ref, o_ref = refs
    tq = q_ref.shape[1]
    ts = min(tq, PAIR_ATTN_SUB_ROWS)
    lane = lax.broadcasted_iota(jnp.int32, (ts, LANES), 1)
    lo = lane < HEAD_DIM
    zero = jnp.zeros((ts, LANES), BF16)

    def with_ones(v):
        return jnp.concatenate([v, jnp.ones_like(v)], axis=1)

    ka = ka_ref[0]
    va = with_ones(va_ref[0])
    if two_sources:
        kb = kb_ref[0]
        vb = with_ones(vb_ref[0])
    def scores(item):
        sub, h = item
        q = q_ref[0, sub * ts:(sub + 1) * ts, :]
        qm = jnp.where(lo, q, zero) if h == 0 else jnp.where(lo, zero, q)
        s_a = _dot_nt(qm, ka)
        s_b = _dot_nt(qm, kb) if two_sources else None
        return sub, h, s_a, s_b

    def softmax(st):
        sub, h, s_a, s_b = st
        m = jnp.max(s_a, axis=-1, keepdims=True)
        if two_sources:
            m = jnp.maximum(m, jnp.max(s_b, axis=-1, keepdims=True))
        p_a = jnp.exp2(s_a - m).astype(BF16)
        p_b = jnp.exp2(s_b - m).astype(BF16) if two_sources else None
        return sub, h, p_a, p_b

    done = {}

    def values(st):
        sub, h, p_a, p_b = st
        o = _dot(p_a, va)
        if two_sources:
            o = o + _dot(p_b, vb)
        done[sub, h] = o[:, :LANES] / o[:, LANES:]
        if h == 1:
            o_ref[0, sub * ts:(sub + 1) * ts, :] = jnp.where(
                lo, done.pop((sub, 0)), done.pop((sub, 1))).astype(BF16)

    _skewed((scores, softmax, values), [(sub, h) for sub in range(tq // ts) for h in range(2)])


def _pair_attention(q, ka, va, kb=None, vb=None, *, shared_kv, tq):
    B, Nq, W = q.shape
    pairs = W // LANES
    kv_map = (lambda b, p, i: (b, 0, 0)) if shared_kv else (lambda b, p, i: (b, 0, p))
    qspec = pl.BlockSpec((1, tq, LANES), lambda b, p, i: (b, i, p))
    in_specs = [qspec,
                pl.BlockSpec((1, ka.shape[1], LANES), kv_map),
                pl.BlockSpec((1, ka.shape[1], LANES), kv_map)]
    args = [q, ka, va]
    if kb is not None:
        in_specs += [pl.BlockSpec((1, kb.shape[1], LANES), kv_map),
                     pl.BlockSpec((1, kb.shape[1], LANES), kv_map)]
        args += [kb, vb]
    return pl.pallas_call(
        functools.partial(_pair_attn_kernel, two_sources=kb is not None),
        grid=(B, pairs, Nq // tq),
        in_specs=in_specs,
        out_specs=qspec,
        out_shape=jax.ShapeDtypeStruct((B, Nq, W), BF16),
        compiler_params=_cparams(3),
        name="pair_attn",
    )(*args)


def _gla_constants():
    c = GLA_CHUNK
    t = np.arange(c)[:, None]
    i = np.arange(c)[None, :]
    mcat = np.zeros((2, (GLA_LEVELS + 1) * c, c), np.float32)
    masks = np.zeros((2, GLA_LEVELS + 1, c, c), np.float32)
    for d in range(2):
        for l in range(1, GLA_LEVELS + 1):
            m = 2 ** l
            same = (t // m) == (i // m)
            q_side = same & ((i <= t) if d == 0 else (i >= t))
            k_side = same & ((i > t) if d == 0 else (i < t))
            if l < GLA_LEVELS:
                is_q_row = ((t // m) % 2 == 1) if d == 0 else ((t // m) % 2 == 0)
                mcat[d, (l - 1) * c:l * c] = np.where(is_q_row, q_side, k_side)
            else:
                mcat[d, (l - 1) * c:l * c] = q_side
                mcat[d, l * c:(l + 1) * c] = k_side
        masks[d, 0] = (t == i)
        for l in range(GLA_LEVELS):
            m = 2 ** l
            tb, sb = t // m, i // m
            if d == 0:
                masks[d, l + 1] = (tb % 2 == 1) & (sb == tb - 1)
            else:
                masks[d, l + 1] = (tb % 2 == 0) & (sb == tb + 1)
    masks = np.tile(masks, (1, 1, 1, GLA_HEADS))
    rows = np.arange(GLA_VW)[:, None] // GLA_DV
    cols = np.arange(GLA_KW)[None, :] // GLA_DK
    stmask = (rows == cols).astype(np.float32)
    return jnp.asarray(mcat, BF16), jnp.asarray(masks, F32), jnp.asarray(stmask, F32)


def _gla_stages(q_ref, k_ref, v_ref, la_ref, acc_ref, st_ref, mcat_ref, lmask_ref, stmask_ref):
    ck = GLA_CHUNK
    lane_k = lax.broadcasted_iota(jnp.int32, (1, GLA_KW), 1) // GLA_DK
    lane_v = lax.broadcasted_iota(jnp.int32, (1, GLA_VW), 1) // GLA_DV
    row = lax.broadcasted_iota(jnp.int32, (ck, GLA_KW), 0)

    def block_rows(x, lane_head):
        return jnp.concatenate([jnp.where(lane_head == h, x, 0.0) for h in range(GLA_HEADS)], axis=0)

    def exponents(item):
        c, d = item
        r0 = pl.multiple_of(c * ck, ck)
        g = la_ref[0, pl.ds(r0, ck), d * GLA_KW:(d + 1) * GLA_KW]
        g1 = g.astype(BF16)
        g2 = (g - g1.astype(F32)).astype(BF16)
        mc = mcat_ref[d]
        e = _dot(mc, g1) + _dot(mc, g2)
        return r0, d, g, e

    def in_chunk(st):
        r0, d, g, e = st
        q = q_ref[0, pl.ds(r0, ck), :]
        k = k_ref[0, pl.ds(r0, ck), :]
        q_row0 = (row & 1) == (1 - d)

        def level_scores(qt, kt, idx):
            kb = block_rows(kt, lane_k).astype(BF16)
            return lmask_ref[d, idx] * _dot_nt(qt.astype(BF16), kb)

        a = level_scores(q, k, 0)
        for l in range(GLA_LEVELS):
            dec = jnp.exp(jnp.where(q_row0, g, 0.0) if l == 0 else e[(l - 1) * ck:l * ck])
            a = a + level_scores(q * dec, k * dec, l + 1)
        q_chunk = e[(GLA_LEVELS - 1) * ck:GLA_LEVELS * ck]
        k_chunk = e[GLA_LEVELS * ck:(GLA_LEVELS + 1) * ck]
        last = ck - 1 if d == 0 else 0
        total = q_chunk[last:last + 1]
        return (r0, d, a.astype(BF16), (q * jnp.exp(q_chunk)).astype(BF16),
                (k * jnp.exp(k_chunk)).astype(BF16), jnp.exp(total))

    def state(st):
        r0, d, a, q_in, k_out, decay = st
        v = v_ref[0, pl.ds(r0, ck), :]
        vbd = block_rows(v.astype(F32), lane_v).astype(BF16)
        s_prev = st_ref[d]
        acc_ref[d, pl.ds(r0, ck), :] = _dot(a, vbd) + _dot_nt(q_in, s_prev.astype(BF16))
        st_ref[d] = s_prev * decay + stmask_ref[...] * _dot_tn(v, k_out)

    return exponents, in_chunk, state


def _gla_finish(acc_ref, g_ref, nw_ref, o_ref, tile):
    n = acc_ref.shape[1]
    lane = lax.broadcasted_iota(jnp.int32, (tile, LANES), 1)
    lo = lane < GLA_DV
    nw = nw_ref[...]

    def body(i, carry):
        r0 = pl.multiple_of(i * tile, tile)
        for j in range(GLA_VW // LANES):
            cols = slice(j * LANES, (j + 1) * LANES)
            o = acc_ref[0, pl.ds(r0, tile), cols] + acc_ref[1, pl.ds(r0, tile), cols]
            on = o * lax.rsqrt(_half_mean_square(o, lo) + RMS_EPS) * nw
            gate = _silu(g_ref[0, pl.ds(r0, tile), cols])
            o_ref[0, pl.ds(r0, tile), cols] = (on * gate).astype(BF16)
        return carry

    lax.fori_loop(0, n // tile, body, 0)


def _gla_kernel(q_ref, k_ref, v_ref, la_ref, g_ref, qc_ref, kc_ref, vc_ref, lac_ref, gc_ref,
                nw_ref, mcat_ref, lmask_ref, stmask_ref, o_ref, oc_ref, acc_ref, accc_ref, st_ref):
    n_chunks = q_ref.shape[1] // GLA_CHUNK
    c_chunks = qc_ref.shape[1] // GLA_CHUNK
    consts = (mcat_ref, lmask_ref, stmask_ref)
    st_ref[...] = jnp.zeros_like(st_ref)

    def steps(refs, acc, n):
        stages = _gla_stages(*refs, acc, st_ref, *consts)

        def body(i, carry):
            items = []
            for j in range(GLA_CHUNKS_PER_STEP):
                c = i * GLA_CHUNKS_PER_STEP + j
                items += [(c, 0), (n - 1 - c, 1)]
            _skewed(stages, items)
            return carry

        lax.fori_loop(0, n // GLA_CHUNKS_PER_STEP, body, 0)

    steps((qc_ref, kc_ref, vc_ref, lac_ref), accc_ref, c_chunks)
    steps((q_ref, k_ref, v_ref, la_ref), acc_ref, n_chunks)
    _gla_finish(acc_ref, g_ref, nw_ref, o_ref, 256)
    _gla_finish(accc_ref, gc_ref, nw_ref, oc_ref, 256)


def _gla(q, k, v, la, g, qc, kc, vc, lac, gc, nw, consts):
    B, N, _ = q.shape
    L = qc.shape[1]
    mcat, lmask, stmask = consts
    tok = lambda n, w: pl.BlockSpec((1, n, w), lambda b: (b, 0, 0))
    full = lambda a: pl.BlockSpec(a.shape, lambda b: (0,) * a.ndim)
    return pl.pallas_call(
        _gla_kernel,
        grid=(B,),
        in_specs=[tok(N, GLA_KW), tok(N, GLA_KW), tok(N, GLA_VW), tok(N, 2 * GLA_KW), tok(N, GLA_VW),
                  tok(L, GLA_KW), tok(L, GLA_KW), tok(L, GLA_VW), tok(L, 2 * GLA_KW), tok(L, GLA_VW),
                  full(nw), full(mcat), full(lmask), full(stmask)],
        out_specs=[tok(N, GLA_VW), tok(L, GLA_VW)],
        out_shape=[jax.ShapeDtypeStruct((B, N, GLA_VW), BF16), jax.ShapeDtypeStruct((B, L, GLA_VW), BF16)],
        scratch_shapes=[pltpu.VMEM((2, N, GLA_VW), F32), pltpu.VMEM((2, L, GLA_VW), F32),
                        pltpu.VMEM((2, GLA_VW, GLA_KW), F32)],
        compiler_params=_cparams(1),
        name="gla",
    )(q, k, v, la, g, qc, kc, vc, lac, gc, nw, mcat, lmask, stmask)


_FFN_CHUNKS = ((0, 768), (768, 1536), (1536, 2304), (2304, FFN_HIDDEN))


def _mix_ffn_kernel(x_ref, na_ref, gl_ref, ga_ref, wm_ref, g1_ref, l1g_ref, l1b_ref,
                    sc_ref, sh_ref, g2_ref, wi_ref, wo_ref, l2g_ref, l2b_ref, o_ref):
    ts = x_ref.shape[1] // ROW_SUBTILES
    x1_of, u_of, acc_of = {}, {}, {}

    def rows(sub):
        return slice(sub * ts, (sub + 1) * ts)

    def mixer_sublayer(sub):
        r = rows(sub)
        o = jnp.concatenate([na_ref[0, r, :], gl_ref[0, r, :], ga_ref[0, r, :]], axis=-1)
        y = DEEPNORM_ALPHA * x_ref[0, r, :] + g1_ref[0] * _dot(o, wm_ref[...])
        x1 = _layer_norm(y, l1g_ref[...], l1b_ref[...])
        x1_of[sub] = x1
        u_of[sub] = (x1 * (1.0 + sc_ref[0]) + sh_ref[0]).astype(BF16)

    def up(item):
        sub, (a0, a1) = item
        if sub not in u_of:
            mixer_sublayer(sub)
        u = u_of[sub]
        return item, _dot(u, wi_ref[:, a0:a1]), _dot(u, wi_ref[:, FFN_HIDDEN + a0:FFN_HIDDEN + a1])

    def gate(st):
        item, ha, hb = st
        return item, (_silu(ha) * hb).astype(BF16)

    def down(st):
        (sub, (a0, a1)), t = st
        part = _dot(t, wo_ref[a0:a1, :])
        acc_of[sub] = acc_of[sub] + part if sub in acc_of else part
        if a1 == FFN_HIDDEN:
            y = DEEPNORM_ALPHA * x1_of.pop(sub) + g2_ref[0] * acc_of.pop(sub)
            o_ref[0, rows(sub), :] = _layer_norm(y, l2g_ref[...], l2b_ref[...])

    _skewed((up, gate, down), [(sub, ch) for sub in range(ROW_SUBTILES) for ch in _FFN_CHUNKS])


def _mix_ffn(x, o_na, o_gl, o_ga, wm, g1, l1g, l1b, sc, sh, g2, wi, wo, l2g, l2b, tm):
    B, N, D = x.shape
    tspec = lambda width: pl.BlockSpec((1, tm, width), lambda b, i: (b, i, 0))
    mod = pl.BlockSpec((1, 1, D), lambda b, i: (b, 0, 0))
    vec = pl.BlockSpec((1, D), lambda b, i: (0, 0))
    resident = lambda a: pl.BlockSpec(a.shape, lambda b, i: (0, 0), pipeline_mode=pl.Buffered(1))
    return pl.pallas_call(
        _mix_ffn_kernel,
        grid=(B, N // tm),
        in_specs=[tspec(D), tspec(NA_W), tspec(GLA_VW), tspec(GQA_QW), resident(wm), mod, vec, vec,
                  mod, mod, mod, resident(wi), resident(wo), vec, vec],
        out_specs=tspec(D),
        out_shape=jax.ShapeDtypeStruct((B, N, D), F32),
        compiler_params=_cparams(2),
        name="mix_ffn",
    )(x, o_na, o_gl, o_ga, wm, g1, l1g, l1b, sc, sh, g2, wi, wo, l2g, l2b)


def _rope_tables(n):
    t = jnp.arange(n)
    row = (t // GRID_W).astype(F32)
    col = (t % GRID_W).astype(F32)
    inv_freq = ROPE_THETA ** (-jnp.arange(ROPE_AXIS_PAIRS, dtype=F32) / ROPE_AXIS_PAIRS)
    ang_r = row[:, None] * inv_freq
    ang_c = col[:, None] * inv_freq
    ang = jnp.concatenate([ang_r, ang_r, ang_c, ang_c], axis=-1)
    sign = jnp.where((jnp.arange(HEAD_DIM) % 32) < 16, -1.0, 1.0).astype(F32)
    cos = jnp.tile(jnp.cos(ang), (1, 2))
    sin = jnp.tile(jnp.sin(ang) * sign, (1, 2))
    return cos, sin


def _pair_major(w, axis):
    shape = w.shape
    lead, tail = shape[:axis], shape[axis + 1:]
    w = w.reshape(lead + (GQA_KV_HEADS, GQA_REP, HEAD_DIM) + tail)
    w = jnp.swapaxes(w, axis, axis + 1)
    return w.reshape(shape)


def kernel(x, c, ctx, c_ctx, w_ada, b_ada, w_in, na_rpb, gla_wa2, gla_ba, gla_norm_w, gqa_qnorm_w,
           gqa_knorm_w, w_out, ln1_g, ln1_b, w_ffn_in, w_ffn_out, ln2_g, ln2_b):
    B, N, D = x.shape
    L = ctx.shape[1]
    depth = w_in.shape[0]
    tm = 1024
    tmc = min(L, 1024)

    pad = (-(B + 1)) % 8
    cvec = jnp.concatenate([c, c_ctx[None, :], jnp.zeros((pad, D), F32)], axis=0)
    mods = _ada(cvec, w_ada, b_ada)

    w_lr = w_in[:, :, _O_GLLR:_O_GAQ].reshape(depth, D, 2, GLA_RANK).transpose(0, 2, 1, 3)
    w_z = _fold_gate_weights(w_lr, gla_wa2)

    cos, sin = _rope_tables(N)
    cos_c = jnp.ones((L, LANES), F32)
    sin_c = jnp.zeros((L, LANES), F32)
    gla_consts = _gla_constants()

    xc = ctx
    for l in range(depth):
        ctx_out = l < depth - 1
        m_lat = mods[l, :B].reshape(B, 6, 1, D)
        m_ctx = jnp.broadcast_to(mods[l, B].reshape(1, 6, 1, D), (B, 6, 1, D))
        sh1, sc1, g1, sh2, sc2, g2 = (m_lat[:, i] for i in range(6))
        sh1c, sc1c, g1c, sh2c, sc2c, g2c = (m_ctx[:, i] for i in range(6))

        wl = w_in[l]
        w_proj = jnp.concatenate(
            [wl[:, _O_NAQ:_O_GLQ], _pair_major(wl[:, _O_GAQ:_O_GAK], 1), wl[:, _O_GLQ:_O_GLLR],
             w_z[l], wl[:, _O_GAK:_O_END]], axis=1).astype(BF16)
        zb = gla_ba[l].reshape(1, 2 * GLA_KW)
        qnw = jnp.tile(gqa_qnorm_w[l], 2).reshape(1, LANES)
        knw = jnp.tile(gqa_knorm_w[l], 2).reshape(1, LANES)
        glnw = jnp.tile(gla_norm_w[l], 2).reshape(1, LANES)
        wo_l = w_out[l]
        w_o = jnp.concatenate([wo_l[:NA_W + GLA_VW], _pair_major(wo_l[NA_W + GLA_VW:], 0)], axis=0).astype(BF16)
        lg1, lb1 = ln1_g[l].reshape(1, D), ln1_b[l].reshape(1, D)
        lg2, lb2 = ln2_g[l].reshape(1, D), ln2_b[l].reshape(1, D)
        wi = w_ffn_in[l].astype(BF16)
        wo = w_ffn_out[l].astype(BF16)

        (na_q, na_k, na_v, ga_q, gl_q, gl_k, gl_v, gl_g, gl_la, ga_k, ga_v) = _inproj(
            x, sc1, sh1, w_proj, zb, qnw, knw, cos, sin, tm)
        (na_qc, na_kc, na_vc, ga_qc, gl_qc, gl_kc, gl_vc, gl_gc, gl_lac, ga_kc, ga_vc) = _inproj(
            xc, sc1c, sh1c, w_proj, zb, qnw, knw, cos_c, sin_c, tmc)

        bias = _na_bias(na_rpb[l].reshape(-1))
        o_na = _na_attention(na_q, na_k, na_v, na_kc, na_vc, bias)
        o_gl, oc_gl = _gla(gl_q, gl_k, gl_v, gl_la, gl_g, gl_qc, gl_kc, gl_vc, gl_lac, gl_gc, glnw, gla_consts)
        o_ga = _pair_attention(ga_q, ga_k, ga_v, ga_kc, ga_vc, shared_kv=True, tq=512)

        x = _mix_ffn(x, o_na, o_gl, o_ga, w_o, g1, lg1, lb1, sc2, sh2, g2, wi, wo, lg2, lb2, tm)

        if ctx_out:
            oc_na = _pair_attention(na_qc, na_kc, na_vc, shared_kv=False, tq=L)
            oc_ga = _pair_attention(ga_qc, ga_kc, ga_vc, shared_kv=True, tq=L)
            xc = _mix_ffn(xc, oc_na, oc_gl, oc_ga, w_o, g1c, lg1, lb1, sc2c, sh2c, g2c, wi, wo, lg2, lb2, tmc)
    return x
```

```python
import functools

import numpy as np
import jax
import jax.numpy as jnp
from jax import lax
from jax.experimental import pallas as pl
from jax.experimental.pallas import tpu as pltpu

F32 = jnp.float32
BF16 = jnp.bfloat16

D_MODEL = 1024
DEPTH = 2
GRID_W = 64
HEAD_DIM = 64
LANES = 128

NA_HEADS = 6
NA_WIN_ROWS = 8
NA_WIN_COLS = 16
NA_W = NA_HEADS * HEAD_DIM
NA_PAIRS = NA_W // LANES
NA_DR = 2 * NA_WIN_ROWS - 1
NA_DC = 2 * NA_WIN_COLS - 1
NA_ROWS_PER_STEP = 8

GLA_HEADS = 4
GLA_DK = 32
GLA_DV = 64
GLA_RANK = 16
GLA_GATE_NORM = 16.0
GLA_CHUNK = 64
GLA_KW = GLA_HEADS * GLA_DK
GLA_VW = GLA_HEADS * GLA_DV
GLA_LEVELS = 6
GLA_MILD_DECAY = 60.0
GLA_CHUNKS_PER_STEP = 4

GQA_Q_HEADS = 6
GQA_KV_HEADS = 2
GQA_REP = GQA_Q_HEADS // GQA_KV_HEADS
GQA_QW = GQA_Q_HEADS * HEAD_DIM
GQA_KVW = GQA_KV_HEADS * HEAD_DIM
GQA_PAIRS = GQA_QW // LANES
ROW_SUBTILES = 2
PAIR_ATTN_SUB_ROWS = 128
ROPE_THETA = 10000.0
ROPE_AXIS_PAIRS = HEAD_DIM // 4

FFN_HIDDEN = 2816
MIX_W = NA_W + GLA_VW + GQA_QW

DEEPNORM_ALPHA = (2.0 * DEPTH) ** 0.25
LN_EPS = 1e-5
RMS_EPS = 1e-6
NEG_BIG = -1e30
LOG2_E = 1.4426950408889634
Q_SCALE = HEAD_DIM ** -0.5 * LOG2_E

_O_NAQ, _O_NAK, _O_NAV = 0, 384, 768
_O_GLQ, _O_GLK, _O_GLV, _O_GLG, _O_GLLR = 1152, 1280, 1408, 1664, 1920
_O_GAQ, _O_GAK, _O_GAV, _O_END = 1952, 2336, 2464, 2592
_C_NAQ, _C_NAK, _C_NAV, _C_GAQ = 0, 384, 768, 1152
_C_GLQ, _C_GLK, _C_GLV, _C_GLG, _C_Z = 1536, 1664, 1792, 2048, 2304
_C_GAK, _C_GAV, _C_END = 2560, 2688, 2816

VMEM_LIMIT = 48 * 1024 * 1024


def _cparams(n_axes):
    return pltpu.CompilerParams(dimension_semantics=("arbitrary",) * n_axes,
                                vmem_limit_bytes=VMEM_LIMIT)


def _dot(a, b):
    return jnp.dot(a, b, preferred_element_type=F32)


def _dot_nt(a, b):
    return lax.dot_general(a, b, (((1,), (1,)), ((), ())), preferred_element_type=F32)


def _dot_tn(a, b):
    return lax.dot_general(a, b, (((0,), (0,)), ((), ())), preferred_element_type=F32)


def _silu(x):
    return x * jax.nn.sigmoid(x)


def _layer_norm(y, g, b):
    mu = jnp.mean(y, axis=-1, keepdims=True)
    d = y - mu
    var = jnp.mean(d * d, axis=-1, keepdims=True)
    return d * lax.rsqrt(var + LN_EPS) * g + b


def _skewed(stages, items):
    n, k = len(items), len(stages)
    live = {}
    for j in range(n + k - 1):
        for s in range(k):
            idx = j - s
            if 0 <= idx < n:
                live[s, idx] = stages[s](items[idx] if s == 0 else live.pop((s - 1, idx)))


def _half_mean_square(y, lo):
    s = y * y
    s_lo = jnp.sum(jnp.where(lo, s, 0.0), axis=-1, keepdims=True)
    s_hi = jnp.sum(jnp.where(lo, 0.0, s), axis=-1, keepdims=True)
    return jnp.where(lo, s_lo, s_hi) * (1.0 / HEAD_DIM)


_ADA_TN = 1024


def _ada_kernel(c_ref, w_ref, b_ref, o_ref):
    s = _silu(c_ref[...])
    o_ref[0] = jnp.dot(s, w_ref[0], precision=lax.Precision.HIGHEST,
                       preferred_element_type=F32) + b_ref[0]


def _ada(cvec, w_ada, b_ada):
    rows = cvec.shape[0]
    depth, d, n6 = w_ada.shape
    return pl.pallas_call(
        _ada_kernel,
        grid=(depth, n6 // _ADA_TN),
        in_specs=[pl.BlockSpec((rows, d), lambda l, j: (0, 0)),
                  pl.BlockSpec((1, d, _ADA_TN), lambda l, j: (l, 0, j)),
                  pl.BlockSpec((1, 1, _ADA_TN), lambda l, j: (l, 0, j))],
        out_specs=pl.BlockSpec((1, rows, _ADA_TN), lambda l, j: (l, 0, j)),
        out_shape=jax.ShapeDtypeStruct((depth, rows, n6), F32),
        compiler_params=_cparams(2),
        name="ada",
    )(cvec, w_ada, b_ada.reshape(depth, 1, n6))


def _fold_kernel(wlr_ref, wa2_ref, o_ref):
    for e in range(2):
        o_ref[0, :, e * GLA_KW:(e + 1) * GLA_KW] = jnp.dot(
            wlr_ref[0, e], wa2_ref[0, e], precision=lax.Precision.HIGHEST,
            preferred_element_type=F32)


def _fold_gate_weights(w_lr, wa2):
    depth = w_lr.shape[0]
    return pl.pallas_call(
        _fold_kernel,
        grid=(depth,),
        in_specs=[pl.BlockSpec((1, 2, D_MODEL, GLA_RANK), lambda l: (l, 0, 0, 0)),
                  pl.BlockSpec((1, 2, GLA_RANK, GLA_KW), lambda l: (l, 0, 0, 0))],
        out_specs=pl.BlockSpec((1, D_MODEL, 2 * GLA_KW), lambda l: (l, 0, 0)),
        out_shape=jax.ShapeDtypeStruct((depth, D_MODEL, 2 * GLA_KW), F32),
        compiler_params=_cparams(1),
        name="fold_gate",
    )(w_lr, wa2)


def _inproj_kernel(x_ref, sc_ref, sh_ref, w_ref, zb_ref, qnw_ref, knw_ref, cos_ref, sin_ref,
                   naq_ref, nak_ref, nav_ref, gaq_ref, glq_ref, glk_ref, glv_ref, glg_ref,
                   gla_ref, gak_ref, gav_ref):
    ts = x_ref.shape[1] // ROW_SUBTILES
    lane = lax.broadcasted_iota(jnp.int32, (ts, LANES), 1)
    lo = lane < HEAD_DIM
    first16 = (lane & 31) < 16
    u_of = {}

    def rows(sub):
        return slice(sub * ts, (sub + 1) * ts)

    def matmul(item):
        sub, (a, b, epilogue) = item
        if sub not in u_of:
            u_of[sub] = (x_ref[0, rows(sub), :] * (1.0 + sc_ref[0]) + sh_ref[0]).astype(BF16)
        return sub, epilogue, _dot(u_of[sub], w_ref[:, a:b])

    def norm_rope(y, w, r):
        yn = y * lax.rsqrt(_half_mean_square(y, lo) + RMS_EPS) * w
        rot = jnp.where(first16, pltpu.roll(yn, LANES - 16, 1), pltpu.roll(yn, 16, 1))
        return yn * cos_ref[r, :] + rot * sin_ref[r, :]

    def na_qk(r, y):
        naq_ref[0, r, :] = (y[:, :NA_W] * Q_SCALE).astype(BF16)
        nak_ref[0, r, :] = y[:, NA_W:].astype(BF16)

    def na_v_gqa_q(r, y):
        nav_ref[0, r, :] = y[:, :NA_W].astype(BF16)
        qnw = qnw_ref[...]
        for p in range(GQA_PAIRS):
            t = norm_rope(y[:, NA_W + p * LANES:NA_W + (p + 1) * LANES], qnw, r)
            gaq_ref[0, r, p * LANES:(p + 1) * LANES] = (t * Q_SCALE).astype(BF16)

    def gla_qkv(r, y):
        glq_ref[0, r, :] = y[:, :GLA_KW] * GLA_DK ** -0.5
        glk_ref[0, r, :] = y[:, GLA_KW:2 * GLA_KW]
        glv_ref[0, r, :] = y[:, 2 * GLA_KW:].astype(BF16)

    def gla_gates(r, y):
        glg_ref[0, r, :] = y[:, :GLA_VW]
        z = y[:, GLA_VW:] + zb_ref[...]
        log_sig = jnp.minimum(z, 0.0) - jnp.log1p(jnp.exp(-jnp.abs(z)))
        gla_ref[0, r, :] = log_sig * (1.0 / GLA_GATE_NORM)

    def gqa_kv(r, y):
        gak_ref[0, r, :] = norm_rope(y[:, :LANES], knw_ref[...], r).astype(BF16)
        gav_ref[0, r, :] = y[:, LANES:].astype(BF16)

    segments = [(_C_NAQ, _C_NAV, na_qk), (_C_NAV, _C_GLQ, na_v_gqa_q), (_C_GLQ, _C_GLG, gla_qkv),
                (_C_GLG, _C_GAK, gla_gates), (_C_GAK, _C_END, gqa_kv)]
    _skewed((matmul, lambda st: st[1](rows(st[0]), st[2])),
            [(sub, sg) for sub in range(ROW_SUBTILES) for sg in segments])


def _inproj(x, sc, sh, w, zb, qnw, knw, cos, sin, tm):
    B, N, D = x.shape
    tok = lambda width, dt: jax.ShapeDtypeStruct((B, N, width), dt)
    tspec = lambda width: pl.BlockSpec((1, tm, width), lambda b, i: (b, i, 0))
    vec = lambda width: pl.BlockSpec((1, width), lambda b, i: (0, 0))
    out_shapes = [tok(NA_W, BF16), tok(NA_W, BF16), tok(NA_W, BF16), tok(GQA_QW, BF16),
                  tok(GLA_KW, F32), tok(GLA_KW, F32), tok(GLA_VW, BF16), tok(GLA_VW, F32),
                  tok(2 * GLA_KW, F32), tok(GQA_KVW, BF16), tok(GQA_KVW, BF16)]
    return pl.pallas_call(
        _inproj_kernel,
        grid=(B, N // tm),
        in_specs=[tspec(D),
                  pl.BlockSpec((1, 1, D), lambda b, i: (b, 0, 0)),
                  pl.BlockSpec((1, 1, D), lambda b, i: (b, 0, 0)),
                  pl.BlockSpec((D, _C_END), lambda b, i: (0, 0)),
                  vec(2 * GLA_KW), vec(LANES), vec(LANES),
                  pl.BlockSpec((tm, LANES), lambda b, i: (i, 0)),
                  pl.BlockSpec((tm, LANES), lambda b, i: (i, 0))],
        out_specs=[tspec(s.shape[-1]) for s in out_shapes],
        out_shape=out_shapes,
        compiler_params=_cparams(2),
        name="inproj",
    )(x, sc, sh, w, zb, qnw, knw, cos, sin)


def _nabias_kernel(rpb_ref, o_ref):
    h = pl.program_id(0)
    qc = lax.broadcasted_iota(jnp.int32, (GRID_W, LANES), 0)
    lane = lax.broadcasted_iota(jnp.int32, (GRID_W, LANES), 1)
    kc = lane & (GRID_W - 1)
    hi = lane >= GRID_W
    idx = jnp.clip(kc - qc, -(NA_WIN_COLS - 1), NA_WIN_COLS - 1) + (NA_WIN_COLS - 1)
    start = jnp.clip(qc - NA_WIN_COLS // 2, 0, GRID_W - NA_WIN_COLS)
    col_in = jnp.logical_and(kc >= start, kc < start + NA_WIN_COLS)
    base = h * (NA_DR * NA_DC)

    pair_tables = []
    for dr0 in range(NA_DR - 1):

        def body(j, t, dr0=dr0):
            s0 = rpb_ref[base + dr0 * NA_DC + j]
            s1 = rpb_ref[base + (dr0 + 1) * NA_DC + j]
            return jnp.where(idx == j, jnp.where(hi, s1, s0), t)

        t = lax.fori_loop(0, NA_DC, body, jnp.zeros((GRID_W, LANES), F32))
        pair_tables.append(jnp.where(col_in, t * LOG2_E, NEG_BIG))
    for oi in range(NA_WIN_ROWS):
        for i in range(NA_WIN_ROWS // 2):
            o_ref[0, oi, :, i * LANES:(i + 1) * LANES] = pair_tables[oi + 2 * i]


def _na_bias(rpb_flat):
    return pl.pallas_call(
        _nabias_kernel,
        grid=(NA_HEADS,),
        in_specs=[pl.BlockSpec(memory_space=pltpu.SMEM)],
        out_specs=pl.BlockSpec((1, NA_WIN_ROWS, GRID_W, NA_WIN_ROWS * GRID_W),
                               lambda h: (h, 0, 0, 0)),
        out_shape=jax.ShapeDtypeStruct((NA_HEADS, NA_WIN_ROWS, GRID_W, NA_WIN_ROWS * GRID_W), F32),
        compiler_params=_cparams(1),
        name="na_bias",
    )(rpb_flat)


def _na_kernel(q_ref, k_ref, v_ref, kc_ref, vc_ref, b_ref, o_ref):
    n = q_ref.shape[1]
    rows = n // GRID_W
    win = NA_WIN_ROWS * GRID_W
    lane = lax.broadcasted_iota(jnp.int32, (GRID_W, LANES), 1)
    lo = lane < HEAD_DIM
    kc = kc_ref[0]
    vc = vc_ref[0]
    zero = jnp.zeros((GRID_W, LANES), BF16)

    def scores(r):
        rs = jnp.clip(r - NA_WIN_ROWS // 2, 0, rows - NA_WIN_ROWS)
        oi = rs - r + (NA_WIN_ROWS - 1)
        q0 = pl.multiple_of(r * GRID_W, GRID_W)
        k0 = pl.multiple_of(rs * GRID_W, GRID_W)
        q = q_ref[0, pl.ds(q0, GRID_W), :]
        kw = k_ref[0, pl.ds(k0, win), :]
        qs = jnp.concatenate([jnp.where(lo, q, zero), jnp.where(lo, zero, q)], axis=0)
        s_lat = _dot_nt(qs, kw) + jnp.concatenate([b_ref[0, oi], b_ref[1, oi]], axis=0)
        s_ctx = _dot_nt(qs, kc)
        return q0, k0, s_lat, s_ctx

    def softmax(st):
        q0, k0, s_lat, s_ctx = st
        m = jnp.maximum(jnp.max(s_lat, axis=-1, keepdims=True),
                        jnp.max(s_ctx, axis=-1, keepdims=True))
        p_lat = jnp.exp2(s_lat - m)
        p_ctx = jnp.exp2(s_ctx - m)
        den = jnp.sum(p_lat, axis=-1, keepdims=True) + jnp.sum(p_ctx, axis=-1, keepdims=True)
        return q0, k0, p_lat.astype(BF16), p_ctx.astype(BF16), den

    def values(st):
        q0, k0, p_lat, p_ctx, den = st
        vw = v_ref[0, pl.ds(k0, win), :]
        o = (_dot(p_lat, vw) + _dot(p_ctx, vc)) / den
        o_ref[0, pl.ds(q0, GRID_W), :] = jnp.where(lo, o[:GRID_W], o[GRID_W:]).astype(BF16)

    def body(i, carry):
        _skewed((scores, softmax, values),
                [i * NA_ROWS_PER_STEP + j for j in range(NA_ROWS_PER_STEP)])
        return carry

    lax.fori_loop(0, rows // NA_ROWS_PER_STEP, body, 0)


def _na_attention(q, k, v, kc, vc, bias):
    B, N, _ = q.shape
    L = kc.shape[1]
    lat = pl.BlockSpec((1, N, LANES), lambda b, p: (b, 0, p))
    cx = pl.BlockSpec((1, L, LANES), lambda b, p: (b, 0, p))
    return pl.pallas_call(
        _na_kernel,
        grid=(B, NA_PAIRS),
        in_specs=[lat, lat, lat, cx, cx,
                  pl.BlockSpec((2, NA_WIN_ROWS, GRID_W, NA_WIN_ROWS * GRID_W),
                               lambda b, p: (p, 0, 0, 0))],
        out_specs=lat,
        out_shape=jax.ShapeDtypeStruct((B, N, NA_W), BF16),
        compiler_params=_cparams(2),
        name="na_attn",
    )(q, k, v, kc, vc, bias)


def _pair_attn_kernel(*refs, two_sources):
    if two_sources:
        q_ref, ka_ref, va_ref, kb_ref, vb_ref, o_ref = refs
    else:
        q_ref, ka_ref, va_ref, o_ref = refs
    tq = q_ref.shape[1]
    ts = min(tq, PAIR_ATTN_SUB_ROWS)
    lane = lax.broadcasted_iota(jnp.int32, (ts, LANES), 1)
    lo = lane < HEAD_DIM
    zero = jnp.zeros((ts, LANES), BF16)

    def with_ones(v):
        return jnp.concatenate([v, jnp.ones_like(v)], axis=1)

    ka = ka_ref[0]
    va = with_ones(va_ref[0])
    if two_sources:
        kb = kb_ref[0]
        vb = with_ones(vb_ref[0])
    def scores(item):
        sub, h = item
        q = q_ref[0, sub * ts:(sub + 1) * ts, :]
        qm = jnp.where(lo, q, zero) if h == 0 else jnp.where(lo, zero, q)
        s_a = _dot_nt(qm, ka)
        s_b = _dot_nt(qm, kb) if two_sources else None
        return sub, h, s_a, s_b

    def softmax(st):
        sub, h, s_a, s_b = st
        m = jnp.max(s_a, axis=-1, keepdims=True)
        if two_sources:
            m = jnp.maximum(m, jnp.max(s_b, axis=-1, keepdims=True))
        p_a = jnp.exp2(s_a - m).astype(BF16)
        p_b = jnp.exp2(s_b - m).astype(BF16) if two_sources else None
        return sub, h, p_a, p_b

    done = {}

    def values(st):
        sub, h, p_a, p_b = st
        o = _dot(p_a, va)
        if two_sources:
            o = o + _dot(p_b, vb)
        done[sub, h] = o[:, :LANES] / o[:, LANES:]
        if h == 1:
            o_ref[0, sub * ts:(sub + 1) * ts, :] = jnp.where(
                lo, done.pop((sub, 0)), done.pop((sub, 1))).astype(BF16)

    _skewed((scores, softmax, values), [(sub, h) for sub in range(tq // ts) for h in range(2)])


def _pair_attention(q, ka, va, kb=None, vb=None, *, shared_kv, tq):
    B, Nq, W = q.shape
    pairs = W // LANES
    kv_map = (lambda b, p, i: (b, 0, 0)) if shared_kv else (lambda b, p, i: (b, 0, p))
    qspec = pl.BlockSpec((1, tq, LANES), lambda b, p, i: (b, i, p))
    in_specs = [qspec,
                pl.BlockSpec((1, ka.shape[1], LANES), kv_map),
                pl.BlockSpec((1, ka.shape[1], LANES), kv_map)]
    args = [q, ka, va]
    if kb is not None:
        in_specs += [pl.BlockSpec((1, kb.shape[1], LANES), kv_map),
                     pl.BlockSpec((1, kb.shape[1], LANES), kv_map)]
        args += [kb, vb]
    return pl.pallas_call(
        functools.partial(_pair_attn_kernel, two_sources=kb is not None),
        grid=(B, pairs, Nq // tq),
        in_specs=in_specs,
        out_specs=qspec,
        out_shape=jax.ShapeDtypeStruct((B, Nq, W), BF16),
        compiler_params=_cparams(3),
        name="pair_attn",
    )(*args)


def _gla_constants():
    c = GLA_CHUNK
    t = np.arange(c)[:, None]
    i = np.arange(c)[None, :]
    mcat = np.zeros((2, (GLA_LEVELS + 1) * c, c), np.float32)
    masks = np.zeros((2, GLA_LEVELS + 2, c, c), np.float32)
    for d in range(2):
        masks[d, GLA_LEVELS + 1] = (i <= t) if d == 0 else (i >= t)
        for l in range(1, GLA_LEVELS + 1):
            m = 2 ** l
            same = (t // m) == (i // m)
            q_side = same & ((i <= t) if d == 0 else (i >= t))
            k_side = same & ((i > t) if d == 0 else (i < t))
            if l < GLA_LEVELS:
                is_q_row = ((t // m) % 2 == 1) if d == 0 else ((t // m) % 2 == 0)
                mcat[d, (l - 1) * c:l * c] = np.where(is_q_row, q_side, k_side)
            else:
                mcat[d, (l - 1) * c:l * c] = q_side
                mcat[d, l * c:(l + 1) * c] = k_side
        masks[d, 0] = (t == i)
        for l in range(GLA_LEVELS):
            m = 2 ** l
            tb, sb = t // m, i // m
            if d == 0:
                masks[d, l + 1] = (tb % 2 == 1) & (sb == tb - 1)
            else:
                masks[d, l + 1] = (tb % 2 == 0) & (sb == tb + 1)
    masks = np.tile(masks, (1, 1, 1, GLA_HEADS))
    rows = np.arange(GLA_VW)[:, None] // GLA_DV
    cols = np.arange(GLA_KW)[None, :] // GLA_DK
    stmask = (rows == cols).astype(np.float32)
    return jnp.asarray(mcat, BF16), jnp.asarray(masks, F32), jnp.asarray(stmask, F32)


def _gla_stages(q_ref, k_ref, v_ref, la_ref, acc_ref, st_ref, mcat_ref, lmask_ref, stmask_ref, mild):
    ck = GLA_CHUNK
    whole = slice((GLA_LEVELS - 1) * ck, (GLA_LEVELS + 1) * ck)
    lane_k = lax.broadcasted_iota(jnp.int32, (1, GLA_KW), 1) // GLA_DK
    lane_v = lax.broadcasted_iota(jnp.int32, (1, GLA_VW), 1) // GLA_DV
    row = lax.broadcasted_iota(jnp.int32, (ck, GLA_KW), 0)

    def block_rows(x, lane_head):
        return jnp.concatenate([jnp.where(lane_head == h, x, 0.0) for h in range(GLA_HEADS)], axis=0)

    def exponents(item):
        c, d = item
        r0 = pl.multiple_of(c * ck, ck)
        g = la_ref[0, pl.ds(r0, ck), d * GLA_KW:(d + 1) * GLA_KW]
        g1 = g.astype(BF16)
        g2 = (g - g1.astype(F32)).astype(BF16)
        mc = mcat_ref[d, whole, :] if mild else mcat_ref[d]
        e = _dot(mc, g1) + _dot(mc, g2)
        return r0, d, g, e

    def in_chunk(st):
        r0, d, g, e = st
        q = q_ref[0, pl.ds(r0, ck), :]
        k = k_ref[0, pl.ds(r0, ck), :]
        q_chunk = e[-2 * ck:-ck]
        k_chunk = e[-ck:]
        last = ck - 1 if d == 0 else 0
        total = q_chunk[last:last + 1]
        q_in = (q * jnp.exp(q_chunk)).astype(BF16)

        def level_scores(qt, kt, idx):
            kb = block_rows(kt, lane_k).astype(BF16)
            return lmask_ref[d, idx] * _dot_nt(qt, kb)

        if mild:
            a = level_scores(q_in, k * jnp.exp(-q_chunk), GLA_LEVELS + 1)
        else:
            q_row0 = (row & 1) == (1 - d)
            a = level_scores(q.astype(BF16), k, 0)
            for l in range(GLA_LEVELS):
                dec = jnp.exp(jnp.where(q_row0, g, 0.0) if l == 0 else e[(l - 1) * ck:l * ck])
                a = a + level_scores((q * dec).astype(BF16), k * dec, l + 1)
        return r0, d, a.astype(BF16), q_in, (k * jnp.exp(k_chunk)).astype(BF16), jnp.exp(total)

    def state(st):
        r0, d, a, q_in, k_out, decay = st
        v = v_ref[0, pl.ds(r0, ck), :]
        vbd = block_rows(v.astype(F32), lane_v).astype(BF16)
        s_prev = st_ref[d]
        acc_ref[d, pl.ds(r0, ck), :] = _dot(a, vbd) + _dot_nt(q_in, s_prev.astype(BF16))
        st_ref[d] = s_prev * decay + stmask_ref[...] * _dot_tn(v, k_out)

    return exponents, in_chunk, state


def _gla_finish(acc_ref, g_ref, nw_ref, o_ref, tile):
    n = acc_ref.shape[1]
    lane = lax.broadcasted_iota(jnp.int32, (tile, LANES), 1)
    lo = lane < GLA_DV
    nw = nw_ref[...]

    def body(i, carry):
        r0 = pl.multiple_of(i * tile, tile)
        for j in range(GLA_VW // LANES):
            cols = slice(j * LANES, (j + 1) * LANES)
            o = acc_ref[0, pl.ds(r0, tile), cols] + acc_ref[1, pl.ds(r0, tile), cols]
            on = o * lax.rsqrt(_half_mean_square(o, lo) + RMS_EPS) * nw
            gate = _silu(g_ref[0, pl.ds(r0, tile), cols])
            o_ref[0, pl.ds(r0, tile), cols] = (on * gate).astype(BF16)
        return carry

    lax.fori_loop(0, n // tile, body, 0)


def _gla_kernel(q_ref, k_ref, v_ref, la_ref, g_ref, qc_ref, kc_ref, vc_ref, lac_ref, gc_ref,
                nw_ref, mcat_ref, lmask_ref, stmask_ref, o_ref, oc_ref, acc_ref, accc_ref, st_ref):
    n_chunks = q_ref.shape[1] // GLA_CHUNK
    c_chunks = qc_ref.shape[1] // GLA_CHUNK
    consts = (mcat_ref, lmask_ref, stmask_ref)
    st_ref[...] = jnp.zeros_like(st_ref)

    def min_chunk_sum(ref):
        la = ref[0]
        sums = jnp.sum(la.reshape(la.shape[0] // GLA_CHUNK, GLA_CHUNK, la.shape[1]), axis=1)
        return jnp.min(sums)

    mild = jnp.minimum(min_chunk_sum(la_ref), min_chunk_sum(lac_ref)) >= -GLA_MILD_DECAY

    def steps(refs, acc, n, is_mild):
        stages = _gla_stages(*refs, acc, st_ref, *consts, is_mild)

        def body(i, carry):
            items = []
            for j in range(GLA_CHUNKS_PER_STEP):
                c = i * GLA_CHUNKS_PER_STEP + j
                items += [(c, 0), (n - 1 - c, 1)]
            _skewed(stages, items)
            return carry

        lax.fori_loop(0, n // GLA_CHUNKS_PER_STEP, body, 0)

    for is_mild in (True, False):
        @pl.when(mild if is_mild else jnp.logical_not(mild))
        def _():
            steps((qc_ref, kc_ref, vc_ref, lac_ref), accc_ref, c_chunks, is_mild)
            steps((q_ref, k_ref, v_ref, la_ref), acc_ref, n_chunks, is_mild)

    _gla_finish(acc_ref, g_ref, nw_ref, o_ref, 256)
    _gla_finish(accc_ref, gc_ref, nw_ref, oc_ref, 256)


def _gla(q, k, v, la, g, qc, kc, vc, lac, gc, nw, consts):
    B, N, _ = q.shape
    L = qc.shape[1]
    mcat, lmask, stmask = consts
    tok = lambda n, w: pl.BlockSpec((1, n, w), lambda b: (b, 0, 0))
    full = lambda a: pl.BlockSpec(a.shape, lambda b: (0,) * a.ndim)
    return pl.pallas_call(
        _gla_kernel,
        grid=(B,),
        in_specs=[tok(N, GLA_KW), tok(N, GLA_KW), tok(N, GLA_VW), tok(N, 2 * GLA_KW), tok(N, GLA_VW),
                  tok(L, GLA_KW), tok(L, GLA_KW), tok(L, GLA_VW), tok(L, 2 * GLA_KW), tok(L, GLA_VW),
                  full(nw), full(mcat), full(lmask), full(stmask)],
        out_specs=[tok(N, GLA_VW), tok(L, GLA_VW)],
        out_shape=[jax.ShapeDtypeStruct((B, N, GLA_VW), BF16), jax.ShapeDtypeStruct((B, L, GLA_VW), BF16)],
        scratch_shapes=[pltpu.VMEM((2, N, GLA_VW), F32), pltpu.VMEM((2, L, GLA_VW), F32),
                        pltpu.VMEM((2, GLA_VW, GLA_KW), F32)],
        compiler_params=_cparams(1),
        name="gla",
    )(q, k, v, la, g, qc, kc, vc, lac, gc, nw, mcat, lmask, stmask)


_FFN_CHUNKS = ((0, 768), (768, 1536), (1536, 2304), (2304, FFN_HIDDEN))


def _mix_ffn_kernel(x_ref, na_ref, gl_ref, ga_ref, wm_ref, g1_ref, l1g_ref, l1b_ref,
                    sc_ref, sh_ref, g2_ref, wi_ref, wo_ref, l2g_ref, l2b_ref, o_ref):
    ts = x_ref.shape[1] // ROW_SUBTILES
    x1_of, u_of, acc_of = {}, {}, {}

    def rows(sub):
        return slice(sub * ts, (sub + 1) * ts)

    def mixer_sublayer(sub):
        r = rows(sub)
        o = jnp.concatenate([na_ref[0, r, :], gl_ref[0, r, :], ga_ref[0, r, :]], axis=-1)
        y = DEEPNORM_ALPHA * x_ref[0, r, :] + g1_ref[0] * _dot(o, wm_ref[...])
        x1 = _layer_norm(y, l1g_ref[...], l1b_ref[...])
        x1_of[sub] = x1
        u_of[sub] = (x1 * (1.0 + sc_ref[0]) + sh_ref[0]).astype(BF16)

    def up(item):
        sub, (a0, a1) = item
        if sub not in u_of:
            mixer_sublayer(sub)
        u = u_of[sub]
        return item, _dot(u, wi_ref[:, a0:a1]), _dot(u, wi_ref[:, FFN_HIDDEN + a0:FFN_HIDDEN + a1])

    def gate(st):
        item, ha, hb = st
        return item, (_silu(ha) * hb).astype(BF16)

    def down(st):
        (sub, (a0, a1)), t = st
        part = _dot(t, wo_ref[a0:a1, :])
        acc_of[sub] = acc_of[sub] + part if sub in acc_of else part
        if a1 == FFN_HIDDEN:
            y = DEEPNORM_ALPHA * x1_of.pop(sub) + g2_ref[0] * acc_of.pop(sub)
            o_ref[0, rows(sub), :] = _layer_norm(y, l2g_ref[...], l2b_ref[...])

    _skewed((up, gate, down), [(sub, ch) for sub in range(ROW_SUBTILES) for ch in _FFN_CHUNKS])


def _mix_ffn(x, o_na, o_gl, o_ga, wm, g1, l1g, l1b, sc, sh, g2, wi, wo, l2g, l2b, tm):
    B, N, D = x.shape
    tspec = lambda width: pl.BlockSpec((1, tm, width), lambda b, i: (b, i, 0))
    mod = pl.BlockSpec((1, 1, D), lambda b, i: (b, 0, 0))
    vec = pl.BlockSpec((1, D), lambda b, i: (0, 0))
    resident = lambda a: pl.BlockSpec(a.shape, lambda b, i: (0, 0), pipeline_mode=pl.Buffered(1))
    return pl.pallas_call(
        _mix_ffn_kernel,
        grid=(B, N // tm),
        in_specs=[tspec(D), tspec(NA_W), tspec(GLA_VW), tspec(GQA_QW), resident(wm), mod, vec, vec,
                  mod, mod, mod, resident(wi), resident(wo), vec, vec],
        out_specs=tspec(D),
        out_shape=jax.ShapeDtypeStruct((B, N, D), F32),
        compiler_params=_cparams(2),
        name="mix_ffn",
    )(x, o_na, o_gl, o_ga, wm, g1, l1g, l1b, sc, sh, g2, wi, wo, l2g, l2b)


def _rope_tables(n):
    t = jnp.arange(n)
    row = (t // GRID_W).astype(F32)
    col = (t % GRID_W).astype(F32)
    inv_freq = ROPE_THETA ** (-jnp.arange(ROPE_AXIS_PAIRS, dtype=F32) / ROPE_AXIS_PAIRS)
    ang_r = row[:, None] * inv_freq
    ang_c = col[:, None] * inv_freq
    ang = jnp.concatenate([ang_r, ang_r, ang_c, ang_c], axis=-1)
    sign = jnp.where((jnp.arange(HEAD_DIM) % 32) < 16, -1.0, 1.0).astype(F32)
    cos = jnp.tile(jnp.cos(ang), (1, 2))
    sin = jnp.tile(jnp.sin(ang) * sign, (1, 2))
    return cos, sin


def _pair_major(w, axis):
    shape = w.shape
    lead, tail = shape[:axis], shape[axis + 1:]
    w = w.reshape(lead + (GQA_KV_HEADS, GQA_REP, HEAD_DIM) + tail)
    w = jnp.swapaxes(w, axis, axis + 1)
    return w.reshape(shape)


def kernel(x, c, ctx, c_ctx, w_ada, b_ada, w_in, na_rpb, gla_wa2, gla_ba, gla_norm_w, gqa_qnorm_w,
           gqa_knorm_w, w_out, ln1_g, ln1_b, w_ffn_in, w_ffn_out, ln2_g, ln2_b):
    B, N, D = x.shape
    L = ctx.shape[1]
    depth = w_in.shape[0]
    tm = 1024
    tmc = min(L, 1024)

    pad = (-(B + 1)) % 8
    cvec = jnp.concatenate([c, c_ctx[None, :], jnp.zeros((pad, D), F32)], axis=0)
    mods = _ada(cvec, w_ada, b_ada)

    w_lr = w_in[:, :, _O_GLLR:_O_GAQ].reshape(depth, D, 2, GLA_RANK).transpose(0, 2, 1, 3)
    w_z = _fold_gate_weights(w_lr, gla_wa2)

    cos, sin = _rope_tables(N)
    cos_c = jnp.ones((L, LANES), F32)
    sin_c = jnp.zeros((L, LANES), F32)
    gla_consts = _gla_constants()

    xc = ctx
    for l in range(depth):
        ctx_out = l < depth - 1
        m_lat = mods[l, :B].reshape(B, 6, 1, D)
        m_ctx = jnp.broadcast_to(mods[l, B].reshape(1, 6, 1, D), (B, 6, 1, D))
        sh1, sc1, g1, sh2, sc2, g2 = (m_lat[:, i] for i in range(6))
        sh1c, sc1c, g1c, sh2c, sc2c, g2c = (m_ctx[:, i] for i in range(6))

        wl = w_in[l]
        w_proj = jnp.concatenate(
            [wl[:, _O_NAQ:_O_GLQ], _pair_major(wl[:, _O_GAQ:_O_GAK], 1), wl[:, _O_GLQ:_O_GLLR],
             w_z[l], wl[:, _O_GAK:_O_END]], axis=1).astype(BF16)
        zb = gla_ba[l].reshape(1, 2 * GLA_KW)
        qnw = jnp.tile(gqa_qnorm_w[l], 2).reshape(1, LANES)
        knw = jnp.tile(gqa_knorm_w[l], 2).reshape(1, LANES)
        glnw = jnp.tile(gla_norm_w[l], 2).reshape(1, LANES)
        wo_l = w_out[l]
        w_o = jnp.concatenate([wo_l[:NA_W + GLA_VW], _pair_major(wo_l[NA_W + GLA_VW:], 0)], axis=0).astype(BF16)
        lg1, lb1 = ln1_g[l].reshape(1, D), ln1_b[l].reshape(1, D)
        lg2, lb2 = ln2_g[l].reshape(1, D), ln2_b[l].reshape(1, D)
        wi = w_ffn_in[l].astype(BF16)
        wo = w_ffn_out[l].astype(BF16)

        (na_q, na_k, na_v, ga_q, gl_q, gl_k, gl_v, gl_g, gl_la, ga_k, ga_v) = _inproj(
            x, sc1, sh1, w_proj, zb, qnw, knw, cos, sin, tm)
        (na_qc, na_kc, na_vc, ga_qc, gl_qc, gl_kc, gl_vc, gl_gc, gl_lac, ga_kc, ga_vc) = _inproj(
            xc, sc1c, sh1c, w_proj, zb, qnw, knw, cos_c, sin_c, tmc)

        bias = _na_bias(na_rpb[l].reshape(-1))
        o_na = _na_attention(na_q, na_k, na_v, na_kc, na_vc, bias)
        o_gl, oc_gl = _gla(gl_q, gl_k, gl_v, gl_la, gl_g, gl_qc, gl_kc, gl_vc, gl_lac, gl_gc, glnw, gla_consts)
        o_ga = _pair_attention(ga_q, ga_k, ga_v, ga_kc, ga_vc, shared_kv=True, tq=512)

        x = _mix_ffn(x, o_na, o_gl, o_ga, w_o, g1, lg1, lb1, sc2, sh2, g2, wi, wo, lg2, lb2, tm)

        if ctx_out:
            oc_na = _pair_attention(na_qc, na_kc, na_vc, shared_kv=False, tq=L)
            oc_ga = _pair_attention(ga_qc, ga_kc, ga_vc, shared_kv=True, tq=L)
            xc = _mix_ffn(xc, oc_na, oc_gl, oc_ga, w_o, g1c, lg1, lb1, sc2c, sh2c, g2c, wi, wo, lg2, lb2, tmc)
    return x
```

```python
import functools

import numpy as np
import jax
import jax.numpy as jnp
from jax import lax
from jax.experimental import pallas as pl
from jax.experimental.pallas import tpu as pltpu

F32 = jnp.float32
BF16 = jnp.bfloat16

D_MODEL = 1024
DEPTH = 2
GRID_W = 64
HEAD_DIM = 64
LANES = 128

NA_HEADS = 6
NA_WIN_ROWS = 8
NA_WIN_COLS = 16
NA_W = NA_HEADS * HEAD_DIM
NA_PAIRS = NA_W // LANES
NA_DR = 2 * NA_WIN_ROWS - 1
NA_DC = 2 * NA_WIN_COLS - 1
NA_ROWS_PER_STEP = 8

GLA_HEADS = 4
GLA_DK = 32
GLA_DV = 64
GLA_RANK = 16
GLA_GATE_NORM = 16.0
GLA_CHUNK = 64
GLA_KW = GLA_HEADS * GLA_DK
GLA_VW = GLA_HEADS * GLA_DV
GLA_LEVELS = 6
GLA_MILD_DECAY = 60.0
GLA_CHUNKS_PER_STEP = 8

GQA_Q_HEADS = 6
GQA_KV_HEADS = 2
GQA_REP = GQA_Q_HEADS // GQA_KV_HEADS
GQA_QW = GQA_Q_HEADS * HEAD_DIM
GQA_KVW = GQA_KV_HEADS * HEAD_DIM
GQA_PAIRS = GQA_QW // LANES
INPROJ_SUBTILES = 4
FFN_SUBTILES = 2
PAIR_ATTN_SUB_ROWS = 128
ROPE_THETA = 10000.0
ROPE_AXIS_PAIRS = HEAD_DIM // 4

FFN_HIDDEN = 2816
MIX_W = NA_W + GLA_VW + GQA_QW

DEEPNORM_ALPHA = (2.0 * DEPTH) ** 0.25
LN_EPS = 1e-5
RMS_EPS = 1e-6
NEG_BIG = -1e30
LOG2_E = 1.4426950408889634
Q_SCALE = HEAD_DIM ** -0.5 * LOG2_E

_O_NAQ, _O_NAK, _O_NAV = 0, 384, 768
_O_GLQ, _O_GLK, _O_GLV, _O_GLG, _O_GLLR = 1152, 1280, 1408, 1664, 1920
_O_GAQ, _O_GAK, _O_GAV, _O_END = 1952, 2336, 2464, 2592
_C_NAQ, _C_NAK, _C_NAV, _C_GAQ = 0, 384, 768, 1152
_C_GLQ, _C_GLK, _C_GLV, _C_GLG, _C_Z = 1536, 1664, 1792, 2048, 2304
_C_GAK, _C_GAV, _C_END = 2560, 2688, 2816

VMEM_LIMIT = 48 * 1024 * 1024


def _cparams(n_axes):
    return pltpu.CompilerParams(dimension_semantics=("arbitrary",) * n_axes,
                                vmem_limit_bytes=VMEM_LIMIT)


def _dot(a, b):
    return jnp.dot(a, b, preferred_element_type=F32)


def _dot_nt(a, b):
    return lax.dot_general(a, b, (((1,), (1,)), ((), ())), preferred_element_type=F32)


def _dot_tn(a, b):
    return lax.dot_general(a, b, (((0,), (0,)), ((), ())), preferred_element_type=F32)


def _silu(x):
    return x * jax.nn.sigmoid(x)


def _layer_norm(y, g, b):
    mu = jnp.mean(y, axis=-1, keepdims=True)
    d = y - mu
    var = jnp.mean(d * d, axis=-1, keepdims=True)
    return d * lax.rsqrt(var + LN_EPS) * g + b


def _skewed(stages, items):
    n, k = len(items), len(stages)
    live = {}
    for j in range(n + k - 1):
        for s in range(k):
            idx = j - s
            if 0 <= idx < n:
                live[s, idx] = stages[s](items[idx] if s == 0 else live.pop((s - 1, idx)))


def _half_mean_square(y, lo):
    s = y * y
    s_lo = jnp.sum(jnp.where(lo, s, 0.0), axis=-1, keepdims=True)
    s_hi = jnp.sum(jnp.where(lo, 0.0, s), axis=-1, keepdims=True)
    return jnp.where(lo, s_lo, s_hi) * (1.0 / HEAD_DIM)


_ADA_TN = 1024


def _ada_kernel(c_ref, w_ref, b_ref, o_ref):
    s = _silu(c_ref[...])
    o_ref[0] = jnp.dot(s, w_ref[0], precision=lax.Precision.HIGHEST,
                       preferred_element_type=F32) + b_ref[0]


def _ada(cvec, w_ada, b_ada):
    rows = cvec.shape[0]
    depth, d, n6 = w_ada.shape
    return pl.pallas_call(
        _ada_kernel,
        grid=(depth, n6 // _ADA_TN),
        in_specs=[pl.BlockSpec((rows, d), lambda l, j: (0, 0)),
                  pl.BlockSpec((1, d, _ADA_TN), lambda l, j: (l, 0, j)),
                  pl.BlockSpec((1, 1, _ADA_TN), lambda l, j: (l, 0, j))],
        out_specs=pl.BlockSpec((1, rows, _ADA_TN), lambda l, j: (l, 0, j)),
        out_shape=jax.ShapeDtypeStruct((depth, rows, n6), F32),
        compiler_params=_cparams(2),
        name="ada",
    )(cvec, w_ada, b_ada.reshape(depth, 1, n6))


def _fold_kernel(wlr_ref, wa2_ref, o_ref):
    for e in range(2):
        o_ref[0, :, e * GLA_KW:(e + 1) * GLA_KW] = jnp.dot(
            wlr_ref[0, e], wa2_ref[0, e], precision=lax.Precision.HIGHEST,
            preferred_element_type=F32)


def _fold_gate_weights(w_lr, wa2):
    depth = w_lr.shape[0]
    return pl.pallas_call(
        _fold_kernel,
        grid=(depth,),
        in_specs=[pl.BlockSpec((1, 2, D_MODEL, GLA_RANK), lambda l: (l, 0, 0, 0)),
                  pl.BlockSpec((1, 2, GLA_RANK, GLA_KW), lambda l: (l, 0, 0, 0))],
        out_specs=pl.BlockSpec((1, D_MODEL, 2 * GLA_KW), lambda l: (l, 0, 0)),
        out_shape=jax.ShapeDtypeStruct((depth, D_MODEL, 2 * GLA_KW), F32),
        compiler_params=_cparams(1),
        name="fold_gate",
    )(w_lr, wa2)


def _inproj_kernel(x_ref, sc_ref, sh_ref, w_ref, zb_ref, qnw_ref, knw_ref, cos_ref, sin_ref,
                   naq_ref, nak_ref, nav_ref, gaq_ref, glq_ref, glk_ref, glv_ref, glg_ref,
                   gla_ref, gak_ref, gav_ref):
    ts = x_ref.shape[1] // INPROJ_SUBTILES
    lane = lax.broadcasted_iota(jnp.int32, (ts, LANES), 1)
    lo = lane < HEAD_DIM
    first16 = (lane & 31) < 16
    u_of = {}

    def rows(sub):
        return slice(sub * ts, (sub + 1) * ts)

    def matmul(item):
        sub, (a, b, epilogue) = item
        if sub not in u_of:
            u_of[sub] = (x_ref[0, rows(sub), :] * (1.0 + sc_ref[0]) + sh_ref[0]).astype(BF16)
        return sub, epilogue, _dot(u_of[sub], w_ref[:, a:b])

    def norm_rope(y, w, r):
        yn = y * lax.rsqrt(_half_mean_square(y, lo) + RMS_EPS) * w
        rot = jnp.where(first16, pltpu.roll(yn, LANES - 16, 1), pltpu.roll(yn, 16, 1))
        return yn * cos_ref[r, :] + rot * sin_ref[r, :]

    def na_qk(r, y):
        naq_ref[0, r, :] = (y[:, :NA_W] * Q_SCALE).astype(BF16)
        nak_ref[0, r, :] = y[:, NA_W:].astype(BF16)

    def na_v_gqa_q(r, y):
        nav_ref[0, r, :] = y[:, :NA_W].astype(BF16)
        qnw = qnw_ref[...]
        for p in range(GQA_PAIRS):
            t = norm_rope(y[:, NA_W + p * LANES:NA_W + (p + 1) * LANES], qnw, r)
            gaq_ref[0, r, p * LANES:(p + 1) * LANES] = (t * Q_SCALE).astype(BF16)

    def gla_qkv(r, y):
        glq_ref[0, r, :] = y[:, :GLA_KW] * GLA_DK ** -0.5
        glk_ref[0, r, :] = y[:, GLA_KW:2 * GLA_KW]
        glv_ref[0, r, :] = y[:, 2 * GLA_KW:].astype(BF16)

    def gla_gates(r, y):
        glg_ref[0, r, :] = y[:, :GLA_VW]
        z = y[:, GLA_VW:] + zb_ref[...]
        log_sig = jnp.minimum(z, 0.0) - jnp.log1p(jnp.exp(-jnp.abs(z)))
        gla_ref[0, r, :] = log_sig * (1.0 / GLA_GATE_NORM)

    def gqa_kv(r, y):
        gak_ref[0, r, :] = norm_rope(y[:, :LANES], knw_ref[...], r).astype(BF16)
        gav_ref[0, r, :] = y[:, LANES:].astype(BF16)

    segments = [(_C_NAQ, _C_NAV, na_qk), (_C_NAV, _C_GLQ, na_v_gqa_q), (_C_GLQ, _C_GLG, gla_qkv),
                (_C_GLG, _C_GAK, gla_gates), (_C_GAK, _C_END, gqa_kv)]
    _skewed((matmul, lambda st: st[1](rows(st[0]), st[2])),
            [(sub, sg) for sub in range(INPROJ_SUBTILES) for sg in segments])


def _inproj(x, sc, sh, w, zb, qnw, knw, cos, sin, tm):
    B, N, D = x.shape
    tok = lambda width, dt: jax.ShapeDtypeStruct((B, N, width), dt)
    tspec = lambda width: pl.BlockSpec((1, tm, width), lambda b, i: (b, i, 0))
    vec = lambda width: pl.BlockSpec((1, width), lambda b, i: (0, 0))
    out_shapes = [tok(NA_W, BF16), tok(NA_W, BF16), tok(NA_W, BF16), tok(GQA_QW, BF16),
                  tok(GLA_KW, F32), tok(GLA_KW, F32), tok(GLA_VW, BF16), tok(GLA_VW, F32),
                  tok(2 * GLA_KW, F32), tok(GQA_KVW, BF16), tok(GQA_KVW, BF16)]
    return pl.pallas_call(
        _inproj_kernel,
        grid=(B, N // tm),
        in_specs=[tspec(D),
                  pl.BlockSpec((1, 1, D), lambda b, i: (b, 0, 0)),
                  pl.BlockSpec((1, 1, D), lambda b, i: (b, 0, 0)),
                  pl.BlockSpec((D, _C_END), lambda b, i: (0, 0)),
                  vec(2 * GLA_KW), vec(LANES), vec(LANES),
                  pl.BlockSpec((tm, LANES), lambda b, i: (i, 0)),
                  pl.BlockSpec((tm, LANES), lambda b, i: (i, 0))],
        out_specs=[tspec(s.shape[-1]) for s in out_shapes],
        out_shape=out_shapes,
        compiler_params=_cparams(2),
        name="inproj",
    )(x, sc, sh, w, zb, qnw, knw, cos, sin)


def _nabias_kernel(rpb_ref, o_ref):
    h = pl.program_id(0)
    qc = lax.broadcasted_iota(jnp.int32, (GRID_W, LANES), 0)
    lane = lax.broadcasted_iota(jnp.int32, (GRID_W, LANES), 1)
    kc = lane & (GRID_W - 1)
    hi = lane >= GRID_W
    idx = jnp.clip(kc - qc, -(NA_WIN_COLS - 1), NA_WIN_COLS - 1) + (NA_WIN_COLS - 1)
    start = jnp.clip(qc - NA_WIN_COLS // 2, 0, GRID_W - NA_WIN_COLS)
    col_in = jnp.logical_and(kc >= start, kc < start + NA_WIN_COLS)
    base = h * (NA_DR * NA_DC)

    pair_tables = []
    for dr0 in range(NA_DR - 1):

        def body(j, t, dr0=dr0):
            s0 = rpb_ref[base + dr0 * NA_DC + j]
            s1 = rpb_ref[base + (dr0 + 1) * NA_DC + j]
            return jnp.where(idx == j, jnp.where(hi, s1, s0), t)

        t = lax.fori_loop(0, NA_DC, body, jnp.zeros((GRID_W, LANES), F32))
        pair_tables.append(jnp.where(col_in, t * LOG2_E, NEG_BIG))
    for oi in range(NA_WIN_ROWS):
        for i in range(NA_WIN_ROWS // 2):
            o_ref[0, oi, :, i * LANES:(i + 1) * LANES] = pair_tables[oi + 2 * i]


def _na_bias(rpb_flat):
    return pl.pallas_call(
        _nabias_kernel,
        grid=(NA_HEADS,),
        in_specs=[pl.BlockSpec(memory_space=pltpu.SMEM)],
        out_specs=pl.BlockSpec((1, NA_WIN_ROWS, GRID_W, NA_WIN_ROWS * GRID_W),
                               lambda h: (h, 0, 0, 0)),
        out_shape=jax.ShapeDtypeStruct((NA_HEADS, NA_WIN_ROWS, GRID_W, NA_WIN_ROWS * GRID_W), F32),
        compiler_params=_cparams(1),
        name="na_bias",
    )(rpb_flat)


def _na_kernel(q_ref, k_ref, v_ref, kc_ref, vc_ref, b_ref, o_ref):
    n = q_ref.shape[1]
    rows = n // GRID_W
    win = NA_WIN_ROWS * GRID_W
    lane = lax.broadcasted_iota(jnp.int32, (GRID_W, LANES), 1)
    lo = lane < HEAD_DIM
    kc = kc_ref[0]
    vc = vc_ref[0]
    zero = jnp.zeros((GRID_W, LANES), BF16)

    def scores(r):
        rs = jnp.clip(r - NA_WIN_ROWS // 2, 0, rows - NA_WIN_ROWS)
        oi = rs - r + (NA_WIN_ROWS - 1)
        q0 = pl.multiple_of(r * GRID_W, GRID_W)
        k0 = pl.multiple_of(rs * GRID_W, GRID_W)
        q = q_ref[0, pl.ds(q0, GRID_W), :]
        kw = k_ref[0, pl.ds(k0, win), :]
        qs = jnp.concatenate([jnp.where(lo, q, zero), jnp.where(lo, zero, q)], axis=0)
        s_lat = _dot_nt(qs, kw) + jnp.concatenate([b_ref[0, oi], b_ref[1, oi]], axis=0)
        s_ctx = _dot_nt(qs, kc)
        return q0, k0, s_lat, s_ctx

    def softmax(st):
        q0, k0, s_lat, s_ctx = st
        m = jnp.maximum(jnp.max(s_lat, axis=-1, keepdims=True),
                        jnp.max(s_ctx, axis=-1, keepdims=True))
        p_lat = jnp.exp2(s_lat - m)
        p_ctx = jnp.exp2(s_ctx - m)
        den = jnp.sum(p_lat, axis=-1, keepdims=True) + jnp.sum(p_ctx, axis=-1, keepdims=True)
        return q0, k0, p_lat.astype(BF16), p_ctx.astype(BF16), den

    def values(st):
        q0, k0, p_lat, p_ctx, den = st
        vw = v_ref[0, pl.ds(k0, win), :]
        o = (_dot(p_lat, vw) + _dot(p_ctx, vc)) / den
        o_ref[0, pl.ds(q0, GRID_W), :] = jnp.where(lo, o[:GRID_W], o[GRID_W:]).astype(BF16)

    def body(i, carry):
        _skewed((scores, softmax, values),
                [i * NA_ROWS_PER_STEP + j for j in range(NA_ROWS_PER_STEP)])
        return carry

    lax.fori_loop(0, rows // NA_ROWS_PER_STEP, body, 0)


def _na_attention(q, k, v, kc, vc, bias):
    B, N, _ = q.shape
    L = kc.shape[1]
    lat = pl.BlockSpec((1, N, LANES), lambda b, p: (b, 0, p))
    cx = pl.BlockSpec((1, L, LANES), lambda b, p: (b, 0, p))
    return pl.pallas_call(
        _na_kernel,
        grid=(B, NA_PAIRS),
        in_specs=[lat, lat, lat, cx, cx,
                  pl.BlockSpec((2, NA_WIN_ROWS, GRID_W, NA_WIN_ROWS * GRID_W),
                               lambda b, p: (p, 0, 0, 0))],
        out_specs=lat,
        out_shape=jax.ShapeDtypeStruct((B, N, NA_W), BF16),
        compiler_params=_cparams(2),
        name="na_attn",
    )(q, k, v, kc, vc, bias)


def _pair_attn_kernel(*refs, two_sources):
    if two_sources:
        q_ref, ka_ref, va_ref, kb_ref, vb_ref, o_ref = refs
    else:
        q_ref, ka_ref, va_ref, o_ref = refs
    tq = q_ref.shape[1]
    ts = min(tq, PAIR_ATTN_SUB_ROWS)
    lane = lax.broadcasted_iota(jnp.int32, (ts, LANES), 1)
    lo = lane < HEAD_DIM
    zero = jnp.zeros((ts, LANES), BF16)

    def with_ones(v):
        return jnp.concatenate([v, jnp.ones_like(v)], axis=1)

    ka = ka_ref[0]
    va = with_ones(va_ref[0])
    if two_sources:
        kb = kb_ref[0]
        vb = with_ones(vb_ref[0])
    def scores(item):
        sub, h = item
        q = q_ref[0, sub * ts:(sub + 1) * ts, :]
        qm = jnp.where(lo, q, zero) if h == 0 else jnp.where(lo, zero, q)
        s_a = _dot_nt(qm, ka)
        s_b = _dot_nt(qm, kb) if two_sources else None
        return sub, h, s_a, s_b

    def softmax(st):
        sub, h, s_a, s_b = st
        m = jnp.max(s_a, axis=-1, keepdims=True)
        if two_sources:
            m = jnp.maximum(m, jnp.max(s_b, axis=-1, keepdims=True))
        p_a = jnp.exp2(s_a - m).astype(BF16)
        p_b = jnp.exp2(s_b - m).astype(BF16) if two_sources else None
        return sub, h, p_a, p_b

    done = {}

    def values(st):
        sub, h, p_a, p_b = st
        o = _dot(p_a, va)
        if two_sources:
            o = o + _dot(p_b, vb)
        done[sub, h] = o[:, :LANES] / o[:, LANES:]
        if h == 1:
            o_ref[0, sub * ts:(sub + 1) * ts, :] = jnp.where(
                lo, done.pop((sub, 0)), done.pop((sub, 1))).astype(BF16)

    _skewed((scores, softmax, values), [(sub, h) for sub in range(tq // ts) for h in range(2)])


def _pair_attention(q, ka, va, kb=None, vb=None, *, shared_kv, tq):
    B, Nq, W = q.shape
    pairs = W // LANES
    kv_map = (lambda b, p, i: (b, 0, 0)) if shared_kv else (lambda b, p, i: (b, 0, p))
    qspec = pl.BlockSpec((1, tq, LANES), lambda b, p, i: (b, i, p))
    in_specs = [qspec,
                pl.BlockSpec((1, ka.shape[1], LANES), kv_map),
                pl.BlockSpec((1, ka.shape[1], LANES), kv_map)]
    args = [q, ka, va]
    if kb is not None:
        in_specs += [pl.BlockSpec((1, kb.shape[1], LANES), kv_map),
                     pl.BlockSpec((1, kb.shape[1], LANES), kv_map)]
        args += [kb, vb]
    return pl.pallas_call(
        functools.partial(_pair_attn_kernel, two_sources=kb is not None),
        grid=(B, pairs, Nq // tq),
        in_specs=in_specs,
        out_specs=qspec,
        out_shape=jax.ShapeDtypeStruct((B, Nq, W), BF16),
        compiler_params=_cparams(3),
        name="pair_attn",
    )(*args)


def _gla_constants():
    c = GLA_CHUNK
    t = np.arange(c)[:, None]
    i = np.arange(c)[None, :]
    mcat = np.zeros((2, (GLA_LEVELS + 1) * c, c), np.float32)
    masks = np.zeros((2, GLA_LEVELS + 2, c, c), np.float32)
    for d in range(2):
        masks[d, GLA_LEVELS + 1] = (i <= t) if d == 0 else (i >= t)
        for l in range(1, GLA_LEVELS + 1):
            m = 2 ** l
            same = (t // m) == (i // m)
            q_side = same & ((i <= t) if d == 0 else (i >= t))
            k_side = same & ((i > t) if d == 0 else (i < t))
            if l < GLA_LEVELS:
                is_q_row = ((t // m) % 2 == 1) if d == 0 else ((t // m) % 2 == 0)
                mcat[d, (l - 1) * c:l * c] = np.where(is_q_row, q_side, k_side)
            else:
                mcat[d, (l - 1) * c:l * c] = q_side
                mcat[d, l * c:(l + 1) * c] = k_side
        masks[d, 0] = (t == i)
        for l in range(GLA_LEVELS):
            m = 2 ** l
            tb, sb = t // m, i // m
            if d == 0:
                masks[d, l + 1] = (tb % 2 == 1) & (sb == tb - 1)
            else:
                masks[d, l + 1] = (tb % 2 == 0) & (sb == tb + 1)
    masks = np.tile(masks, (1, 1, 1, GLA_HEADS))
    rows = np.arange(GLA_VW)[:, None] // GLA_DV
    cols = np.arange(GLA_KW)[None, :] // GLA_DK
    stmask = (rows == cols).astype(np.float32)
    return jnp.asarray(mcat, BF16), jnp.asarray(masks, F32), jnp.asarray(stmask, F32)


def _gla_stages(q_ref, k_ref, v_ref, la_ref, acc_ref, st_ref, mcat_ref, lmask_ref, stmask_ref, mild):
    ck = GLA_CHUNK
    whole = slice((GLA_LEVELS - 1) * ck, (GLA_LEVELS + 1) * ck)
    lane_k = lax.broadcasted_iota(jnp.int32, (1, GLA_KW), 1) // GLA_DK
    lane_v = lax.broadcasted_iota(jnp.int32, (1, GLA_VW), 1) // GLA_DV
    row = lax.broadcasted_iota(jnp.int32, (ck, GLA_KW), 0)

    def block_rows(x, lane_head):
        return jnp.concatenate([jnp.where(lane_head == h, x, 0.0) for h in range(GLA_HEADS)], axis=0)

    def exponents(item):
        c, d = item
        r0 = pl.multiple_of(c * ck, ck)
        g = la_ref[0, pl.ds(r0, ck), d * GLA_KW:(d + 1) * GLA_KW]
        g1 = g.astype(BF16)
        g2 = (g - g1.astype(F32)).astype(BF16)
        mc = mcat_ref[d, whole, :] if mild else mcat_ref[d]
        e = _dot(mc, g1) + _dot(mc, g2)
        return r0, d, g, e

    def in_chunk(st):
        r0, d, g, e = st
        q = q_ref[0, pl.ds(r0, ck), :]
        k = k_ref[0, pl.ds(r0, ck), :]
        q_chunk = e[-2 * ck:-ck]
        k_chunk = e[-ck:]
        last = ck - 1 if d == 0 else 0
        total = q_chunk[last:last + 1]
        q_in = (q * jnp.exp(q_chunk)).astype(BF16)

        def level_scores(qt, kt, idx):
            kb = block_rows(kt, lane_k).astype(BF16)
            return lmask_ref[d, idx] * _dot_nt(qt, kb)

        if mild:
            a = level_scores(q_in, k * jnp.exp(-q_chunk), GLA_LEVELS + 1)
        else:
            q_row0 = (row & 1) == (1 - d)
            a = level_scores(q.astype(BF16), k, 0)
            for l in range(GLA_LEVELS):
                dec = jnp.exp(jnp.where(q_row0, g, 0.0) if l == 0 else e[(l - 1) * ck:l * ck])
                a = a + level_scores((q * dec).astype(BF16), k * dec, l + 1)
        return r0, d, a.astype(BF16), q_in, (k * jnp.exp(k_chunk)).astype(BF16), jnp.exp(total)

    def state(st):
        r0, d, a, q_in, k_out, decay = st
        v = v_ref[0, pl.ds(r0, ck), :]
        vbd = block_rows(v.astype(F32), lane_v).astype(BF16)
        s_prev = st_ref[d]
        acc_ref[d, pl.ds(r0, ck), :] = _dot(a, vbd) + _dot_nt(q_in, s_prev.astype(BF16))
        st_ref[d] = s_prev * decay + stmask_ref[...] * _dot_tn(v, k_out)

    return exponents, in_chunk, state


def _gla_finish(acc_ref, g_ref, nw_ref, o_ref, tile):
    n = acc_ref.shape[1]
    lane = lax.broadcasted_iota(jnp.int32, (tile, LANES), 1)
    lo = lane < GLA_DV
    nw = nw_ref[...]

    def body(i, carry):
        r0 = pl.multiple_of(i * tile, tile)
        for j in range(GLA_VW // LANES):
            cols = slice(j * LANES, (j + 1) * LANES)
            o = acc_ref[0, pl.ds(r0, tile), cols] + acc_ref[1, pl.ds(r0, tile), cols]
            on = o * lax.rsqrt(_half_mean_square(o, lo) + RMS_EPS) * nw
            gate = _silu(g_ref[0, pl.ds(r0, tile), cols])
            o_ref[0, pl.ds(r0, tile), cols] = (on * gate).astype(BF16)
        return carry

    lax.fori_loop(0, n // tile, body, 0)


def _gla_kernel(q_ref, k_ref, v_ref, la_ref, g_ref, qc_ref, kc_ref, vc_ref, lac_ref, gc_ref,
                nw_ref, mcat_ref, lmask_ref, stmask_ref, o_ref, oc_ref, acc_ref, accc_ref, st_ref):
    n_chunks = q_ref.shape[1] // GLA_CHUNK
    c_chunks = qc_ref.shape[1] // GLA_CHUNK
    consts = (mcat_ref, lmask_ref, stmask_ref)
    st_ref[...] = jnp.zeros_like(st_ref)

    def min_chunk_sum(ref):
        la = ref[0]
        sums = jnp.sum(la.reshape(la.shape[0] // GLA_CHUNK, GLA_CHUNK, la.shape[1]), axis=1)
        return jnp.min(sums)

    mild = jnp.minimum(min_chunk_sum(la_ref), min_chunk_sum(lac_ref)) >= -GLA_MILD_DECAY

    def steps(refs, acc, n, is_mild):
        stages = _gla_stages(*refs, acc, st_ref, *consts, is_mild)

        per = min(GLA_CHUNKS_PER_STEP, n)

        def body(i, carry):
            items = []
            for j in range(per):
                c = i * per + j
                items += [(c, 0), (n - 1 - c, 1)]
            _skewed(stages, items)
            return carry

        lax.fori_loop(0, n // per, body, 0)

    for is_mild in (True, False):
        @pl.when(mild if is_mild else jnp.logical_not(mild))
        def _():
            steps((qc_ref, kc_ref, vc_ref, lac_ref), accc_ref, c_chunks, is_mild)
            steps((q_ref, k_ref, v_ref, la_ref), acc_ref, n_chunks, is_mild)

    _gla_finish(acc_ref, g_ref, nw_ref, o_ref, 256)
    _gla_finish(accc_ref, gc_ref, nw_ref, oc_ref, 256)


def _gla(q, k, v, la, g, qc, kc, vc, lac, gc, nw, consts):
    B, N, _ = q.shape
    L = qc.shape[1]
    mcat, lmask, stmask = consts
    tok = lambda n, w: pl.BlockSpec((1, n, w), lambda b: (b, 0, 0))
    full = lambda a: pl.BlockSpec(a.shape, lambda b: (0,) * a.ndim)
    return pl.pallas_call(
        _gla_kernel,
        grid=(B,),
        in_specs=[tok(N, GLA_KW), tok(N, GLA_KW), tok(N, GLA_VW), tok(N, 2 * GLA_KW), tok(N, GLA_VW),
                  tok(L, GLA_KW), tok(L, GLA_KW), tok(L, GLA_VW), tok(L, 2 * GLA_KW), tok(L, GLA_VW),
                  full(nw), full(mcat), full(lmask), full(stmask)],
        out_specs=[tok(N, GLA_VW), tok(L, GLA_VW)],
        out_shape=[jax.ShapeDtypeStruct((B, N, GLA_VW), BF16), jax.ShapeDtypeStruct((B, L, GLA_VW), BF16)],
        scratch_shapes=[pltpu.VMEM((2, N, GLA_VW), F32), pltpu.VMEM((2, L, GLA_VW), F32),
                        pltpu.VMEM((2, GLA_VW, GLA_KW), F32)],
        compiler_params=_cparams(1),
        name="gla",
    )(q, k, v, la, g, qc, kc, vc, lac, gc, nw, mcat, lmask, stmask)


_FFN_CHUNKS = ((0, 768), (768, 1536), (1536, 2304), (2304, FFN_HIDDEN))


def _mix_ffn_kernel(x_ref, na_ref, gl_ref, ga_ref, wm_ref, g1_ref, l1g_ref, l1b_ref,
                    sc_ref, sh_ref, g2_ref, wi_ref, wo_ref, l2g_ref, l2b_ref, o_ref):
    ts = x_ref.shape[1] // FFN_SUBTILES
    x1_of, u_of, acc_of = {}, {}, {}

    def rows(sub):
        return slice(sub * ts, (sub + 1) * ts)

    def mixer_sublayer(sub):
        r = rows(sub)
        o = jnp.concatenate([na_ref[0, r, :], gl_ref[0, r, :], ga_ref[0, r, :]], axis=-1)
        y = DEEPNORM_ALPHA * x_ref[0, r, :] + g1_ref[0] * _dot(o, wm_ref[...])
        x1 = _layer_norm(y, l1g_ref[...], l1b_ref[...])
        x1_of[sub] = x1
        u_of[sub] = (x1 * (1.0 + sc_ref[0]) + sh_ref[0]).astype(BF16)

    def up(item):
        sub, (a0, a1) = item
        if sub not in u_of:
            mixer_sublayer(sub)
        u = u_of[sub]
        return item, _dot(u, wi_ref[:, a0:a1]), _dot(u, wi_ref[:, FFN_HIDDEN + a0:FFN_HIDDEN + a1])

    def gate(st):
        item, ha, hb = st
        return item, (_silu(ha) * hb).astype(BF16)

    def down(st):
        (sub, (a0, a1)), t = st
        part = _dot(t, wo_ref[a0:a1, :])
        acc_of[sub] = acc_of[sub] + part if sub in acc_of else part
        if a1 == FFN_HIDDEN:
            y = DEEPNORM_ALPHA * x1_of.pop(sub) + g2_ref[0] * acc_of.pop(sub)
            o_ref[0, rows(sub), :] = _layer_norm(y, l2g_ref[...], l2b_ref[...])

    _skewed((up, gate, down), [(sub, ch) for sub in range(FFN_SUBTILES) for ch in _FFN_CHUNKS])


def _mix_ffn(x, o_na, o_gl, o_ga, wm, g1, l1g, l1b, sc, sh, g2, wi, wo, l2g, l2b, tm):
    B, N, D = x.shape
    tspec = lambda width: pl.BlockSpec((1, tm, width), lambda b, i: (b, i, 0))
    mod = pl.BlockSpec((1, 1, D), lambda b, i: (b, 0, 0))
    vec = pl.BlockSpec((1, D), lambda b, i: (0, 0))
    resident = lambda a: pl.BlockSpec(a.shape, lambda b, i: (0, 0), pipeline_mode=pl.Buffered(1))
    return pl.pallas_call(
        _mix_ffn_kernel,
        grid=(B, N // tm),
        in_specs=[tspec(D), tspec(NA_W), tspec(GLA_VW), tspec(GQA_QW), resident(wm), mod, vec, vec,
                  mod, mod, mod, resident(wi), resident(wo), vec, vec],
        out_specs=tspec(D),
        out_shape=jax.ShapeDtypeStruct((B, N, D), F32),
        compiler_params=_cparams(2),
        name="mix_ffn",
    )(x, o_na, o_gl, o_ga, wm, g1, l1g, l1b, sc, sh, g2, wi, wo, l2g, l2b)


def _rope_tables(n):
    t = jnp.arange(n)
    row = (t // GRID_W).astype(F32)
    col = (t % GRID_W).astype(F32)
    inv_freq = ROPE_THETA ** (-jnp.arange(ROPE_AXIS_PAIRS, dtype=F32) / ROPE_AXIS_PAIRS)
    ang_r = row[:, None] * inv_freq
    ang_c = col[:, None] * inv_freq
    ang = jnp.concatenate([ang_r, ang_r, ang_c, ang_c], axis=-1)
    sign = jnp.where((jnp.arange(HEAD_DIM) % 32) < 16, -1.0, 1.0).astype(F32)
    cos = jnp.tile(jnp.cos(ang), (1, 2))
    sin = jnp.tile(jnp.sin(ang) * sign, (1, 2))
    return cos, sin


def _pair_major(w, axis):
    shape = w.shape
    lead, tail = shape[:axis], shape[axis + 1:]
    w = w.reshape(lead + (GQA_KV_HEADS, GQA_REP, HEAD_DIM) + tail)
    w = jnp.swapaxes(w, axis, axis + 1)
    return w.reshape(shape)


def kernel(x, c, ctx, c_ctx, w_ada, b_ada, w_in, na_rpb, gla_wa2, gla_ba, gla_norm_w, gqa_qnorm_w,
           gqa_knorm_w, w_out, ln1_g, ln1_b, w_ffn_in, w_ffn_out, ln2_g, ln2_b):
    B, N, D = x.shape
    L = ctx.shape[1]
    depth = w_in.shape[0]
    tm = 1024
    tmc = min(L, 1024)

    pad = (-(B + 1)) % 8
    cvec = jnp.concatenate([c, c_ctx[None, :], jnp.zeros((pad, D), F32)], axis=0)
    mods = _ada(cvec, w_ada, b_ada)

    w_lr = w_in[:, :, _O_GLLR:_O_GAQ].reshape(depth, D, 2, GLA_RANK).transpose(0, 2, 1, 3)
    w_z = _fold_gate_weights(w_lr, gla_wa2)

    cos, sin = _rope_tables(N)
    cos_c = jnp.ones((L, LANES), F32)
    sin_c = jnp.zeros((L, LANES), F32)
    gla_consts = _gla_constants()

    xc = ctx
    for l in range(depth):
        ctx_out = l < depth - 1
        m_lat = mods[l, :B].reshape(B, 6, 1, D)
        m_ctx = jnp.broadcast_to(mods[l, B].reshape(1, 6, 1, D), (B, 6, 1, D))
        sh1, sc1, g1, sh2, sc2, g2 = (m_lat[:, i] for i in range(6))
        sh1c, sc1c, g1c, sh2c, sc2c, g2c = (m_ctx[:, i] for i in range(6))

        wl = w_in[l]
        w_proj = jnp.concatenate(
            [wl[:, _O_NAQ:_O_GLQ], _pair_major(wl[:, _O_GAQ:_O_GAK], 1), wl[:, _O_GLQ:_O_GLLR],
             w_z[l], wl[:, _O_GAK:_O_END]], axis=1).astype(BF16)
        zb = gla_ba[l].reshape(1, 2 * GLA_KW)
        qnw = jnp.tile(gqa_qnorm_w[l], 2).reshape(1, LANES)
        knw = jnp.tile(gqa_knorm_w[l], 2).reshape(1, LANES)
        glnw = jnp.tile(gla_norm_w[l], 2).reshape(1, LANES)
        wo_l = w_out[l]
        w_o = jnp.concatenate([wo_l[:NA_W + GLA_VW], _pair_major(wo_l[NA_W + GLA_VW:], 0)], axis=0).astype(BF16)
        lg1, lb1 = ln1_g[l].reshape(1, D), ln1_b[l].reshape(1, D)
        lg2, lb2 = ln2_g[l].reshape(1, D), ln2_b[l].reshape(1, D)
        wi = w_ffn_in[l].astype(BF16)
        wo = w_ffn_out[l].astype(BF16)

        (na_q, na_k, na_v, ga_q, gl_q, gl_k, gl_v, gl_g, gl_la, ga_k, ga_v) = _inproj(
            x, sc1, sh1, w_proj, zb, qnw, knw, cos, sin, tm)
        (na_qc, na_kc, na_vc, ga_qc, gl_qc, gl_kc, gl_vc, gl_gc, gl_lac, ga_kc, ga_vc) = _inproj(
            xc, sc1c, sh1c, w_proj, zb, qnw, knw, cos_c, sin_c, tmc)

        bias = _na_bias(na_rpb[l].reshape(-1))
        o_na = _na_attention(na_q, na_k, na_v, na_kc, na_vc, bias)
        o_gl, oc_gl = _gla(gl_q, gl_k, gl_v, gl_la, gl_g, gl_qc, gl_kc, gl_vc, gl_lac, gl_gc, glnw, gla_consts)
        o_ga = _pair_attention(ga_q, ga_k, ga_v, ga_kc, ga_vc, shared_kv=True, tq=min(N, 2048))

        x = _mix_ffn(x, o_na, o_gl, o_ga, w_o, g1, lg1, lb1, sc2, sh2, g2, wi, wo, lg2, lb2, tm)

        if ctx_out:
            oc_na = _pair_attention(na_qc, na_kc, na_vc, shared_kv=False, tq=L)
            oc_ga = _pair_attention(ga_qc, ga_kc, ga_vc, shared_kv=True, tq=L)
            xc = _mix_ffn(xc, oc_na, oc_gl, oc_ga, w_o, g1c, lg1, lb1, sc2c, sh2c, g2c, wi, wo, lg2, lb2, tmc)
    return x
```

```python
import functools

import numpy as np
import jax
import jax.numpy as jnp
from jax import lax
from jax.experimental import pallas as pl
from jax.experimental.pallas import tpu as pltpu

F32 = jnp.float32
BF16 = jnp.bfloat16

D_MODEL = 1024
DEPTH = 2
GRID_W = 64
HEAD_DIM = 64
LANES = 128

NA_HEADS = 6
NA_WIN_ROWS = 8
NA_WIN_COLS = 16
NA_W = NA_HEADS * HEAD_DIM
NA_PAIRS = NA_W // LANES
NA_DR = 2 * NA_WIN_ROWS - 1
NA_DC = 2 * NA_WIN_COLS - 1
NA_ROWS_PER_STEP = 16

GLA_HEADS = 4
GLA_DK = 32
GLA_DV = 64
GLA_RANK = 16
GLA_GATE_NORM = 16.0
GLA_CHUNK = 64
GLA_KW = GLA_HEADS * GLA_DK
GLA_VW = GLA_HEADS * GLA_DV
GLA_LEVELS = 6
GLA_MILD_DECAY = 60.0
GLA_CHUNKS_PER_STEP = 8

GQA_Q_HEADS = 6
GQA_KV_HEADS = 2
GQA_REP = GQA_Q_HEADS // GQA_KV_HEADS
GQA_QW = GQA_Q_HEADS * HEAD_DIM
GQA_KVW = GQA_KV_HEADS * HEAD_DIM
GQA_PAIRS = GQA_QW // LANES
INPROJ_SUBTILES = 4
FFN_SUBTILES = 2
PAIR_ATTN_SUB_ROWS = 128
ROPE_THETA = 10000.0
ROPE_AXIS_PAIRS = HEAD_DIM // 4

FFN_HIDDEN = 2816
MIX_W = NA_W + GLA_VW + GQA_QW

DEEPNORM_ALPHA = (2.0 * DEPTH) ** 0.25
LN_EPS = 1e-5
RMS_EPS = 1e-6
NEG_BIG = -1e30
LOG2_E = 1.4426950408889634
Q_SCALE = HEAD_DIM ** -0.5 * LOG2_E

_O_NAQ, _O_NAK, _O_NAV = 0, 384, 768
_O_GLQ, _O_GLK, _O_GLV, _O_GLG, _O_GLLR = 1152, 1280, 1408, 1664, 1920
_O_GAQ, _O_GAK, _O_GAV, _O_END = 1952, 2336, 2464, 2592
_C_NAQ, _C_NAK, _C_NAV, _C_GAQ = 0, 384, 768, 1152
_C_GLQ, _C_GLK, _C_GLV, _C_GLG, _C_Z = 1536, 1664, 1792, 2048, 2304
_C_GAK, _C_GAV, _C_END = 2560, 2688, 2816

VMEM_LIMIT = 48 * 1024 * 1024


def _cparams(n_axes):
    return pltpu.CompilerParams(dimension_semantics=("arbitrary",) * n_axes,
                                vmem_limit_bytes=VMEM_LIMIT)


def _dot(a, b):
    return jnp.dot(a, b, preferred_element_type=F32)


def _dot_nt(a, b):
    return lax.dot_general(a, b, (((1,), (1,)), ((), ())), preferred_element_type=F32)


def _dot_tn(a, b):
    return lax.dot_general(a, b, (((0,), (0,)), ((), ())), preferred_element_type=F32)


def _silu(x):
    return x * jax.nn.sigmoid(x)


def _layer_norm(y, g, b):
    mu = jnp.mean(y, axis=-1, keepdims=True)
    d = y - mu
    var = jnp.mean(d * d, axis=-1, keepdims=True)
    return d * lax.rsqrt(var + LN_EPS) * g + b


def _skewed(stages, items):
    n, k = len(items), len(stages)
    live = {}
    for j in range(n + k - 1):
        for s in range(k):
            idx = j - s
            if 0 <= idx < n:
                live[s, idx] = stages[s](items[idx] if s == 0 else live.pop((s - 1, idx)))


def _half_mean_square(y, lo):
    s = y * y
    s_lo = jnp.sum(jnp.where(lo, s, 0.0), axis=-1, keepdims=True)
    s_hi = jnp.sum(jnp.where(lo, 0.0, s), axis=-1, keepdims=True)
    return jnp.where(lo, s_lo, s_hi) * (1.0 / HEAD_DIM)


_ADA_TN = 1024


def _ada_kernel(c_ref, w_ref, b_ref, o_ref):
    s = _silu(c_ref[...])
    o_ref[0] = jnp.dot(s, w_ref[0], precision=lax.Precision.HIGHEST,
                       preferred_element_type=F32) + b_ref[0]


def _ada(cvec, w_ada, b_ada):
    rows = cvec.shape[0]
    depth, d, n6 = w_ada.shape
    return pl.pallas_call(
        _ada_kernel,
        grid=(depth, n6 // _ADA_TN),
        in_specs=[pl.BlockSpec((rows, d), lambda l, j: (0, 0)),
                  pl.BlockSpec((1, d, _ADA_TN), lambda l, j: (l, 0, j)),
                  pl.BlockSpec((1, 1, _ADA_TN), lambda l, j: (l, 0, j))],
        out_specs=pl.BlockSpec((1, rows, _ADA_TN), lambda l, j: (l, 0, j)),
        out_shape=jax.ShapeDtypeStruct((depth, rows, n6), F32),
        compiler_params=_cparams(2),
        name="ada",
    )(cvec, w_ada, b_ada.reshape(depth, 1, n6))


def _fold_kernel(wlr_ref, wa2_ref, o_ref):
    for e in range(2):
        o_ref[0, :, e * GLA_KW:(e + 1) * GLA_KW] = jnp.dot(
            wlr_ref[0, e], wa2_ref[0, e], precision=lax.Precision.HIGHEST,
            preferred_element_type=F32)


def _fold_gate_weights(w_lr, wa2):
    depth = w_lr.shape[0]
    return pl.pallas_call(
        _fold_kernel,
        grid=(depth,),
        in_specs=[pl.BlockSpec((1, 2, D_MODEL, GLA_RANK), lambda l: (l, 0, 0, 0)),
                  pl.BlockSpec((1, 2, GLA_RANK, GLA_KW), lambda l: (l, 0, 0, 0))],
        out_specs=pl.BlockSpec((1, D_MODEL, 2 * GLA_KW), lambda l: (l, 0, 0)),
        out_shape=jax.ShapeDtypeStruct((depth, D_MODEL, 2 * GLA_KW), F32),
        compiler_params=_cparams(1),
        name="fold_gate",
    )(w_lr, wa2)


def _inproj_kernel(x_ref, sc_ref, sh_ref, w_ref, zb_ref, qnw_ref, knw_ref, cos_ref, sin_ref,
                   naq_ref, nak_ref, nav_ref, gaq_ref, glq_ref, glk_ref, glv_ref, glg_ref,
                   gla_ref, gak_ref, gav_ref):
    ts = x_ref.shape[1] // INPROJ_SUBTILES
    lane = lax.broadcasted_iota(jnp.int32, (ts, LANES), 1)
    lo = lane < HEAD_DIM
    first16 = (lane & 31) < 16
    u_of = {}

    def rows(sub):
        return slice(sub * ts, (sub + 1) * ts)

    def matmul(item):
        sub, (a, b, epilogue) = item
        if sub not in u_of:
            u_of[sub] = (x_ref[0, rows(sub), :] * (1.0 + sc_ref[0]) + sh_ref[0]).astype(BF16)
        return sub, epilogue, _dot(u_of[sub], w_ref[:, a:b])

    def norm_rope(y, w, r):
        yn = y * lax.rsqrt(_half_mean_square(y, lo) + RMS_EPS) * w
        rot = jnp.where(first16, pltpu.roll(yn, LANES - 16, 1), pltpu.roll(yn, 16, 1))
        return yn * cos_ref[r, :] + rot * sin_ref[r, :]

    def na_qk(r, y):
        naq_ref[0, r, :] = (y[:, :NA_W] * Q_SCALE).astype(BF16)
        nak_ref[0, r, :] = y[:, NA_W:].astype(BF16)

    def na_v_gqa_q(r, y):
        nav_ref[0, r, :] = y[:, :NA_W].astype(BF16)
        qnw = qnw_ref[...]
        for p in range(GQA_PAIRS):
            t = norm_rope(y[:, NA_W + p * LANES:NA_W + (p + 1) * LANES], qnw, r)
            gaq_ref[0, r, p * LANES:(p + 1) * LANES] = (t * Q_SCALE).astype(BF16)

    def gla_qkv(r, y):
        glq_ref[0, r, :] = y[:, :GLA_KW] * GLA_DK ** -0.5
        glk_ref[0, r, :] = y[:, GLA_KW:2 * GLA_KW]
        glv_ref[0, r, :] = y[:, 2 * GLA_KW:].astype(BF16)

    def gla_gates(r, y):
        glg_ref[0, r, :] = y[:, :GLA_VW]
        z = y[:, GLA_VW:] + zb_ref[...]
        log_sig = jnp.minimum(z, 0.0) - jnp.log1p(jnp.exp(-jnp.abs(z)))
        gla_ref[0, r, :] = log_sig * (1.0 / GLA_GATE_NORM)

    def gqa_kv(r, y):
        gak_ref[0, r, :] = norm_rope(y[:, :LANES], knw_ref[...], r).astype(BF16)
        gav_ref[0, r, :] = y[:, LANES:].astype(BF16)

    segments = [(_C_NAQ, _C_NAV, na_qk), (_C_NAV, _C_GLQ, na_v_gqa_q), (_C_GLQ, _C_GLG, gla_qkv),
                (_C_GLG, _C_GAK, gla_gates), (_C_GAK, _C_END, gqa_kv)]
    _skewed((matmul, lambda st: st[1](rows(st[0]), st[2])),
            [(sub, sg) for sub in range(INPROJ_SUBTILES) for sg in segments])


def _inproj(x, sc, sh, w, zb, qnw, knw, cos, sin, tm):
    B, N, D = x.shape
    tok = lambda width, dt: jax.ShapeDtypeStruct((B, N, width), dt)
    tspec = lambda width: pl.BlockSpec((1, tm, width), lambda b, i: (b, i, 0))
    vec = lambda width: pl.BlockSpec((1, width), lambda b, i: (0, 0))
    out_shapes = [tok(NA_W, BF16), tok(NA_W, BF16), tok(NA_W, BF16), tok(GQA_QW, BF16),
                  tok(GLA_KW, F32), tok(GLA_KW, F32), tok(GLA_VW, BF16), tok(GLA_VW, F32),
                  tok(2 * GLA_KW, F32), tok(GQA_KVW, BF16), tok(GQA_KVW, BF16)]
    return pl.pallas_call(
        _inproj_kernel,
        grid=(B, N // tm),
        in_specs=[tspec(D),
                  pl.BlockSpec((1, 1, D), lambda b, i: (b, 0, 0)),
                  pl.BlockSpec((1, 1, D), lambda b, i: (b, 0, 0)),
                  pl.BlockSpec((D, _C_END), lambda b, i: (0, 0)),
                  vec(2 * GLA_KW), vec(LANES), vec(LANES),
                  pl.BlockSpec((tm, LANES), lambda b, i: (i, 0)),
                  pl.BlockSpec((tm, LANES), lambda b, i: (i, 0))],
        out_specs=[tspec(s.shape[-1]) for s in out_shapes],
        out_shape=out_shapes,
        compiler_params=_cparams(2),
        name="inproj",
    )(x, sc, sh, w, zb, qnw, knw, cos, sin)


def _nabias_kernel(rpb_ref, o_ref):
    h = pl.program_id(0)
    qc = lax.broadcasted_iota(jnp.int32, (GRID_W, LANES), 0)
    lane = lax.broadcasted_iota(jnp.int32, (GRID_W, LANES), 1)
    kc = lane & (GRID_W - 1)
    hi = lane >= GRID_W
    idx = jnp.clip(kc - qc, -(NA_WIN_COLS - 1), NA_WIN_COLS - 1) + (NA_WIN_COLS - 1)
    start = jnp.clip(qc - NA_WIN_COLS // 2, 0, GRID_W - NA_WIN_COLS)
    col_in = jnp.logical_and(kc >= start, kc < start + NA_WIN_COLS)
    base = h * (NA_DR * NA_DC)

    pair_tables = []
    for dr0 in range(NA_DR - 1):

        def body(j, t, dr0=dr0):
            s0 = rpb_ref[base + dr0 * NA_DC + j]
            s1 = rpb_ref[base + (dr0 + 1) * NA_DC + j]
            return jnp.where(idx == j, jnp.where(hi, s1, s0), t)

        t = lax.fori_loop(0, NA_DC, body, jnp.zeros((GRID_W, LANES), F32), unroll=True)
        pair_tables.append(jnp.where(col_in, t * LOG2_E, NEG_BIG))
    for oi in range(NA_WIN_ROWS):
        for i in range(NA_WIN_ROWS // 2):
            o_ref[0, oi, :, i * LANES:(i + 1) * LANES] = pair_tables[oi + 2 * i]


def _na_bias(rpb_flat):
    return pl.pallas_call(
        _nabias_kernel,
        grid=(NA_HEADS,),
        in_specs=[pl.BlockSpec(memory_space=pltpu.SMEM)],
        out_specs=pl.BlockSpec((1, NA_WIN_ROWS, GRID_W, NA_WIN_ROWS * GRID_W),
                               lambda h: (h, 0, 0, 0)),
        out_shape=jax.ShapeDtypeStruct((NA_HEADS, NA_WIN_ROWS, GRID_W, NA_WIN_ROWS * GRID_W), F32),
        compiler_params=_cparams(1),
        name="na_bias",
    )(rpb_flat)


def _na_kernel(q_ref, k_ref, v_ref, kc_ref, vc_ref, b_ref, o_ref):
    n = q_ref.shape[1]
    rows = n // GRID_W
    win = NA_WIN_ROWS * GRID_W
    lane = lax.broadcasted_iota(jnp.int32, (GRID_W, LANES), 1)
    lo = lane < HEAD_DIM
    kc = kc_ref[0]
    vc = vc_ref[0]
    zero = jnp.zeros((GRID_W, LANES), BF16)

    def scores(r):
        rs = jnp.clip(r - NA_WIN_ROWS // 2, 0, rows - NA_WIN_ROWS)
        oi = rs - r + (NA_WIN_ROWS - 1)
        q0 = pl.multiple_of(r * GRID_W, GRID_W)
        k0 = pl.multiple_of(rs * GRID_W, GRID_W)
        q = q_ref[0, pl.ds(q0, GRID_W), :]
        kw = k_ref[0, pl.ds(k0, win), :]
        qs = jnp.concatenate([jnp.where(lo, q, zero), jnp.where(lo, zero, q)], axis=0)
        s_lat = _dot_nt(qs, kw) + jnp.concatenate([b_ref[0, oi], b_ref[1, oi]], axis=0)
        s_ctx = _dot_nt(qs, kc)
        return q0, k0, s_lat, s_ctx

    def softmax(st):
        q0, k0, s_lat, s_ctx = st
        m = jnp.maximum(jnp.max(s_lat, axis=-1, keepdims=True),
                        jnp.max(s_ctx, axis=-1, keepdims=True))
        p_lat = jnp.exp2(s_lat - m)
        p_ctx = jnp.exp2(s_ctx - m)
        den = jnp.sum(p_lat, axis=-1, keepdims=True) + jnp.sum(p_ctx, axis=-1, keepdims=True)
        return q0, k0, p_lat.astype(BF16), p_ctx.astype(BF16), den

    def values(st):
        q0, k0, p_lat, p_ctx, den = st
        vw = v_ref[0, pl.ds(k0, win), :]
        o = (_dot(p_lat, vw) + _dot(p_ctx, vc)) / den
        o_ref[0, pl.ds(q0, GRID_W), :] = jnp.where(lo, o[:GRID_W], o[GRID_W:]).astype(BF16)

    def body(i, carry):
        _skewed((scores, softmax, values),
                [i * NA_ROWS_PER_STEP + j for j in range(NA_ROWS_PER_STEP)])
        return carry

    lax.fori_loop(0, rows // NA_ROWS_PER_STEP, body, 0)


def _na_attention(q, k, v, kc, vc, bias):
    B, N, _ = q.shape
    L = kc.shape[1]
    lat = pl.BlockSpec((1, N, LANES), lambda b, p: (b, 0, p))
    cx = pl.BlockSpec((1, L, LANES), lambda b, p: (b, 0, p))
    return pl.pallas_call(
        _na_kernel,
        grid=(B, NA_PAIRS),
        in_specs=[lat, lat, lat, cx, cx,
                  pl.BlockSpec((2, NA_WIN_ROWS, GRID_W, NA_WIN_ROWS * GRID_W),
                               lambda b, p: (p, 0, 0, 0))],
        out_specs=lat,
        out_shape=jax.ShapeDtypeStruct((B, N, NA_W), BF16),
        compiler_params=_cparams(2),
        name="na_attn",
    )(q, k, v, kc, vc, bias)


def _pair_attn_kernel(*refs, two_sources):
    if two_sources:
        q_ref, ka_ref, va_ref, kb_ref, vb_ref, o_ref = refs
    else:
        q_ref, ka_ref, va_ref, o_ref = refs
    tq = q_ref.shape[1]
    ts = min(tq, PAIR_ATTN_SUB_ROWS)
    lane = lax.broadcasted_iota(jnp.int32, (ts, LANES), 1)
    lo = lane < HEAD_DIM
    zero = jnp.zeros((ts, LANES), BF16)

    def with_ones(v):
        return jnp.concatenate([v, jnp.ones_like(v)], axis=1)

    ka = ka_ref[0]
    va = with_ones(va_ref[0])
    if two_sources:
        kb = kb_ref[0]
        vb = with_ones(vb_ref[0])
    def scores(item):
        sub, h = item
        q = q_ref[0, sub * ts:(sub + 1) * ts, :]
        qm = jnp.where(lo, q, zero) if h == 0 else jnp.where(lo, zero, q)
        s_a = _dot_nt(qm, ka)
        s_b = _dot_nt(qm, kb) if two_sources else None
        return sub, h, s_a, s_b

    def softmax(st):
        sub, h, s_a, s_b = st
        m = jnp.max(s_a, axis=-1, keepdims=True)
        if two_sources:
            m = jnp.maximum(m, jnp.max(s_b, axis=-1, keepdims=True))
        p_a = jnp.exp2(s_a - m).astype(BF16)
        p_b = jnp.exp2(s_b - m).astype(BF16) if two_sources else None
        return sub, h, p_a, p_b

    done = {}

    def values(st):
        sub, h, p_a, p_b = st
        o = _dot(p_a, va)
        if two_sources:
            o = o + _dot(p_b, vb)
        done[sub, h] = o[:, :LANES] / o[:, LANES:]
        if h == 1:
            o_ref[0, sub * ts:(sub + 1) * ts, :] = jnp.where(
                lo, done.pop((sub, 0)), done.pop((sub, 1))).astype(BF16)

    _skewed((scores, softmax, values), [(sub, h) for sub in range(tq // ts) for h in range(2)])


def _pair_attention(q, ka, va, kb=None, vb=None, *, shared_kv, tq):
    B, Nq, W = q.shape
    pairs = W // LANES
    kv_map = (lambda b, p, i: (b, 0, 0)) if shared_kv else (lambda b, p, i: (b, 0, p))
    qspec = pl.BlockSpec((1, tq, LANES), lambda b, p, i: (b, i, p))
    in_specs = [qspec,
                pl.BlockSpec((1, ka.shape[1], LANES), kv_map),
                pl.BlockSpec((1, ka.shape[1], LANES), kv_map)]
    args = [q, ka, va]
    if kb is not None:
        in_specs += [pl.BlockSpec((1, kb.shape[1], LANES), kv_map),
                     pl.BlockSpec((1, kb.shape[1], LANES), kv_map)]
        args += [kb, vb]
    return pl.pallas_call(
        functools.partial(_pair_attn_kernel, two_sources=kb is not None),
        grid=(B, pairs, Nq // tq),
        in_specs=in_specs,
        out_specs=qspec,
        out_shape=jax.ShapeDtypeStruct((B, Nq, W), BF16),
        compiler_params=_cparams(3),
        name="pair_attn",
    )(*args)


def _gla_constants():
    c = GLA_CHUNK
    t = np.arange(c)[:, None]
    i = np.arange(c)[None, :]
    mcat = np.zeros((2, (GLA_LEVELS + 1) * c, c), np.float32)
    masks = np.zeros((2, GLA_LEVELS + 2, c, c), np.float32)
    for d in range(2):
        masks[d, GLA_LEVELS + 1] = (i <= t) if d == 0 else (i >= t)
        for l in range(1, GLA_LEVELS + 1):
            m = 2 ** l
            same = (t // m) == (i // m)
            q_side = same & ((i <= t) if d == 0 else (i >= t))
            k_side = same & ((i > t) if d == 0 else (i < t))
            if l < GLA_LEVELS:
                is_q_row = ((t // m) % 2 == 1) if d == 0 else ((t // m) % 2 == 0)
                mcat[d, (l - 1) * c:l * c] = np.where(is_q_row, q_side, k_side)
            else:
                mcat[d, (l - 1) * c:l * c] = q_side
                mcat[d, l * c:(l + 1) * c] = k_side
        masks[d, 0] = (t == i)
        for l in range(GLA_LEVELS):
            m = 2 ** l
            tb, sb = t // m, i // m
            if d == 0:
                masks[d, l + 1] = (tb % 2 == 1) & (sb == tb - 1)
            else:
                masks[d, l + 1] = (tb % 2 == 0) & (sb == tb + 1)
    masks = np.tile(masks, (1, 1, 1, GLA_HEADS))
    rows = np.arange(GLA_VW)[:, None] // GLA_DV
    cols = np.arange(GLA_KW)[None, :] // GLA_DK
    stmask = (rows == cols).astype(np.float32)
    return jnp.asarray(mcat, BF16), jnp.asarray(masks, F32), jnp.asarray(stmask, F32)


def _gla_stages(q_ref, k_ref, v_ref, la_ref, acc_ref, st_ref, mcat_ref, lmask_ref, stmask_ref, mild):
    ck = GLA_CHUNK
    whole = slice((GLA_LEVELS - 1) * ck, (GLA_LEVELS + 1) * ck)
    lane_k = lax.broadcasted_iota(jnp.int32, (1, GLA_KW), 1) // GLA_DK
    lane_v = lax.broadcasted_iota(jnp.int32, (1, GLA_VW), 1) // GLA_DV
    row = lax.broadcasted_iota(jnp.int32, (ck, GLA_KW), 0)

    def block_rows(x, lane_head):
        return jnp.concatenate([jnp.where(lane_head == h, x, 0.0) for h in range(GLA_HEADS)], axis=0)

    def exponents(item):
        c, d = item
        r0 = pl.multiple_of(c * ck, ck)
        g = la_ref[0, pl.ds(r0, ck), d * GLA_KW:(d + 1) * GLA_KW]
        g1 = g.astype(BF16)
        g2 = (g - g1.astype(F32)).astype(BF16)
        mc = mcat_ref[d, whole, :] if mild else mcat_ref[d]
        e = _dot(mc, g1) + _dot(mc, g2)
        return r0, d, g, e

    def in_chunk(st):
        r0, d, g, e = st
        q = q_ref[0, pl.ds(r0, ck), :]
        k = k_ref[0, pl.ds(r0, ck), :]
        q_chunk = e[-2 * ck:-ck]
        k_chunk = e[-ck:]
        last = ck - 1 if d == 0 else 0
        total = q_chunk[last:last + 1]
        q_in = (q * jnp.exp(q_chunk)).astype(BF16)

        def level_scores(qt, kt, idx):
            kb = block_rows(kt, lane_k).astype(BF16)
            return lmask_ref[d, idx] * _dot_nt(qt, kb)

        if mild:
            a = level_scores(q_in, k * jnp.exp(-q_chunk), GLA_LEVELS + 1)
        else:
            q_row0 = (row & 1) == (1 - d)
            a = level_scores(q.astype(BF16), k, 0)
            for l in range(GLA_LEVELS):
                dec = jnp.exp(jnp.where(q_row0, g, 0.0) if l == 0 else e[(l - 1) * ck:l * ck])
                a = a + level_scores((q * dec).astype(BF16), k * dec, l + 1)
        return r0, d, a.astype(BF16), q_in, (k * jnp.exp(k_chunk)).astype(BF16), jnp.exp(total)

    def state(st):
        r0, d, a, q_in, k_out, decay = st
        v = v_ref[0, pl.ds(r0, ck), :]
        vbd = block_rows(v.astype(F32), lane_v).astype(BF16)
        s_prev = st_ref[d]
        acc_ref[d, pl.ds(r0, ck), :] = _dot(a, vbd) + _dot_nt(q_in, s_prev.astype(BF16))
        st_ref[d] = s_prev * decay + stmask_ref[...] * _dot_tn(v, k_out)

    return exponents, in_chunk, state


def _gla_finish(acc_ref, g_ref, nw_ref, o_ref, tile):
    n = acc_ref.shape[1]
    lane = lax.broadcasted_iota(jnp.int32, (tile, LANES), 1)
    lo = lane < GLA_DV
    nw = nw_ref[...]

    def body(i, carry):
        r0 = pl.multiple_of(i * tile, tile)
        for j in range(GLA_VW // LANES):
            cols = slice(j * LANES, (j + 1) * LANES)
            o = acc_ref[0, pl.ds(r0, tile), cols] + acc_ref[1, pl.ds(r0, tile), cols]
            on = o * lax.rsqrt(_half_mean_square(o, lo) + RMS_EPS) * nw
            gate = _silu(g_ref[0, pl.ds(r0, tile), cols])
            o_ref[0, pl.ds(r0, tile), cols] = (on * gate).astype(BF16)
        return carry

    lax.fori_loop(0, n // tile, body, 0)


def _gla_kernel(q_ref, k_ref, v_ref, la_ref, g_ref, qc_ref, kc_ref, vc_ref, lac_ref, gc_ref,
                nw_ref, mcat_ref, lmask_ref, stmask_ref, o_ref, oc_ref, acc_ref, accc_ref, st_ref):
    n_chunks = q_ref.shape[1] // GLA_CHUNK
    c_chunks = qc_ref.shape[1] // GLA_CHUNK
    consts = (mcat_ref, lmask_ref, stmask_ref)
    st_ref[...] = jnp.zeros_like(st_ref)

    def min_chunk_sum(ref):
        la = ref[0]
        sums = jnp.sum(la.reshape(la.shape[0] // GLA_CHUNK, GLA_CHUNK, la.shape[1]), axis=1)
        return jnp.min(sums)

    mild = jnp.minimum(min_chunk_sum(la_ref), min_chunk_sum(lac_ref)) >= -GLA_MILD_DECAY

    def steps(refs, acc, n, is_mild):
        stages = _gla_stages(*refs, acc, st_ref, *consts, is_mild)

        per = min(GLA_CHUNKS_PER_STEP, n)

        def body(i, carry):
            items = []
            for j in range(per):
                c = i * per + j
                items += [(c, 0), (n - 1 - c, 1)]
            _skewed(stages, items)
            return carry

        lax.fori_loop(0, n // per, body, 0)

    for is_mild in (True, False):
        @pl.when(mild if is_mild else jnp.logical_not(mild))
        def _():
            steps((qc_ref, kc_ref, vc_ref, lac_ref), accc_ref, c_chunks, is_mild)
            steps((q_ref, k_ref, v_ref, la_ref), acc_ref, n_chunks, is_mild)

    _gla_finish(acc_ref, g_ref, nw_ref, o_ref, 256)
    _gla_finish(accc_ref, gc_ref, nw_ref, oc_ref, 256)


def _gla(q, k, v, la, g, qc, kc, vc, lac, gc, nw, consts):
    B, N, _ = q.shape
    L = qc.shape[1]
    mcat, lmask, stmask = consts
    tok = lambda n, w: pl.BlockSpec((1, n, w), lambda b: (b, 0, 0))
    full = lambda a: pl.BlockSpec(a.shape, lambda b: (0,) * a.ndim)
    return pl.pallas_call(
        _gla_kernel,
        grid=(B,),
        in_specs=[tok(N, GLA_KW), tok(N, GLA_KW), tok(N, GLA_VW), tok(N, 2 * GLA_KW), tok(N, GLA_VW),
                  tok(L, GLA_KW), tok(L, GLA_KW), tok(L, GLA_VW), tok(L, 2 * GLA_KW), tok(L, GLA_VW),
                  full(nw), full(mcat), full(lmask), full(stmask)],
        out_specs=[tok(N, GLA_VW), tok(L, GLA_VW)],
        out_shape=[jax.ShapeDtypeStruct((B, N, GLA_VW), BF16), jax.ShapeDtypeStruct((B, L, GLA_VW), BF16)],
        scratch_shapes=[pltpu.VMEM((2, N, GLA_VW), F32), pltpu.VMEM((2, L, GLA_VW), F32),
                        pltpu.VMEM((2, GLA_VW, GLA_KW), F32)],
        compiler_params=_cparams(1),
        name="gla",
    )(q, k, v, la, g, qc, kc, vc, lac, gc, nw, mcat, lmask, stmask)


_FFN_CHUNKS = ((0, 768), (768, 1536), (1536, 2304), (2304, FFN_HIDDEN))


def _mix_ffn_kernel(x_ref, na_ref, gl_ref, ga_ref, wm_ref, g1_ref, l1g_ref, l1b_ref,
                    sc_ref, sh_ref, g2_ref, wi_ref, wo_ref, l2g_ref, l2b_ref, o_ref):
    ts = x_ref.shape[1] // FFN_SUBTILES
    x1_of, u_of, acc_of = {}, {}, {}

    def rows(sub):
        return slice(sub * ts, (sub + 1) * ts)

    def mixer_sublayer(sub):
        r = rows(sub)
        o = jnp.concatenate([na_ref[0, r, :], gl_ref[0, r, :], ga_ref[0, r, :]], axis=-1)
        y = DEEPNORM_ALPHA * x_ref[0, r, :] + g1_ref[0] * _dot(o, wm_ref[...])
        x1 = _layer_norm(y, l1g_ref[...], l1b_ref[...])
        x1_of[sub] = x1
        u_of[sub] = (x1 * (1.0 + sc_ref[0]) + sh_ref[0]).astype(BF16)

    def up(item):
        sub, (a0, a1) = item
        if sub not in u_of:
            mixer_sublayer(sub)
        u = u_of[sub]
        return item, _dot(u, wi_ref[:, a0:a1]), _dot(u, wi_ref[:, FFN_HIDDEN + a0:FFN_HIDDEN + a1])

    def gate(st):
        item, ha, hb = st
        return item, (_silu(ha) * hb).astype(BF16)

    def down(st):
        (sub, (a0, a1)), t = st
        part = _dot(t, wo_ref[a0:a1, :])
        acc_of[sub] = acc_of[sub] + part if sub in acc_of else part
        if a1 == FFN_HIDDEN:
            y = DEEPNORM_ALPHA * x1_of.pop(sub) + g2_ref[0] * acc_of.pop(sub)
            o_ref[0, rows(sub), :] = _layer_norm(y, l2g_ref[...], l2b_ref[...])

    _skewed((up, gate, down), [(sub, ch) for sub in range(FFN_SUBTILES) for ch in _FFN_CHUNKS])


def _mix_ffn(x, o_na, o_gl, o_ga, wm, g1, l1g, l1b, sc, sh, g2, wi, wo, l2g, l2b, tm):
    B, N, D = x.shape
    tspec = lambda width: pl.BlockSpec((1, tm, width), lambda b, i: (b, i, 0))
    mod = pl.BlockSpec((1, 1, D), lambda b, i: (b, 0, 0))
    vec = pl.BlockSpec((1, D), lambda b, i: (0, 0))
    resident = lambda a: pl.BlockSpec(a.shape, lambda b, i: (0, 0), pipeline_mode=pl.Buffered(1))
    return pl.pallas_call(
        _mix_ffn_kernel,
        grid=(B, N // tm),
        in_specs=[tspec(D), tspec(NA_W), tspec(GLA_VW), tspec(GQA_QW), resident(wm), mod, vec, vec,
                  mod, mod, mod, resident(wi), resident(wo), vec, vec],
        out_specs=tspec(D),
        out_shape=jax.ShapeDtypeStruct((B, N, D), F32),
        compiler_params=_cparams(2),
        name="mix_ffn",
    )(x, o_na, o_gl, o_ga, wm, g1, l1g, l1b, sc, sh, g2, wi, wo, l2g, l2b)


def _rope_tables(n):
    t = jnp.arange(n)
    row = (t // GRID_W).astype(F32)
    col = (t % GRID_W).astype(F32)
    inv_freq = ROPE_THETA ** (-jnp.arange(ROPE_AXIS_PAIRS, dtype=F32) / ROPE_AXIS_PAIRS)
    ang_r = row[:, None] * inv_freq
    ang_c = col[:, None] * inv_freq
    ang = jnp.concatenate([ang_r, ang_r, ang_c, ang_c], axis=-1)
    sign = jnp.where((jnp.arange(HEAD_DIM) % 32) < 16, -1.0, 1.0).astype(F32)
    cos = jnp.tile(jnp.cos(ang), (1, 2))
    sin = jnp.tile(jnp.sin(ang) * sign, (1, 2))
    return cos, sin


def _pair_major(w, axis):
    shape = w.shape
    lead, tail = shape[:axis], shape[axis + 1:]
    w = w.reshape(lead + (GQA_KV_HEADS, GQA_REP, HEAD_DIM) + tail)
    w = jnp.swapaxes(w, axis, axis + 1)
    return w.reshape(shape)


def kernel(x, c, ctx, c_ctx, w_ada, b_ada, w_in, na_rpb, gla_wa2, gla_ba, gla_norm_w, gqa_qnorm_w,
           gqa_knorm_w, w_out, ln1_g, ln1_b, w_ffn_in, w_ffn_out, ln2_g, ln2_b):
    B, N, D = x.shape
    L = ctx.shape[1]
    depth = w_in.shape[0]
    tm = 1024
    tmc = min(B * L, 1024)

    pad = (-(B + 1)) % 8
    cvec = jnp.concatenate([c, c_ctx[None, :], jnp.zeros((pad, D), F32)], axis=0)
    mods = _ada(cvec, w_ada, b_ada)

    w_lr = w_in[:, :, _O_GLLR:_O_GAQ].reshape(depth, D, 2, GLA_RANK).transpose(0, 2, 1, 3)
    w_z = _fold_gate_weights(w_lr, gla_wa2)

    cos, sin = _rope_tables(N)
    cos_c = jnp.ones((B * L, LANES), F32)
    sin_c = jnp.zeros((B * L, LANES), F32)
    gla_consts = _gla_constants()
    per_batch = lambda t: t.reshape(B, L, t.shape[-1])
    flat = lambda t: t.reshape(1, B * L, t.shape[-1])

    xc = flat(ctx)
    for l in range(depth):
        ctx_out = l < depth - 1
        m_lat = mods[l, :B].reshape(B, 6, 1, D)
        m_ctx = mods[l, B].reshape(1, 6, 1, D)
        sh1, sc1, g1, sh2, sc2, g2 = (m_lat[:, i] for i in range(6))
        sh1c, sc1c, g1c, sh2c, sc2c, g2c = (m_ctx[:, i] for i in range(6))

        wl = w_in[l]
        w_proj = jnp.concatenate(
            [wl[:, _O_NAQ:_O_GLQ], _pair_major(wl[:, _O_GAQ:_O_GAK], 1), wl[:, _O_GLQ:_O_GLLR],
             w_z[l], wl[:, _O_GAK:_O_END]], axis=1).astype(BF16)
        zb = gla_ba[l].reshape(1, 2 * GLA_KW)
        qnw = jnp.tile(gqa_qnorm_w[l], 2).reshape(1, LANES)
        knw = jnp.tile(gqa_knorm_w[l], 2).reshape(1, LANES)
        glnw = jnp.tile(gla_norm_w[l], 2).reshape(1, LANES)
        wo_l = w_out[l]
        w_o = jnp.concatenate([wo_l[:NA_W + GLA_VW], _pair_major(wo_l[NA_W + GLA_VW:], 0)], axis=0).astype(BF16)
        lg1, lb1 = ln1_g[l].reshape(1, D), ln1_b[l].reshape(1, D)
        lg2, lb2 = ln2_g[l].reshape(1, D), ln2_b[l].reshape(1, D)
        wi = w_ffn_in[l].astype(BF16)
        wo = w_ffn_out[l].astype(BF16)

        (na_q, na_k, na_v, ga_q, gl_q, gl_k, gl_v, gl_g, gl_la, ga_k, ga_v) = _inproj(
            x, sc1, sh1, w_proj, zb, qnw, knw, cos, sin, tm)
        (na_qc, na_kc, na_vc, ga_qc, gl_qc, gl_kc, gl_vc, gl_gc, gl_lac, ga_kc, ga_vc) = map(
            per_batch, _inproj(xc, sc1c, sh1c, w_proj, zb, qnw, knw, cos_c, sin_c, tmc))

        bias = _na_bias(na_rpb[l].reshape(-1))
        o_na = _na_attention(na_q, na_k, na_v, na_kc, na_vc, bias)
        o_gl, oc_gl = _gla(gl_q, gl_k, gl_v, gl_la, gl_g, gl_qc, gl_kc, gl_vc, gl_lac, gl_gc, glnw, gla_consts)
        o_ga = _pair_attention(ga_q, ga_k, ga_v, ga_kc, ga_vc, shared_kv=True, tq=min(N, 2048))

        x = _mix_ffn(x, o_na, o_gl, o_ga, w_o, g1, lg1, lb1, sc2, sh2, g2, wi, wo, lg2, lb2, tm)

        if ctx_out:
            oc_na = _pair_attention(na_qc, na_kc, na_vc, shared_kv=False, tq=L)
            oc_ga = _pair_attention(ga_qc, ga_kc, ga_vc, shared_kv=True, tq=L)
            xc = _mix_ffn(xc, flat(oc_na), flat(oc_gl), flat(oc_ga), w_o, g1c, lg1, lb1, sc2c, sh2c, g2c,
                          wi, wo, lg2, lb2, tmc)
    return x
```

```python
import functools

import numpy as np
import jax
import jax.numpy as jnp
from jax import lax
from jax.experimental import pallas as pl
from jax.experimental.pallas import tpu as pltpu

F32 = jnp.float32
BF16 = jnp.bfloat16

D_MODEL = 1024
DEPTH = 2
GRID_W = 64
HEAD_DIM = 64
LANES = 128

NA_HEADS = 6
NA_WIN_ROWS = 8
NA_WIN_COLS = 16
NA_W = NA_HEADS * HEAD_DIM
NA_PAIRS = NA_W // LANES
NA_DR = 2 * NA_WIN_ROWS - 1
NA_DC = 2 * NA_WIN_COLS - 1
NA_ROWS_PER_STEP = 32

GLA_HEADS = 4
GLA_DK = 32
GLA_DV = 64
GLA_RANK = 16
GLA_GATE_NORM = 16.0
GLA_CHUNK = 64
GLA_KW = GLA_HEADS * GLA_DK
GLA_VW = GLA_HEADS * GLA_DV
GLA_LEVELS = 6
GLA_MILD_DECAY = 60.0
GLA_CHUNKS_PER_STEP = 8

GQA_Q_HEADS = 6
GQA_KV_HEADS = 2
GQA_REP = GQA_Q_HEADS // GQA_KV_HEADS
GQA_QW = GQA_Q_HEADS * HEAD_DIM
GQA_KVW = GQA_KV_HEADS * HEAD_DIM
GQA_PAIRS = GQA_QW // LANES
ROW_TILE = 1024
GQA_Q_TILE = 2048
INPROJ_SUBTILES = 4
FFN_SUBTILES = 2
PAIR_ATTN_SUB_ROWS = 128
ROPE_THETA = 10000.0
ROPE_AXIS_PAIRS = HEAD_DIM // 4

FFN_HIDDEN = 2816
MIX_W = NA_W + GLA_VW + GQA_QW

DEEPNORM_ALPHA = (2.0 * DEPTH) ** 0.25
LN_EPS = 1e-5
RMS_EPS = 1e-6
NEG_BIG = -1e30
LOG2_E = 1.4426950408889634
Q_SCALE = HEAD_DIM ** -0.5 * LOG2_E

_O_NAQ, _O_NAK, _O_NAV = 0, 384, 768
_O_GLQ, _O_GLK, _O_GLV, _O_GLG, _O_GLLR = 1152, 1280, 1408, 1664, 1920
_O_GAQ, _O_GAK, _O_GAV, _O_END = 1952, 2336, 2464, 2592
_C_NAQ, _C_NAK, _C_NAV, _C_GAQ = 0, 384, 768, 1152
_C_GLQ, _C_GLK, _C_GLV, _C_GLG, _C_Z = 1536, 1664, 1792, 2048, 2304
_C_GAK, _C_GAV, _C_END = 2560, 2688, 2816

VMEM_LIMIT = 48 * 1024 * 1024


def _cparams(n_axes):
    return pltpu.CompilerParams(dimension_semantics=("arbitrary",) * n_axes,
                                vmem_limit_bytes=VMEM_LIMIT)


def _dot(a, b):
    return jnp.dot(a, b, preferred_element_type=F32)


def _dot_nt(a, b):
    return lax.dot_general(a, b, (((1,), (1,)), ((), ())), preferred_element_type=F32)


def _dot_tn(a, b):
    return lax.dot_general(a, b, (((0,), (0,)), ((), ())), preferred_element_type=F32)


def _silu(x):
    return x * jax.nn.sigmoid(x)


def _layer_norm(y, g, b):
    mu = jnp.mean(y, axis=-1, keepdims=True)
    d = y - mu
    var = jnp.mean(d * d, axis=-1, keepdims=True)
    return d * lax.rsqrt(var + LN_EPS) * g + b


def _skewed(stages, items):
    n, k = len(items), len(stages)
    live = {}
    for j in range(n + k - 1):
        for s in range(k):
            idx = j - s
            if 0 <= idx < n:
                live[s, idx] = stages[s](items[idx] if s == 0 else live.pop((s - 1, idx)))


def _half_mean_square(y, lo):
    s = y * y
    s_lo = jnp.sum(jnp.where(lo, s, 0.0), axis=-1, keepdims=True)
    s_hi = jnp.sum(jnp.where(lo, 0.0, s), axis=-1, keepdims=True)
    return jnp.where(lo, s_lo, s_hi) * (1.0 / HEAD_DIM)


_ADA_TN = 1024


def _ada_kernel(c_ref, w_ref, b_ref, o_ref):
    s = _silu(c_ref[...])
    o_ref[0] = jnp.dot(s, w_ref[0], precision=lax.Precision.HIGHEST,
                       preferred_element_type=F32) + b_ref[0]


def _ada(cvec, w_ada, b_ada):
    rows = cvec.shape[0]
    depth, d, n6 = w_ada.shape
    return pl.pallas_call(
        _ada_kernel,
        grid=(depth, n6 // _ADA_TN),
        in_specs=[pl.BlockSpec((rows, d), lambda l, j: (0, 0)),
                  pl.BlockSpec((1, d, _ADA_TN), lambda l, j: (l, 0, j)),
                  pl.BlockSpec((1, 1, _ADA_TN), lambda l, j: (l, 0, j))],
        out_specs=pl.BlockSpec((1, rows, _ADA_TN), lambda l, j: (l, 0, j)),
        out_shape=jax.ShapeDtypeStruct((depth, rows, n6), F32),
        compiler_params=_cparams(2),
        name="ada",
    )(cvec, w_ada, b_ada.reshape(depth, 1, n6))


def _fold_kernel(wlr_ref, wa2_ref, o_ref):
    for e in range(2):
        o_ref[0, :, e * GLA_KW:(e + 1) * GLA_KW] = jnp.dot(
            wlr_ref[0, e], wa2_ref[0, e], precision=lax.Precision.HIGHEST,
            preferred_element_type=F32)


def _fold_gate_weights(w_lr, wa2):
    depth = w_lr.shape[0]
    return pl.pallas_call(
        _fold_kernel,
        grid=(depth,),
        in_specs=[pl.BlockSpec((1, 2, D_MODEL, GLA_RANK), lambda l: (l, 0, 0, 0)),
                  pl.BlockSpec((1, 2, GLA_RANK, GLA_KW), lambda l: (l, 0, 0, 0))],
        out_specs=pl.BlockSpec((1, D_MODEL, 2 * GLA_KW), lambda l: (l, 0, 0)),
        out_shape=jax.ShapeDtypeStruct((depth, D_MODEL, 2 * GLA_KW), F32),
        compiler_params=_cparams(1),
        name="fold_gate",
    )(w_lr, wa2)


def _inproj_kernel(x_ref, sc_ref, sh_ref, w_ref, zb_ref, qnw_ref, knw_ref, cos_ref, sin_ref,
                   naq_ref, nak_ref, nav_ref, gaq_ref, glq_ref, glk_ref, glv_ref, glg_ref,
                   gla_ref, gak_ref, gav_ref):
    ts = x_ref.shape[1] // INPROJ_SUBTILES
    lane = lax.broadcasted_iota(jnp.int32, (ts, LANES), 1)
    lo = lane < HEAD_DIM
    first16 = (lane & 31) < 16
    u_of = {}

    def rows(sub):
        return slice(sub * ts, (sub + 1) * ts)

    def matmul(item):
        sub, (a, b, epilogue) = item
        if sub not in u_of:
            u_of[sub] = (x_ref[0, rows(sub), :] * (1.0 + sc_ref[0]) + sh_ref[0]).astype(BF16)
        return sub, epilogue, _dot(u_of[sub], w_ref[:, a:b])

    def norm_rope(y, w, r):
        yn = y * lax.rsqrt(_half_mean_square(y, lo) + RMS_EPS) * w
        rot = jnp.where(first16, pltpu.roll(yn, LANES - 16, 1), pltpu.roll(yn, 16, 1))
        return yn * cos_ref[r, :] + rot * sin_ref[r, :]

    def na_qk(r, y):
        naq_ref[0, r, :] = (y[:, :NA_W] * Q_SCALE).astype(BF16)
        nak_ref[0, r, :] = y[:, NA_W:].astype(BF16)

    def na_v_gqa_q(r, y):
        nav_ref[0, r, :] = y[:, :NA_W].astype(BF16)
        qnw = qnw_ref[...]
        for p in range(GQA_PAIRS):
            t = norm_rope(y[:, NA_W + p * LANES:NA_W + (p + 1) * LANES], qnw, r)
            gaq_ref[0, r, p * LANES:(p + 1) * LANES] = (t * Q_SCALE).astype(BF16)

    def gla_qkv(r, y):
        glq_ref[0, r, :] = y[:, :GLA_KW] * GLA_DK ** -0.5
        glk_ref[0, r, :] = y[:, GLA_KW:2 * GLA_KW]
        glv_ref[0, r, :] = y[:, 2 * GLA_KW:].astype(BF16)

    def gla_gates(r, y):
        glg_ref[0, r, :] = y[:, :GLA_VW]
        z = y[:, GLA_VW:] + zb_ref[...]
        log_sig = jnp.minimum(z, 0.0) - jnp.log1p(jnp.exp(-jnp.abs(z)))
        gla_ref[0, r, :] = log_sig * (1.0 / GLA_GATE_NORM)

    def gqa_kv(r, y):
        gak_ref[0, r, :] = norm_rope(y[:, :LANES], knw_ref[...], r).astype(BF16)
        gav_ref[0, r, :] = y[:, LANES:].astype(BF16)

    segments = [(_C_NAQ, _C_NAV, na_qk), (_C_NAV, _C_GLQ, na_v_gqa_q), (_C_GLQ, _C_GLG, gla_qkv),
                (_C_GLG, _C_GAK, gla_gates), (_C_GAK, _C_END, gqa_kv)]
    _skewed((matmul, lambda st: st[1](rows(st[0]), st[2])),
            [(sub, sg) for sub in range(INPROJ_SUBTILES) for sg in segments])


def _inproj(x, sc, sh, w, zb, qnw, knw, cos, sin, tm):
    B, N, D = x.shape
    tok = lambda width, dt: jax.ShapeDtypeStruct((B, N, width), dt)
    tspec = lambda width: pl.BlockSpec((1, tm, width), lambda b, i: (b, i, 0))
    vec = lambda width: pl.BlockSpec((1, width), lambda b, i: (0, 0))
    out_shapes = [tok(NA_W, BF16), tok(NA_W, BF16), tok(NA_W, BF16), tok(GQA_QW, BF16),
                  tok(GLA_KW, F32), tok(GLA_KW, F32), tok(GLA_VW, BF16), tok(GLA_VW, F32),
                  tok(2 * GLA_KW, F32), tok(GQA_KVW, BF16), tok(GQA_KVW, BF16)]
    return pl.pallas_call(
        _inproj_kernel,
        grid=(B, N // tm),
        in_specs=[tspec(D),
                  pl.BlockSpec((1, 1, D), lambda b, i: (b, 0, 0)),
                  pl.BlockSpec((1, 1, D), lambda b, i: (b, 0, 0)),
                  pl.BlockSpec((D, _C_END), lambda b, i: (0, 0)),
                  vec(2 * GLA_KW), vec(LANES), vec(LANES),
                  pl.BlockSpec((tm, LANES), lambda b, i: (i, 0)),
                  pl.BlockSpec((tm, LANES), lambda b, i: (i, 0))],
        out_specs=[tspec(s.shape[-1]) for s in out_shapes],
        out_shape=out_shapes,
        compiler_params=_cparams(2),
        name="inproj",
    )(x, sc, sh, w, zb, qnw, knw, cos, sin)


def _nabias_kernel(rpb_ref, o_ref):
    h = pl.program_id(0)
    qc = lax.broadcasted_iota(jnp.int32, (GRID_W, LANES), 0)
    lane = lax.broadcasted_iota(jnp.int32, (GRID_W, LANES), 1)
    kc = lane & (GRID_W - 1)
    hi = lane >= GRID_W
    idx = jnp.clip(kc - qc, -(NA_WIN_COLS - 1), NA_WIN_COLS - 1) + (NA_WIN_COLS - 1)
    start = jnp.clip(qc - NA_WIN_COLS // 2, 0, GRID_W - NA_WIN_COLS)
    col_in = jnp.logical_and(kc >= start, kc < start + NA_WIN_COLS)
    base = h * (NA_DR * NA_DC)

    pair_tables = []
    for dr0 in range(NA_DR - 1):

        def body(j, t, dr0=dr0):
            s0 = rpb_ref[base + dr0 * NA_DC + j]
            s1 = rpb_ref[base + (dr0 + 1) * NA_DC + j]
            return jnp.where(idx == j, jnp.where(hi, s1, s0), t)

        t = lax.fori_loop(0, NA_DC, body, jnp.zeros((GRID_W, LANES), F32), unroll=True)
        pair_tables.append(jnp.where(col_in, t * LOG2_E, NEG_BIG))
    for oi in range(NA_WIN_ROWS):
        for i in range(NA_WIN_ROWS // 2):
            o_ref[0, oi, :, i * LANES:(i + 1) * LANES] = pair_tables[oi + 2 * i]


def _na_bias(rpb_flat):
    return pl.pallas_call(
        _nabias_kernel,
        grid=(NA_HEADS,),
        in_specs=[pl.BlockSpec(memory_space=pltpu.SMEM)],
        out_specs=pl.BlockSpec((1, NA_WIN_ROWS, GRID_W, NA_WIN_ROWS * GRID_W),
                               lambda h: (h, 0, 0, 0)),
        out_shape=jax.ShapeDtypeStruct((NA_HEADS, NA_WIN_ROWS, GRID_W, NA_WIN_ROWS * GRID_W), F32),
        compiler_params=_cparams(1),
        name="na_bias",
    )(rpb_flat)


def _na_kernel(q_ref, k_ref, v_ref, kc_ref, vc_ref, b_ref, o_ref):
    n = q_ref.shape[1]
    rows = n // GRID_W
    win = NA_WIN_ROWS * GRID_W
    lane = lax.broadcasted_iota(jnp.int32, (GRID_W, LANES), 1)
    lo = lane < HEAD_DIM
    kc = kc_ref[0]
    vc = vc_ref[0]
    zero = jnp.zeros((GRID_W, LANES), BF16)

    def scores(r):
        rs = jnp.clip(r - NA_WIN_ROWS // 2, 0, rows - NA_WIN_ROWS)
        oi = rs - r + (NA_WIN_ROWS - 1)
        q0 = pl.multiple_of(r * GRID_W, GRID_W)
        k0 = pl.multiple_of(rs * GRID_W, GRID_W)
        q = q_ref[0, pl.ds(q0, GRID_W), :]
        kw = k_ref[0, pl.ds(k0, win), :]
        qs = jnp.concatenate([jnp.where(lo, q, zero), jnp.where(lo, zero, q)], axis=0)
        s_lat = _dot_nt(qs, kw) + jnp.concatenate([b_ref[0, oi], b_ref[1, oi]], axis=0)
        s_ctx = _dot_nt(qs, kc)
        return q0, k0, s_lat, s_ctx

    def softmax(st):
        q0, k0, s_lat, s_ctx = st
        m = jnp.maximum(jnp.max(s_lat, axis=-1, keepdims=True),
                        jnp.max(s_ctx, axis=-1, keepdims=True))
        p_lat = jnp.exp2(s_lat - m)
        p_ctx = jnp.exp2(s_ctx - m)
        den = jnp.sum(p_lat, axis=-1, keepdims=True) + jnp.sum(p_ctx, axis=-1, keepdims=True)
        return q0, k0, p_lat.astype(BF16), p_ctx.astype(BF16), den

    def values(st):
        q0, k0, p_lat, p_ctx, den = st
        vw = v_ref[0, pl.ds(k0, win), :]
        o = (_dot(p_lat, vw) + _dot(p_ctx, vc)) / den
        o_ref[0, pl.ds(q0, GRID_W), :] = jnp.where(lo, o[:GRID_W], o[GRID_W:]).astype(BF16)

    per = min(NA_ROWS_PER_STEP, rows)

    def body(i, carry):
        _skewed((scores, softmax, values), [i * per + j for j in range(per)])
        return carry

    lax.fori_loop(0, rows // per, body, 0)


def _na_attention(q, k, v, kc, vc, bias):
    B, N, _ = q.shape
    L = kc.shape[1]
    lat = pl.BlockSpec((1, N, LANES), lambda b, p: (b, 0, p))
    cx = pl.BlockSpec((1, L, LANES), lambda b, p: (b, 0, p))
    return pl.pallas_call(
        _na_kernel,
        grid=(B, NA_PAIRS),
        in_specs=[lat, lat, lat, cx, cx,
                  pl.BlockSpec((2, NA_WIN_ROWS, GRID_W, NA_WIN_ROWS * GRID_W),
                               lambda b, p: (p, 0, 0, 0))],
        out_specs=lat,
        out_shape=jax.ShapeDtypeStruct((B, N, NA_W), BF16),
        compiler_params=_cparams(2),
        name="na_attn",
    )(q, k, v, kc, vc, bias)


def _pair_attn_kernel(*refs, two_sources):
    if two_sources:
        q_ref, ka_ref, va_ref, kb_ref, vb_ref, o_ref = refs
    else:
        q_ref, ka_ref, va_ref, o_ref = refs
    tq = q_ref.shape[1]
    ts = min(tq, PAIR_ATTN_SUB_ROWS)
    lane = lax.broadcasted_iota(jnp.int32, (ts, LANES), 1)
    lo = lane < HEAD_DIM
    zero = jnp.zeros((ts, LANES), BF16)

    def with_ones(v):
        return jnp.concatenate([v, jnp.ones_like(v)], axis=1)

    ka = ka_ref[0]
    va = with_ones(va_ref[0])
    if two_sources:
        kb = kb_ref[0]
        vb = with_ones(vb_ref[0])
    def scores(item):
        sub, h = item
        q = q_ref[0, sub * ts:(sub + 1) * ts, :]
        qm = jnp.where(lo, q, zero) if h == 0 else jnp.where(lo, zero, q)
        s_a = _dot_nt(qm, ka)
        s_b = _dot_nt(qm, kb) if two_sources else None
        return sub, h, s_a, s_b

    def softmax(st):
        sub, h, s_a, s_b = st
        m = jnp.max(s_a, axis=-1, keepdims=True)
        if two_sources:
            m = jnp.maximum(m, jnp.max(s_b, axis=-1, keepdims=True))
        p_a = jnp.exp2(s_a - m).astype(BF16)
        p_b = jnp.exp2(s_b - m).astype(BF16) if two_sources else None
        return sub, h, p_a, p_b

    done = {}

    def values(st):
        sub, h, p_a, p_b = st
        o = _dot(p_a, va)
        if two_sources:
            o = o + _dot(p_b, vb)
        done[sub, h] = o[:, :LANES] / o[:, LANES:]
        if h == 1:
            o_ref[0, sub * ts:(sub + 1) * ts, :] = jnp.where(
                lo, done.pop((sub, 0)), done.pop((sub, 1))).astype(BF16)

    _skewed((scores, softmax, values), [(sub, h) for sub in range(tq // ts) for h in range(2)])


def _pair_attention(q, ka, va, kb=None, vb=None, *, shared_kv, tq):
    B, Nq, W = q.shape
    pairs = W // LANES
    kv_map = (lambda b, p, i: (b, 0, 0)) if shared_kv else (lambda b, p, i: (b, 0, p))
    qspec = pl.BlockSpec((1, tq, LANES), lambda b, p, i: (b, i, p))
    in_specs = [qspec,
                pl.BlockSpec((1, ka.shape[1], LANES), kv_map),
                pl.BlockSpec((1, ka.shape[1], LANES), kv_map)]
    args = [q, ka, va]
    if kb is not None:
        in_specs += [pl.BlockSpec((1, kb.shape[1], LANES), kv_map),
                     pl.BlockSpec((1, kb.shape[1], LANES), kv_map)]
        args += [kb, vb]
    return pl.pallas_call(
        functools.partial(_pair_attn_kernel, two_sources=kb is not None),
        grid=(B, pairs, Nq // tq),
        in_specs=in_specs,
        out_specs=qspec,
        out_shape=jax.ShapeDtypeStruct((B, Nq, W), BF16),
        compiler_params=_cparams(3),
        name="pair_attn",
    )(*args)


def _gla_constants():
    c = GLA_CHUNK
    t = np.arange(c)[:, None]
    i = np.arange(c)[None, :]
    mcat = np.zeros((2, (GLA_LEVELS + 1) * c, c), np.float32)
    masks = np.zeros((2, GLA_LEVELS + 2, c, c), np.float32)
    for d in range(2):
        masks[d, GLA_LEVELS + 1] = (i <= t) if d == 0 else (i >= t)
        for l in range(1, GLA_LEVELS + 1):
            m = 2 ** l
            same = (t // m) == (i // m)
            q_side = same & ((i <= t) if d == 0 else (i >= t))
            k_side = same & ((i > t) if d == 0 else (i < t))
            if l < GLA_LEVELS:
                is_q_row = ((t // m) % 2 == 1) if d == 0 else ((t // m) % 2 == 0)
                mcat[d, (l - 1) * c:l * c] = np.where(is_q_row, q_side, k_side)
            else:
                mcat[d, (l - 1) * c:l * c] = q_side
                mcat[d, l * c:(l + 1) * c] = k_side
        masks[d, 0] = (t == i)
        for l in range(GLA_LEVELS):
            m = 2 ** l
            tb, sb = t // m, i // m
            if d == 0:
                masks[d, l + 1] = (tb % 2 == 1) & (sb == tb - 1)
            else:
                masks[d, l + 1] = (tb % 2 == 0) & (sb == tb + 1)
    masks = np.tile(masks, (1, 1, 1, GLA_HEADS))
    rows = np.arange(GLA_VW)[:, None] // GLA_DV
    cols = np.arange(GLA_KW)[None, :] // GLA_DK
    stmask = (rows == cols).astype(np.float32)
    return jnp.asarray(mcat, BF16), jnp.asarray(masks, F32), jnp.asarray(stmask, F32)


def _gla_stages(q_ref, k_ref, v_ref, la_ref, acc_ref, st_ref, mcat_ref, lmask_ref, stmask_ref, mild):
    ck = GLA_CHUNK
    whole = slice((GLA_LEVELS - 1) * ck, (GLA_LEVELS + 1) * ck)
    lane_k = lax.broadcasted_iota(jnp.int32, (1, GLA_KW), 1) // GLA_DK
    lane_v = lax.broadcasted_iota(jnp.int32, (1, GLA_VW), 1) // GLA_DV
    row = lax.broadcasted_iota(jnp.int32, (ck, GLA_KW), 0)

    def block_rows(x, lane_head):
        return jnp.concatenate([jnp.where(lane_head == h, x, 0.0) for h in range(GLA_HEADS)], axis=0)

    def exponents(item):
        c, d = item
        r0 = pl.multiple_of(c * ck, ck)
        g = la_ref[0, pl.ds(r0, ck), d * GLA_KW:(d + 1) * GLA_KW]
        g1 = g.astype(BF16)
        g2 = (g - g1.astype(F32)).astype(BF16)
        mc = mcat_ref[d, whole, :] if mild else mcat_ref[d]
        e = _dot(mc, g1) + _dot(mc, g2)
        return r0, d, g, e

    def in_chunk(st):
        r0, d, g, e = st
        q = q_ref[0, pl.ds(r0, ck), :]
        k = k_ref[0, pl.ds(r0, ck), :]
        q_chunk = e[-2 * ck:-ck]
        k_chunk = e[-ck:]
        last = ck - 1 if d == 0 else 0
        total = q_chunk[last:last + 1]
        q_in = (q * jnp.exp(q_chunk)).astype(BF16)

        def level_scores(qt, kt, idx):
            kb = block_rows(kt, lane_k).astype(BF16)
            return lmask_ref[d, idx] * _dot_nt(qt, kb)

        if mild:
            a = level_scores(q_in, k * jnp.exp(-q_chunk), GLA_LEVELS + 1)
        else:
            q_row0 = (row & 1) == (1 - d)
            a = level_scores(q.astype(BF16), k, 0)
            for l in range(GLA_LEVELS):
                dec = jnp.exp(jnp.where(q_row0, g, 0.0) if l == 0 else e[(l - 1) * ck:l * ck])
                a = a + level_scores((q * dec).astype(BF16), k * dec, l + 1)
        return r0, d, a.astype(BF16), q_in, (k * jnp.exp(k_chunk)).astype(BF16), jnp.exp(total)

    def state(st):
        r0, d, a, q_in, k_out, decay = st
        v = v_ref[0, pl.ds(r0, ck), :]
        vbd = block_rows(v.astype(F32), lane_v).astype(BF16)
        s_prev = st_ref[d]
        acc_ref[d, pl.ds(r0, ck), :] = _dot(a, vbd) + _dot_nt(q_in, s_prev.astype(BF16))
        st_ref[d] = s_prev * decay + stmask_ref[...] * _dot_tn(v, k_out)

    return exponents, in_chunk, state


def _gla_finish(acc_ref, g_ref, nw_ref, o_ref, tile):
    n = acc_ref.shape[1]
    lane = lax.broadcasted_iota(jnp.int32, (tile, LANES), 1)
    lo = lane < GLA_DV
    nw = nw_ref[...]

    def body(i, carry):
        r0 = pl.multiple_of(i * tile, tile)
        for j in range(GLA_VW // LANES):
            cols = slice(j * LANES, (j + 1) * LANES)
            o = acc_ref[0, pl.ds(r0, tile), cols] + acc_ref[1, pl.ds(r0, tile), cols]
            on = o * lax.rsqrt(_half_mean_square(o, lo) + RMS_EPS) * nw
            gate = _silu(g_ref[0, pl.ds(r0, tile), cols])
            o_ref[0, pl.ds(r0, tile), cols] = (on * gate).astype(BF16)
        return carry

    lax.fori_loop(0, n // tile, body, 0)


def _gla_kernel(q_ref, k_ref, v_ref, la_ref, g_ref, qc_ref, kc_ref, vc_ref, lac_ref, gc_ref,
                nw_ref, mcat_ref, lmask_ref, stmask_ref, o_ref, oc_ref, acc_ref, accc_ref, st_ref):
    n_chunks = q_ref.shape[1] // GLA_CHUNK
    c_chunks = qc_ref.shape[1] // GLA_CHUNK
    consts = (mcat_ref, lmask_ref, stmask_ref)
    st_ref[...] = jnp.zeros_like(st_ref)

    def min_chunk_sum(ref):
        la = ref[0]
        sums = jnp.sum(la.reshape(la.shape[0] // GLA_CHUNK, GLA_CHUNK, la.shape[1]), axis=1)
        return jnp.min(sums)

    mild = jnp.minimum(min_chunk_sum(la_ref), min_chunk_sum(lac_ref)) >= -GLA_MILD_DECAY

    def steps(refs, acc, n, is_mild):
        stages = _gla_stages(*refs, acc, st_ref, *consts, is_mild)

        per = min(GLA_CHUNKS_PER_STEP, n)

        def body(i, carry):
            items = []
            for j in range(per):
                c = i * per + j
                items += [(c, 0), (n - 1 - c, 1)]
            _skewed(stages, items)
            return carry

        lax.fori_loop(0, n // per, body, 0)

    for is_mild in (True, False):
        @pl.when(mild if is_mild else jnp.logical_not(mild))
        def _():
            steps((qc_ref, kc_ref, vc_ref, lac_ref), accc_ref, c_chunks, is_mild)
            steps((q_ref, k_ref, v_ref, la_ref), acc_ref, n_chunks, is_mild)

    _gla_finish(acc_ref, g_ref, nw_ref, o_ref, 256)
    _gla_finish(accc_ref, gc_ref, nw_ref, oc_ref, 256)


def _gla(q, k, v, la, g, qc, kc, vc, lac, gc, nw, consts):
    B, N, _ = q.shape
    L = qc.shape[1]
    mcat, lmask, stmask = consts
    tok = lambda n, w: pl.BlockSpec((1, n, w), lambda b: (b, 0, 0))
    full = lambda a: pl.BlockSpec(a.shape, lambda b: (0,) * a.ndim)
    return pl.pallas_call(
        _gla_kernel,
        grid=(B,),
        in_specs=[tok(N, GLA_KW), tok(N, GLA_KW), tok(N, GLA_VW), tok(N, 2 * GLA_KW), tok(N, GLA_VW),
                  tok(L, GLA_KW), tok(L, GLA_KW), tok(L, GLA_VW), tok(L, 2 * GLA_KW), tok(L, GLA_VW),
                  full(nw), full(mcat), full(lmask), full(stmask)],
        out_specs=[tok(N, GLA_VW), tok(L, GLA_VW)],
        out_shape=[jax.ShapeDtypeStruct((B, N, GLA_VW), BF16), jax.ShapeDtypeStruct((B, L, GLA_VW), BF16)],
        scratch_shapes=[pltpu.VMEM((2, N, GLA_VW), F32), pltpu.VMEM((2, L, GLA_VW), F32),
                        pltpu.VMEM((2, GLA_VW, GLA_KW), F32)],
        compiler_params=_cparams(1),
        name="gla",
    )(q, k, v, la, g, qc, kc, vc, lac, gc, nw, mcat, lmask, stmask)


_FFN_CHUNKS = ((0, 768), (768, 1536), (1536, 2304), (2304, FFN_HIDDEN))


def _mix_ffn_kernel(x_ref, na_ref, gl_ref, ga_ref, wm_ref, g1_ref, l1g_ref, l1b_ref,
                    sc_ref, sh_ref, g2_ref, wi_ref, wo_ref, l2g_ref, l2b_ref, o_ref):
    ts = x_ref.shape[1] // FFN_SUBTILES
    x1_of, u_of, acc_of = {}, {}, {}

    def rows(sub):
        return slice(sub * ts, (sub + 1) * ts)

    def mixer_sublayer(sub):
        r = rows(sub)
        o = jnp.concatenate([na_ref[0, r, :], gl_ref[0, r, :], ga_ref[0, r, :]], axis=-1)
        y = DEEPNORM_ALPHA * x_ref[0, r, :] + g1_ref[0] * _dot(o, wm_ref[...])
        x1 = _layer_norm(y, l1g_ref[...], l1b_ref[...])
        x1_of[sub] = x1
        u_of[sub] = (x1 * (1.0 + sc_ref[0]) + sh_ref[0]).astype(BF16)

    def up(item):
        sub, (a0, a1) = item
        if sub not in u_of:
            mixer_sublayer(sub)
        u = u_of[sub]
        return item, _dot(u, wi_ref[:, a0:a1]), _dot(u, wi_ref[:, FFN_HIDDEN + a0:FFN_HIDDEN + a1])

    def gate(st):
        item, ha, hb = st
        return item, (_silu(ha) * hb).astype(BF16)

    def down(st):
        (sub, (a0, a1)), t = st
        part = _dot(t, wo_ref[a0:a1, :])
        acc_of[sub] = acc_of[sub] + part if sub in acc_of else part
        if a1 == FFN_HIDDEN:
            y = DEEPNORM_ALPHA * x1_of.pop(sub) + g2_ref[0] * acc_of.pop(sub)
            o_ref[0, rows(sub), :] = _layer_norm(y, l2g_ref[...], l2b_ref[...])

    _skewed((up, gate, down), [(sub, ch) for sub in range(FFN_SUBTILES) for ch in _FFN_CHUNKS])


def _mix_ffn(x, o_na, o_gl, o_ga, wm, g1, l1g, l1b, sc, sh, g2, wi, wo, l2g, l2b, tm):
    B, N, D = x.shape
    tspec = lambda width: pl.BlockSpec((1, tm, width), lambda b, i: (b, i, 0))
    mod = pl.BlockSpec((1, 1, D), lambda b, i: (b, 0, 0))
    vec = pl.BlockSpec((1, D), lambda b, i: (0, 0))
    resident = lambda a: pl.BlockSpec(a.shape, lambda b, i: (0, 0), pipeline_mode=pl.Buffered(1))
    return pl.pallas_call(
        _mix_ffn_kernel,
        grid=(B, N // tm),
        in_specs=[tspec(D), tspec(NA_W), tspec(GLA_VW), tspec(GQA_QW), resident(wm), mod, vec, vec,
                  mod, mod, mod, resident(wi), resident(wo), vec, vec],
        out_specs=tspec(D),
        out_shape=jax.ShapeDtypeStruct((B, N, D), F32),
        compiler_params=_cparams(2),
        name="mix_ffn",
    )(x, o_na, o_gl, o_ga, wm, g1, l1g, l1b, sc, sh, g2, wi, wo, l2g, l2b)


def _rope_tables(n):
    t = jnp.arange(n)
    row = (t // GRID_W).astype(F32)
    col = (t % GRID_W).astype(F32)
    inv_freq = ROPE_THETA ** (-jnp.arange(ROPE_AXIS_PAIRS, dtype=F32) / ROPE_AXIS_PAIRS)
    ang_r = row[:, None] * inv_freq
    ang_c = col[:, None] * inv_freq
    ang = jnp.concatenate([ang_r, ang_r, ang_c, ang_c], axis=-1)
    sign = jnp.where((jnp.arange(HEAD_DIM) % 32) < 16, -1.0, 1.0).astype(F32)
    cos = jnp.tile(jnp.cos(ang), (1, 2))
    sin = jnp.tile(jnp.sin(ang) * sign, (1, 2))
    return cos, sin


def _pair_major(w, axis):
    shape = w.shape
    lead, tail = shape[:axis], shape[axis + 1:]
    w = w.reshape(lead + (GQA_KV_HEADS, GQA_REP, HEAD_DIM) + tail)
    w = jnp.swapaxes(w, axis, axis + 1)
    return w.reshape(shape)


def kernel(x, c, ctx, c_ctx, w_ada, b_ada, w_in, na_rpb, gla_wa2, gla_ba, gla_norm_w, gqa_qnorm_w,
           gqa_knorm_w, w_out, ln1_g, ln1_b, w_ffn_in, w_ffn_out, ln2_g, ln2_b):
    B, N, D = x.shape
    L = ctx.shape[1]
    depth = w_in.shape[0]
    tm = min(N, ROW_TILE)
    tmc = min(B * L, ROW_TILE)

    pad = (-(B + 1)) % 8
    cvec = jnp.concatenate([c, c_ctx[None, :], jnp.zeros((pad, D), F32)], axis=0)
    mods = _ada(cvec, w_ada, b_ada)

    w_lr = w_in[:, :, _O_GLLR:_O_GAQ].reshape(depth, D, 2, GLA_RANK).transpose(0, 2, 1, 3)
    w_z = _fold_gate_weights(w_lr, gla_wa2)

    cos, sin = _rope_tables(N)
    cos_c = jnp.ones((B * L, LANES), F32)
    sin_c = jnp.zeros((B * L, LANES), F32)
    gla_consts = _gla_constants()
    per_batch = lambda t: t.reshape(B, L, t.shape[-1])
    flat = lambda t: t.reshape(1, B * L, t.shape[-1])

    xc = flat(ctx)
    for l in range(depth):
        ctx_out = l < depth - 1
        m_lat = mods[l, :B].reshape(B, 6, 1, D)
        m_ctx = mods[l, B].reshape(1, 6, 1, D)
        sh1, sc1, g1, sh2, sc2, g2 = (m_lat[:, i] for i in range(6))
        sh1c, sc1c, g1c, sh2c, sc2c, g2c = (m_ctx[:, i] for i in range(6))

        wl = w_in[l]
        w_proj = jnp.concatenate(
            [wl[:, _O_NAQ:_O_GLQ], _pair_major(wl[:, _O_GAQ:_O_GAK], 1), wl[:, _O_GLQ:_O_GLLR],
             w_z[l], wl[:, _O_GAK:_O_END]], axis=1).astype(BF16)
        zb = gla_ba[l].reshape(1, 2 * GLA_KW)
        qnw = jnp.tile(gqa_qnorm_w[l], 2).reshape(1, LANES)
        knw = jnp.tile(gqa_knorm_w[l], 2).reshape(1, LANES)
        glnw = jnp.tile(gla_norm_w[l], 2).reshape(1, LANES)
        wo_l = w_out[l]
        w_o = jnp.concatenate([wo_l[:NA_W + GLA_VW], _pair_major(wo_l[NA_W + GLA_VW:], 0)], axis=0).astype(BF16)
        lg1, lb1 = ln1_g[l].reshape(1, D), ln1_b[l].reshape(1, D)
        lg2, lb2 = ln2_g[l].reshape(1, D), ln2_b[l].reshape(1, D)
        wi = w_ffn_in[l].astype(BF16)
        wo = w_ffn_out[l].astype(BF16)

        (na_q, na_k, na_v, ga_q, gl_q, gl_k, gl_v, gl_g, gl_la, ga_k, ga_v) = _inproj(
            x, sc1, sh1, w_proj, zb, qnw, knw, cos, sin, tm)
        (na_qc, na_kc, na_vc, ga_qc, gl_qc, gl_kc, gl_vc, gl_gc, gl_lac, ga_kc, ga_vc) = map(
            per_batch, _inproj(xc, sc1c, sh1c, w_proj, zb, qnw, knw, cos_c, sin_c, tmc))

        bias = _na_bias(na_rpb[l].reshape(-1))
        o_na = _na_attention(na_q, na_k, na_v, na_kc, na_vc, bias)
        o_gl, oc_gl = _gla(gl_q, gl_k, gl_v, gl_la, gl_g, gl_qc, gl_kc, gl_vc, gl_lac, gl_gc, glnw, gla_consts)
        o_ga = _pair_attention(ga_q, ga_k, ga_v, ga_kc, ga_vc, shared_kv=True, tq=min(N, GQA_Q_TILE))

        x = _mix_ffn(x, o_na, o_gl, o_ga, w_o, g1, lg1, lb1, sc2, sh2, g2, wi, wo, lg2, lb2, tm)

        if ctx_out:
            oc_na = _pair_attention(na_qc, na_kc, na_vc, shared_kv=False, tq=L)
            oc_ga = _pair_attention(ga_qc, ga_kc, ga_vc, shared_kv=True, tq=L)
            xc = _mix_ffn(xc, flat(oc_na), flat(oc_gl), flat(oc_ga), w_o, g1c, lg1, lb1, sc2c, sh2c, g2c,
                          wi, wo, lg2, lb2, tmc)
    return x
```

```python
import functools
import math

import numpy as np
import jax
import jax.numpy as jnp
from jax import lax
from jax.experimental import pallas as pl
from jax.experimental.pallas import tpu as pltpu

F32 = jnp.float32
BF16 = jnp.bfloat16

D_MODEL = 1024
DEPTH = 2
GRID_W = 64
HEAD_DIM = 64
LANES = 128

NA_HEADS = 6
NA_WIN_ROWS = 8
NA_WIN_COLS = 16
NA_W = NA_HEADS * HEAD_DIM
NA_PAIRS = NA_W // LANES
NA_DR = 2 * NA_WIN_ROWS - 1
NA_DC = 2 * NA_WIN_COLS - 1
NA_ROWS_PER_STEP = 32

GLA_HEADS = 4
GLA_DK = 32
GLA_DV = 64
GLA_RANK = 16
GLA_GATE_NORM = 16.0
GLA_CHUNK = 64
GLA_KW = GLA_HEADS * GLA_DK
GLA_VW = GLA_HEADS * GLA_DV
GLA_LEVELS = 6
GLA_MILD_DECAY = 40.0
GLA_CHUNKS_PER_STEP = 8

GQA_Q_HEADS = 6
GQA_KV_HEADS = 2
GQA_REP = GQA_Q_HEADS // GQA_KV_HEADS
GQA_QW = GQA_Q_HEADS * HEAD_DIM
GQA_KVW = GQA_KV_HEADS * HEAD_DIM
GQA_PAIRS = GQA_QW // LANES
ROW_TILE = 1024
GQA_Q_TILE = 2048
CTX_ATTN_BATCH_BLOCK = 4
INPROJ_SUBTILES = 4
FFN_SUBTILES = 2
PAIR_ATTN_SUB_ROWS = 128
ROPE_THETA = 10000.0
ROPE_AXIS_PAIRS = HEAD_DIM // 4

FFN_HIDDEN = 2816
MIX_W = NA_W + GLA_VW + GQA_QW

DEEPNORM_ALPHA = (2.0 * DEPTH) ** 0.25
LN_EPS = 1e-5
RMS_EPS = 1e-6
NEG_BIG = -1e30
LOG2_E = 1.4426950408889634
Q_SCALE = HEAD_DIM ** -0.5 * LOG2_E

_O_NAQ, _O_NAK, _O_NAV = 0, 384, 768
_O_GLQ, _O_GLK, _O_GLV, _O_GLG, _O_GLLR = 1152, 1280, 1408, 1664, 1920
_O_GAQ, _O_GAK, _O_GAV, _O_END = 1952, 2336, 2464, 2592
_C_NAQ, _C_NAK, _C_NAV, _C_GAQ = 0, 384, 768, 1152
_C_GLQ, _C_GLK, _C_GLV, _C_GLG, _C_Z = 1536, 1664, 1792, 2048, 2304
_C_GAK, _C_GAV, _C_END = 2560, 2688, 2816

VMEM_LIMIT = 48 * 1024 * 1024


def _cparams(n_axes):
    return pltpu.CompilerParams(dimension_semantics=("arbitrary",) * n_axes,
                                vmem_limit_bytes=VMEM_LIMIT)


def _dot(a, b):
    return jnp.dot(a, b, preferred_element_type=F32)


def _dot_nt(a, b):
    return lax.dot_general(a, b, (((1,), (1,)), ((), ())), preferred_element_type=F32)


def _dot_tn(a, b):
    return lax.dot_general(a, b, (((0,), (0,)), ((), ())), preferred_element_type=F32)


def _silu(x):
    return x * jax.nn.sigmoid(x)


def _layer_norm(y, g, b):
    mu = jnp.mean(y, axis=-1, keepdims=True)
    d = y - mu
    var = jnp.mean(d * d, axis=-1, keepdims=True)
    return d * lax.rsqrt(var + LN_EPS) * g + b


def _skewed(stages, items):
    n, k = len(items), len(stages)
    live = {}
    for j in range(n + k - 1):
        for s in range(k):
            idx = j - s
            if 0 <= idx < n:
                live[s, idx] = stages[s](items[idx] if s == 0 else live.pop((s - 1, idx)))


def _half_mean_square(y, lo):
    s = y * y
    s_lo = jnp.sum(jnp.where(lo, s, 0.0), axis=-1, keepdims=True)
    s_hi = jnp.sum(jnp.where(lo, 0.0, s), axis=-1, keepdims=True)
    return jnp.where(lo, s_lo, s_hi) * (1.0 / HEAD_DIM)


_ADA_TN = 1024


def _ada_kernel(c_ref, w_ref, b_ref, o_ref):
    s = _silu(c_ref[...])
    o_ref[0] = jnp.dot(s, w_ref[0], precision=lax.Precision.HIGHEST,
                       preferred_element_type=F32) + b_ref[0]


def _ada(cvec, w_ada, b_ada):
    rows = cvec.shape[0]
    depth, d, n6 = w_ada.shape
    return pl.pallas_call(
        _ada_kernel,
        grid=(depth, n6 // _ADA_TN),
        in_specs=[pl.BlockSpec((rows, d), lambda l, j: (0, 0)),
                  pl.BlockSpec((1, d, _ADA_TN), lambda l, j: (l, 0, j)),
                  pl.BlockSpec((1, 1, _ADA_TN), lambda l, j: (l, 0, j))],
        out_specs=pl.BlockSpec((1, rows, _ADA_TN), lambda l, j: (l, 0, j)),
        out_shape=jax.ShapeDtypeStruct((depth, rows, n6), F32),
        compiler_params=_cparams(2),
        name="ada",
    )(cvec, w_ada, b_ada.reshape(depth, 1, n6))


def _fold_kernel(wlr_ref, wa2_ref, o_ref):
    for e in range(2):
        o_ref[0, :, e * GLA_KW:(e + 1) * GLA_KW] = jnp.dot(
            wlr_ref[0, e], wa2_ref[0, e], precision=lax.Precision.HIGHEST,
            preferred_element_type=F32)


def _fold_gate_weights(w_lr, wa2):
    depth = w_lr.shape[0]
    return pl.pallas_call(
        _fold_kernel,
        grid=(depth,),
        in_specs=[pl.BlockSpec((1, 2, D_MODEL, GLA_RANK), lambda l: (l, 0, 0, 0)),
                  pl.BlockSpec((1, 2, GLA_RANK, GLA_KW), lambda l: (l, 0, 0, 0))],
        out_specs=pl.BlockSpec((1, D_MODEL, 2 * GLA_KW), lambda l: (l, 0, 0)),
        out_shape=jax.ShapeDtypeStruct((depth, D_MODEL, 2 * GLA_KW), F32),
        compiler_params=_cparams(1),
        name="fold_gate",
    )(w_lr, wa2)


def _inproj_kernel(x_ref, sc_ref, sh_ref, w_ref, zb_ref, qnw_ref, knw_ref, cos_ref, sin_ref,
                   naq_ref, nak_ref, nav_ref, gaq_ref, glq_ref, glk_ref, glv_ref, glg_ref,
                   gla_ref, gak_ref, gav_ref):
    ts = x_ref.shape[1] // INPROJ_SUBTILES
    lane = lax.broadcasted_iota(jnp.int32, (ts, LANES), 1)
    lo = lane < HEAD_DIM
    first16 = (lane & 31) < 16
    u_of = {}

    def rows(sub):
        return slice(sub * ts, (sub + 1) * ts)

    def matmul(item):
        sub, (a, b, epilogue) = item
        if sub not in u_of:
            u_of[sub] = (x_ref[0, rows(sub), :] * (1.0 + sc_ref[0]) + sh_ref[0]).astype(BF16)
        return sub, epilogue, _dot(u_of[sub], w_ref[:, a:b])

    def norm_rope(y, w, r):
        yn = y * lax.rsqrt(_half_mean_square(y, lo) + RMS_EPS) * w
        rot = jnp.where(first16, pltpu.roll(yn, LANES - 16, 1), pltpu.roll(yn, 16, 1))
        return yn * cos_ref[r, :] + rot * sin_ref[r, :]

    def na_qk(r, y):
        naq_ref[0, r, :] = (y[:, :NA_W] * Q_SCALE).astype(BF16)
        nak_ref[0, r, :] = y[:, NA_W:].astype(BF16)

    def na_v_gqa_q(r, y):
        nav_ref[0, r, :] = y[:, :NA_W].astype(BF16)
        qnw = qnw_ref[...]
        for p in range(GQA_PAIRS):
            t = norm_rope(y[:, NA_W + p * LANES:NA_W + (p + 1) * LANES], qnw, r)
            gaq_ref[0, r, p * LANES:(p + 1) * LANES] = (t * Q_SCALE).astype(BF16)

    def gla_qkv(r, y):
        glq_ref[0, r, :] = y[:, :GLA_KW] * GLA_DK ** -0.5
        glk_ref[0, r, :] = y[:, GLA_KW:2 * GLA_KW]
        glv_ref[0, r, :] = y[:, 2 * GLA_KW:].astype(BF16)

    def gla_gates(r, y):
        glg_ref[0, r, :] = y[:, :GLA_VW]
        z = y[:, GLA_VW:] + zb_ref[...]
        log_sig = jnp.minimum(z, 0.0) - jnp.log1p(jnp.exp(-jnp.abs(z)))
        gla_ref[0, r, :] = log_sig * (1.0 / GLA_GATE_NORM)

    def gqa_kv(r, y):
        gak_ref[0, r, :] = norm_rope(y[:, :LANES], knw_ref[...], r).astype(BF16)
        gav_ref[0, r, :] = y[:, LANES:].astype(BF16)

    segments = [(_C_NAQ, _C_NAV, na_qk), (_C_NAV, _C_GLQ, na_v_gqa_q), (_C_GLQ, _C_GLG, gla_qkv),
                (_C_GLG, _C_GAK, gla_gates), (_C_GAK, _C_END, gqa_kv)]
    _skewed((matmul, lambda st: st[1](rows(st[0]), st[2])),
            [(sub, sg) for sub in range(INPROJ_SUBTILES) for sg in segments])


def _inproj(x, sc, sh, w, zb, qnw, knw, cos, sin, tm):
    B, N, D = x.shape
    tok = lambda width, dt: jax.ShapeDtypeStruct((B, N, width), dt)
    tspec = lambda width: pl.BlockSpec((1, tm, width), lambda b, i: (b, i, 0))
    vec = lambda width: pl.BlockSpec((1, width), lambda b, i: (0, 0))
    out_shapes = [tok(NA_W, BF16), tok(NA_W, BF16), tok(NA_W, BF16), tok(GQA_QW, BF16),
                  tok(GLA_KW, F32), tok(GLA_KW, F32), tok(GLA_VW, BF16), tok(GLA_VW, F32),
                  tok(2 * GLA_KW, F32), tok(GQA_KVW, BF16), tok(GQA_KVW, BF16)]
    return pl.pallas_call(
        _inproj_kernel,
        grid=(B, N // tm),
        in_specs=[tspec(D),
                  pl.BlockSpec((1, 1, D), lambda b, i: (b, 0, 0)),
                  pl.BlockSpec((1, 1, D), lambda b, i: (b, 0, 0)),
                  pl.BlockSpec((D, _C_END), lambda b, i: (0, 0)),
                  vec(2 * GLA_KW), vec(LANES), vec(LANES),
                  pl.BlockSpec((tm, LANES), lambda b, i: (i, 0)),
                  pl.BlockSpec((tm, LANES), lambda b, i: (i, 0))],
        out_specs=[tspec(s.shape[-1]) for s in out_shapes],
        out_shape=out_shapes,
        compiler_params=_cparams(2),
        name="inproj",
    )(x, sc, sh, w, zb, qnw, knw, cos, sin)


def _nabias_kernel(rpb_ref, o_ref):
    h = pl.program_id(0)
    qc = lax.broadcasted_iota(jnp.int32, (GRID_W, LANES), 0)
    lane = lax.broadcasted_iota(jnp.int32, (GRID_W, LANES), 1)
    kc = lane & (GRID_W - 1)
    hi = lane >= GRID_W
    idx = jnp.clip(kc - qc, -(NA_WIN_COLS - 1), NA_WIN_COLS - 1) + (NA_WIN_COLS - 1)
    start = jnp.clip(qc - NA_WIN_COLS // 2, 0, GRID_W - NA_WIN_COLS)
    col_in = jnp.logical_and(kc >= start, kc < start + NA_WIN_COLS)
    base = h * (NA_DR * NA_DC)

    pair_tables = []
    for dr0 in range(NA_DR - 1):

        def body(j, t, dr0=dr0):
            s0 = rpb_ref[base + dr0 * NA_DC + j]
            s1 = rpb_ref[base + (dr0 + 1) * NA_DC + j]
            return jnp.where(idx == j, jnp.where(hi, s1, s0), t)

        t = lax.fori_loop(0, NA_DC, body, jnp.zeros((GRID_W, LANES), F32), unroll=True)
        pair_tables.append(jnp.where(col_in, t * LOG2_E, NEG_BIG))
    for oi in range(NA_WIN_ROWS):
        for i in range(NA_WIN_ROWS // 2):
            o_ref[0, oi, :, i * LANES:(i + 1) * LANES] = pair_tables[oi + 2 * i]


def _na_bias(rpb_flat):
    return pl.pallas_call(
        _nabias_kernel,
        grid=(NA_HEADS,),
        in_specs=[pl.BlockSpec(memory_space=pltpu.SMEM)],
        out_specs=pl.BlockSpec((1, NA_WIN_ROWS, GRID_W, NA_WIN_ROWS * GRID_W),
                               lambda h: (h, 0, 0, 0)),
        out_shape=jax.ShapeDtypeStruct((NA_HEADS, NA_WIN_ROWS, GRID_W, NA_WIN_ROWS * GRID_W), F32),
        compiler_params=_cparams(1),
        name="na_bias",
    )(rpb_flat)


def _na_kernel(q_ref, k_ref, v_ref, kc_ref, vc_ref, b_ref, o_ref):
    n = q_ref.shape[1]
    rows = n // GRID_W
    win = NA_WIN_ROWS * GRID_W
    lane = lax.broadcasted_iota(jnp.int32, (GRID_W, LANES), 1)
    lo = lane < HEAD_DIM
    kc = kc_ref[0]
    vc = vc_ref[0]
    zero = jnp.zeros((GRID_W, LANES), BF16)

    def scores(r):
        rs = jnp.clip(r - NA_WIN_ROWS // 2, 0, rows - NA_WIN_ROWS)
        oi = rs - r + (NA_WIN_ROWS - 1)
        q0 = pl.multiple_of(r * GRID_W, GRID_W)
        k0 = pl.multiple_of(rs * GRID_W, GRID_W)
        q = q_ref[0, pl.ds(q0, GRID_W), :]
        kw = k_ref[0, pl.ds(k0, win), :]
        qs = jnp.concatenate([jnp.where(lo, q, zero), jnp.where(lo, zero, q)], axis=0)
        s_lat = _dot_nt(qs, kw) + jnp.concatenate([b_ref[0, oi], b_ref[1, oi]], axis=0)
        s_ctx = _dot_nt(qs, kc)
        return q0, k0, s_lat, s_ctx

    def softmax(st):
        q0, k0, s_lat, s_ctx = st
        m = jnp.maximum(jnp.max(s_lat, axis=-1, keepdims=True),
                        jnp.max(s_ctx, axis=-1, keepdims=True))
        p_lat = jnp.exp2(s_lat - m)
        p_ctx = jnp.exp2(s_ctx - m)
        den = jnp.sum(p_lat, axis=-1, keepdims=True) + jnp.sum(p_ctx, axis=-1, keepdims=True)
        return q0, k0, p_lat.astype(BF16), p_ctx.astype(BF16), den

    def values(st):
        q0, k0, p_lat, p_ctx, den = st
        vw = v_ref[0, pl.ds(k0, win), :]
        o = (_dot(p_lat, vw) + _dot(p_ctx, vc)) / den
        o_ref[0, pl.ds(q0, GRID_W), :] = jnp.where(lo, o[:GRID_W], o[GRID_W:]).astype(BF16)

    per = min(NA_ROWS_PER_STEP, rows)

    def body(i, carry):
        _skewed((scores, softmax, values), [i * per + j for j in range(per)])
        return carry

    lax.fori_loop(0, rows // per, body, 0)


def _na_attention(q, k, v, kc, vc, bias):
    B, N, _ = q.shape
    L = kc.shape[1]
    lat = pl.BlockSpec((1, N, LANES), lambda b, p: (b, 0, p))
    cx = pl.BlockSpec((1, L, LANES), lambda b, p: (b, 0, p))
    return pl.pallas_call(
        _na_kernel,
        grid=(B, NA_PAIRS),
        in_specs=[lat, lat, lat, cx, cx,
                  pl.BlockSpec((2, NA_WIN_ROWS, GRID_W, NA_WIN_ROWS * GRID_W),
                               lambda b, p: (p, 0, 0, 0))],
        out_specs=lat,
        out_shape=jax.ShapeDtypeStruct((B, N, NA_W), BF16),
        compiler_params=_cparams(2),
        name="na_attn",
    )(q, k, v, kc, vc, bias)


def _pair_attn_kernel(*refs, two_sources):
    if two_sources:
        q_ref, ka_ref, va_ref, kb_ref, vb_ref, o_ref = refs
    else:
        q_ref, ka_ref, va_ref, o_ref = refs
    tq = q_ref.shape[1]
    ts = min(tq, PAIR_ATTN_SUB_ROWS)
    lane = lax.broadcasted_iota(jnp.int32, (ts, LANES), 1)
    lo = lane < HEAD_DIM
    zero = jnp.zeros((ts, LANES), BF16)

    def with_ones(v):
        return jnp.concatenate([v, jnp.ones_like(v)], axis=1)

    n_batch = q_ref.shape[0]
    ka = [ka_ref[b] for b in range(n_batch)]
    va = [with_ones(va_ref[b]) for b in range(n_batch)]
    if two_sources:
        kb = [kb_ref[b] for b in range(n_batch)]
        vb = [with_ones(vb_ref[b]) for b in range(n_batch)]

    def scores(item):
        b, sub, h = item
        q = q_ref[b, sub * ts:(sub + 1) * ts, :]
        qm = jnp.where(lo, q, zero) if h == 0 else jnp.where(lo, zero, q)
        s_a = _dot_nt(qm, ka[b])
        s_b = _dot_nt(qm, kb[b]) if two_sources else None
        return item, s_a, s_b

    def softmax(st):
        item, s_a, s_b = st
        m = jnp.max(s_a, axis=-1, keepdims=True)
        if two_sources:
            m = jnp.maximum(m, jnp.max(s_b, axis=-1, keepdims=True))
        p_a = jnp.exp2(s_a - m).astype(BF16)
        p_b = jnp.exp2(s_b - m).astype(BF16) if two_sources else None
        return item, p_a, p_b

    done = {}

    def values(st):
        (b, sub, h), p_a, p_b = st
        o = _dot(p_a, va[b])
        if two_sources:
            o = o + _dot(p_b, vb[b])
        done[b, sub, h] = o[:, :LANES] / o[:, LANES:]
        if h == 1:
            o_ref[b, sub * ts:(sub + 1) * ts, :] = jnp.where(
                lo, done.pop((b, sub, 0)), done.pop((b, sub, 1))).astype(BF16)

    _skewed((scores, softmax, values),
            [(b, sub, h) for b in range(n_batch) for sub in range(tq // ts) for h in range(2)])


def _pair_attention(q, ka, va, kb=None, vb=None, *, shared_kv, tq, batch_block=1):
    B, Nq, W = q.shape
    pairs = W // LANES
    nb = batch_block
    kv_map = (lambda b, p, i: (b, 0, 0)) if shared_kv else (lambda b, p, i: (b, 0, p))
    qspec = pl.BlockSpec((nb, tq, LANES), lambda b, p, i: (b, i, p))
    in_specs = [qspec,
                pl.BlockSpec((nb, ka.shape[1], LANES), kv_map),
                pl.BlockSpec((nb, ka.shape[1], LANES), kv_map)]
    args = [q, ka, va]
    if kb is not None:
        in_specs += [pl.BlockSpec((nb, kb.shape[1], LANES), kv_map),
                     pl.BlockSpec((nb, kb.shape[1], LANES), kv_map)]
        args += [kb, vb]
    return pl.pallas_call(
        functools.partial(_pair_attn_kernel, two_sources=kb is not None),
        grid=(B // nb, pairs, Nq // tq),
        in_specs=in_specs,
        out_specs=qspec,
        out_shape=jax.ShapeDtypeStruct((B, Nq, W), BF16),
        compiler_params=_cparams(3),
        name="pair_attn",
    )(*args)


def _gla_constants():
    c = GLA_CHUNK
    t = np.arange(c)[:, None]
    i = np.arange(c)[None, :]
    mcat = np.zeros((2, (GLA_LEVELS + 1) * c, c), np.float32)
    masks = np.zeros((2, GLA_LEVELS + 2, c, c), np.float32)
    for d in range(2):
        masks[d, GLA_LEVELS + 1] = (i <= t) if d == 0 else (i >= t)
        for l in range(1, GLA_LEVELS + 1):
            m = 2 ** l
            same = (t // m) == (i // m)
            q_side = same & ((i <= t) if d == 0 else (i >= t))
            k_side = same & ((i > t) if d == 0 else (i < t))
            if l < GLA_LEVELS:
                is_q_row = ((t // m) % 2 == 1) if d == 0 else ((t // m) % 2 == 0)
                mcat[d, (l - 1) * c:l * c] = np.where(is_q_row, q_side, k_side)
            else:
                mcat[d, (l - 1) * c:l * c] = q_side
                mcat[d, l * c:(l + 1) * c] = k_side
        masks[d, 0] = (t == i)
        for l in range(GLA_LEVELS):
            m = 2 ** l
            tb, sb = t // m, i // m
            if d == 0:
                masks[d, l + 1] = (tb % 2 == 1) & (sb == tb - 1)
            else:
                masks[d, l + 1] = (tb % 2 == 0) & (sb == tb + 1)
    masks = np.tile(masks, (1, 1, 1, GLA_HEADS))
    rows = np.arange(GLA_VW)[:, None] // GLA_DV
    cols = np.arange(GLA_KW)[None, :] // GLA_DK
    stmask = (rows == cols).astype(np.float32)
    return jnp.asarray(mcat, BF16), jnp.asarray(masks, F32), jnp.asarray(stmask, F32)


def _gla_stages(q_ref, k_ref, v_ref, la_ref, acc_ref, st_ref, mcat_ref, lmask_ref, stmask_ref, mild):
    ck = GLA_CHUNK
    whole = slice((GLA_LEVELS - 1) * ck, (GLA_LEVELS + 1) * ck)
    lane_k = lax.broadcasted_iota(jnp.int32, (1, GLA_KW), 1) // GLA_DK
    lane_v = lax.broadcasted_iota(jnp.int32, (1, GLA_VW), 1) // GLA_DV
    row = lax.broadcasted_iota(jnp.int32, (ck, GLA_KW), 0)

    def block_rows(x, lane_head):
        return jnp.concatenate([jnp.where(lane_head == h, x, 0.0) for h in range(GLA_HEADS)], axis=0)

    def exponents(item):
        c, d = item
        r0 = pl.multiple_of(c * ck, ck)
        g = la_ref[0, pl.ds(r0, ck), d * GLA_KW:(d + 1) * GLA_KW]
        g1 = g.astype(BF16)
        g2 = (g - g1.astype(F32)).astype(BF16)
        mc = mcat_ref[d, whole, :] if mild else mcat_ref[d]
        e = _dot(mc, g1) + _dot(mc, g2)
        return r0, d, g, e

    def in_chunk(st):
        r0, d, g, e = st
        q = q_ref[0, pl.ds(r0, ck), :]
        k = k_ref[0, pl.ds(r0, ck), :]
        q_chunk = e[-2 * ck:-ck]
        k_chunk = e[-ck:]
        last = ck - 1 if d == 0 else 0
        total = q_chunk[last:last + 1]
        q_in = (q * jnp.exp(q_chunk)).astype(BF16)

        def level_scores(qt, kt, idx):
            kb = block_rows(kt, lane_k).astype(BF16)
            return jnp.where(lmask_ref[d, idx] != 0.0, _dot_nt(qt, kb), 0.0)

        if mild:
            a = level_scores(q_in, k * jnp.exp(-q_chunk), GLA_LEVELS + 1)
        else:
            q_row0 = (row & 1) == (1 - d)
            a = level_scores(q.astype(BF16), k, 0)
            for l in range(GLA_LEVELS):
                dec = jnp.exp(jnp.where(q_row0, g, 0.0) if l == 0 else e[(l - 1) * ck:l * ck])
                a = a + level_scores((q * dec).astype(BF16), k * dec, l + 1)
        return r0, d, a.astype(BF16), q_in, (k * jnp.exp(k_chunk)).astype(BF16), jnp.exp(total)

    def state(st):
        r0, d, a, q_in, k_out, decay = st
        v = v_ref[0, pl.ds(r0, ck), :]
        vbd = block_rows(v.astype(F32), lane_v).astype(BF16)
        s_prev = st_ref[d]
        acc_ref[d, pl.ds(r0, ck), :] = _dot(a, vbd) + _dot_nt(q_in, s_prev.astype(BF16))
        st_ref[d] = s_prev * decay + stmask_ref[...] * _dot_tn(v, k_out)

    return exponents, in_chunk, state


def _gla_finish(acc_ref, g_ref, nw_ref, o_ref, tile):
    n = acc_ref.shape[1]
    lane = lax.broadcasted_iota(jnp.int32, (tile, LANES), 1)
    lo = lane < GLA_DV
    nw = nw_ref[...]

    def body(i, carry):
        r0 = pl.multiple_of(i * tile, tile)
        for j in range(GLA_VW // LANES):
            cols = slice(j * LANES, (j + 1) * LANES)
            o = acc_ref[0, pl.ds(r0, tile), cols] + acc_ref[1, pl.ds(r0, tile), cols]
            on = o * lax.rsqrt(_half_mean_square(o, lo) + RMS_EPS) * nw
            gate = _silu(g_ref[0, pl.ds(r0, tile), cols])
            o_ref[0, pl.ds(r0, tile), cols] = (on * gate).astype(BF16)
        return carry

    lax.fori_loop(0, n // tile, body, 0)


def _gla_kernel(q_ref, k_ref, v_ref, la_ref, g_ref, qc_ref, kc_ref, vc_ref, lac_ref, gc_ref,
                nw_ref, mcat_ref, lmask_ref, stmask_ref, o_ref, oc_ref, acc_ref, accc_ref, st_ref):
    n_chunks = q_ref.shape[1] // GLA_CHUNK
    c_chunks = qc_ref.shape[1] // GLA_CHUNK
    consts = (mcat_ref, lmask_ref, stmask_ref)
    st_ref[...] = jnp.zeros_like(st_ref)

    def min_chunk_sum(ref):
        la = ref[0]
        sums = jnp.sum(la.reshape(la.shape[0] // GLA_CHUNK, GLA_CHUNK, la.shape[1]), axis=1)
        return jnp.min(sums)

    mild = jnp.minimum(min_chunk_sum(la_ref), min_chunk_sum(lac_ref)) >= -GLA_MILD_DECAY

    def steps(refs, acc, n, is_mild):
        stages = _gla_stages(*refs, acc, st_ref, *consts, is_mild)

        per = min(GLA_CHUNKS_PER_STEP, n)

        def body(i, carry):
            items = []
            for j in range(per):
                c = i * per + j
                items += [(c, 0), (n - 1 - c, 1)]
            _skewed(stages, items)
            return carry

        lax.fori_loop(0, n // per, body, 0)

    for is_mild in (True, False):
        @pl.when(mild if is_mild else jnp.logical_not(mild))
        def _():
            steps((qc_ref, kc_ref, vc_ref, lac_ref), accc_ref, c_chunks, is_mild)
            steps((q_ref, k_ref, v_ref, la_ref), acc_ref, n_chunks, is_mild)

    _gla_finish(acc_ref, g_ref, nw_ref, o_ref, 256)
    _gla_finish(accc_ref, gc_ref, nw_ref, oc_ref, 256)


def _gla(q, k, v, la, g, qc, kc, vc, lac, gc, nw, consts):
    B, N, _ = q.shape
    L = qc.shape[1]
    mcat, lmask, stmask = consts
    tok = lambda n, w: pl.BlockSpec((1, n, w), lambda b: (b, 0, 0))
    full = lambda a: pl.BlockSpec(a.shape, lambda b: (0,) * a.ndim)
    return pl.pallas_call(
        _gla_kernel,
        grid=(B,),
        in_specs=[tok(N, GLA_KW), tok(N, GLA_KW), tok(N, GLA_VW), tok(N, 2 * GLA_KW), tok(N, GLA_VW),
                  tok(L, GLA_KW), tok(L, GLA_KW), tok(L, GLA_VW), tok(L, 2 * GLA_KW), tok(L, GLA_VW),
                  full(nw), full(mcat), full(lmask), full(stmask)],
        out_specs=[tok(N, GLA_VW), tok(L, GLA_VW)],
        out_shape=[jax.ShapeDtypeStruct((B, N, GLA_VW), BF16), jax.ShapeDtypeStruct((B, L, GLA_VW), BF16)],
        scratch_shapes=[pltpu.VMEM((2, N, GLA_VW), F32), pltpu.VMEM((2, L, GLA_VW), F32),
                        pltpu.VMEM((2, GLA_VW, GLA_KW), F32)],
        compiler_params=_cparams(1),
        name="gla",
    )(q, k, v, la, g, qc, kc, vc, lac, gc, nw, mcat, lmask, stmask)


_FFN_CHUNKS = ((0, 768), (768, 1536), (1536, 2304), (2304, FFN_HIDDEN))


def _mix_ffn_kernel(x_ref, na_ref, gl_ref, ga_ref, wm_ref, g1_ref, l1g_ref, l1b_ref,
                    sc_ref, sh_ref, g2_ref, wi_ref, wo_ref, l2g_ref, l2b_ref, o_ref):
    ts = x_ref.shape[1] // FFN_SUBTILES
    x1_of, u_of, acc_of = {}, {}, {}

    def rows(sub):
        return slice(sub * ts, (sub + 1) * ts)

    def mixer_sublayer(sub):
        r = rows(sub)
        o = jnp.concatenate([na_ref[0, r, :], gl_ref[0, r, :], ga_ref[0, r, :]], axis=-1)
        y = DEEPNORM_ALPHA * x_ref[0, r, :] + g1_ref[0] * _dot(o, wm_ref[...])
        x1 = _layer_norm(y, l1g_ref[...], l1b_ref[...])
        x1_of[sub] = x1
        u_of[sub] = (x1 * (1.0 + sc_ref[0]) + sh_ref[0]).astype(BF16)

    def up(item):
        sub, (a0, a1) = item
        if sub not in u_of:
            mixer_sublayer(sub)
        u = u_of[sub]
        return item, _dot(u, wi_ref[:, a0:a1]), _dot(u, wi_ref[:, FFN_HIDDEN + a0:FFN_HIDDEN + a1])

    def gate(st):
        item, ha, hb = st
        return item, (_silu(ha) * hb).astype(BF16)

    def down(st):
        (sub, (a0, a1)), t = st
        part = _dot(t, wo_ref[a0:a1, :])
        acc_of[sub] = acc_of[sub] + part if sub in acc_of else part
        if a1 == FFN_HIDDEN:
            y = DEEPNORM_ALPHA * x1_of.pop(sub) + g2_ref[0] * acc_of.pop(sub)
            o_ref[0, rows(sub), :] = _layer_norm(y, l2g_ref[...], l2b_ref[...])

    _skewed((up, gate, down), [(sub, ch) for sub in range(FFN_SUBTILES) for ch in _FFN_CHUNKS])


def _mix_ffn(x, o_na, o_gl, o_ga, wm, g1, l1g, l1b, sc, sh, g2, wi, wo, l2g, l2b, tm):
    B, N, D = x.shape
    tspec = lambda width: pl.BlockSpec((1, tm, width), lambda b, i: (b, i, 0))
    mod = pl.BlockSpec((1, 1, D), lambda b, i: (b, 0, 0))
    vec = pl.BlockSpec((1, D), lambda b, i: (0, 0))
    resident = lambda a: pl.BlockSpec(a.shape, lambda b, i: (0, 0), pipeline_mode=pl.Buffered(1))
    return pl.pallas_call(
        _mix_ffn_kernel,
        grid=(B, N // tm),
        in_specs=[tspec(D), tspec(NA_W), tspec(GLA_VW), tspec(GQA_QW), resident(wm), mod, vec, vec,
                  mod, mod, mod, resident(wi), resident(wo), vec, vec],
        out_specs=tspec(D),
        out_shape=jax.ShapeDtypeStruct((B, N, D), F32),
        compiler_params=_cparams(2),
        name="mix_ffn",
    )(x, o_na, o_gl, o_ga, wm, g1, l1g, l1b, sc, sh, g2, wi, wo, l2g, l2b)


def _rope_tables(n):
    t = jnp.arange(n)
    row = (t // GRID_W).astype(F32)
    col = (t % GRID_W).astype(F32)
    inv_freq = ROPE_THETA ** (-jnp.arange(ROPE_AXIS_PAIRS, dtype=F32) / ROPE_AXIS_PAIRS)
    ang_r = row[:, None] * inv_freq
    ang_c = col[:, None] * inv_freq
    ang = jnp.concatenate([ang_r, ang_r, ang_c, ang_c], axis=-1)
    sign = jnp.where((jnp.arange(HEAD_DIM) % 32) < 16, -1.0, 1.0).astype(F32)
    cos = jnp.tile(jnp.cos(ang), (1, 2))
    sin = jnp.tile(jnp.sin(ang) * sign, (1, 2))
    return cos, sin


def _pair_major(w, axis):
    shape = w.shape
    lead, tail = shape[:axis], shape[axis + 1:]
    w = w.reshape(lead + (GQA_KV_HEADS, GQA_REP, HEAD_DIM) + tail)
    w = jnp.swapaxes(w, axis, axis + 1)
    return w.reshape(shape)


def kernel(x, c, ctx, c_ctx, w_ada, b_ada, w_in, na_rpb, gla_wa2, gla_ba, gla_norm_w, gqa_qnorm_w,
           gqa_knorm_w, w_out, ln1_g, ln1_b, w_ffn_in, w_ffn_out, ln2_g, ln2_b):
    B, N, D = x.shape
    L = ctx.shape[1]
    depth = w_in.shape[0]
    tm = min(N, ROW_TILE)
    tmc = min(B * L, ROW_TILE)
    assert D == D_MODEL and N % tm == 0 and (B * L) % tmc == 0
    assert N % (GRID_W * NA_WIN_ROWS) == 0 and N % GLA_CHUNK == 0 and L % GLA_CHUNK == 0

    pad = (-(B + 1)) % 8
    cvec = jnp.concatenate([c, c_ctx[None, :], jnp.zeros((pad, D), F32)], axis=0)
    mods = _ada(cvec, w_ada, b_ada)

    w_lr = w_in[:, :, _O_GLLR:_O_GAQ].reshape(depth, D, 2, GLA_RANK).transpose(0, 2, 1, 3)
    w_z = _fold_gate_weights(w_lr, gla_wa2)

    cos, sin = _rope_tables(N)
    cos_c = jnp.ones((B * L, LANES), F32)
    sin_c = jnp.zeros((B * L, LANES), F32)
    gla_consts = _gla_constants()
    per_batch = lambda t: t.reshape(B, L, t.shape[-1])
    flat = lambda t: t.reshape(1, B * L, t.shape[-1])

    xc = flat(ctx)
    for l in range(depth):
        ctx_out = l < depth - 1
        m_lat = mods[l, :B].reshape(B, 6, 1, D)
        m_ctx = mods[l, B].reshape(1, 6, 1, D)
        sh1, sc1, g1, sh2, sc2, g2 = (m_lat[:, i] for i in range(6))
        sh1c, sc1c, g1c, sh2c, sc2c, g2c = (m_ctx[:, i] for i in range(6))

        wl = w_in[l]
        w_proj = jnp.concatenate(
            [wl[:, _O_NAQ:_O_GLQ], _pair_major(wl[:, _O_GAQ:_O_GAK], 1), wl[:, _O_GLQ:_O_GLLR],
             w_z[l], wl[:, _O_GAK:_O_END]], axis=1).astype(BF16)
        zb = gla_ba[l].reshape(1, 2 * GLA_KW)
        qnw = jnp.tile(gqa_qnorm_w[l], 2).reshape(1, LANES)
        knw = jnp.tile(gqa_knorm_w[l], 2).reshape(1, LANES)
        glnw = jnp.tile(gla_norm_w[l], 2).reshape(1, LANES)
        wo_l = w_out[l]
        w_o = jnp.concatenate([wo_l[:NA_W + GLA_VW], _pair_major(wo_l[NA_W + GLA_VW:], 0)], axis=0).astype(BF16)
        lg1, lb1 = ln1_g[l].reshape(1, D), ln1_b[l].reshape(1, D)
        lg2, lb2 = ln2_g[l].reshape(1, D), ln2_b[l].reshape(1, D)
        wi = w_ffn_in[l].astype(BF16)
        wo = w_ffn_out[l].astype(BF16)

        (na_q, na_k, na_v, ga_q, gl_q, gl_k, gl_v, gl_g, gl_la, ga_k, ga_v) = _inproj(
            x, sc1, sh1, w_proj, zb, qnw, knw, cos, sin, tm)
        (na_qc, na_kc, na_vc, ga_qc, gl_qc, gl_kc, gl_vc, gl_gc, gl_lac, ga_kc, ga_vc) = map(
            per_batch, _inproj(xc, sc1c, sh1c, w_proj, zb, qnw, knw, cos_c, sin_c, tmc))

        bias = _na_bias(na_rpb[l].reshape(-1))
        o_na = _na_attention(na_q, na_k, na_v, na_kc, na_vc, bias)
        o_gl, oc_gl = _gla(gl_q, gl_k, gl_v, gl_la, gl_g, gl_qc, gl_kc, gl_vc, gl_lac, gl_gc, glnw, gla_consts)
        o_ga = _pair_attention(ga_q, ga_k, ga_v, ga_kc, ga_vc, shared_kv=True, tq=min(N, GQA_Q_TILE))

        x = _mix_ffn(x, o_na, o_gl, o_ga, w_o, g1, lg1, lb1, sc2, sh2, g2, wi, wo, lg2, lb2, tm)

        if ctx_out:
            nbc = math.gcd(B, CTX_ATTN_BATCH_BLOCK)
            oc_na = _pair_attention(na_qc, na_kc, na_vc, shared_kv=False, tq=L, batch_block=nbc)
            oc_ga = _pair_attention(ga_qc, ga_kc, ga_vc, shared_kv=True, tq=L, batch_block=nbc)
            xc = _mix_ffn(xc, flat(oc_na), flat(oc_gl), flat(oc_ga), w_o, g1c, lg1, lb1, sc2c, sh2c, g2c,
                          wi, wo, lg2, lb2, tmc)
    return x
```

```python
import functools
import math

import numpy as np
import jax
import jax.numpy as jnp
from jax import lax
from jax.experimental import pallas as pl
from jax.experimental.pallas import tpu as pltpu

F32 = jnp.float32
BF16 = jnp.bfloat16

D_MODEL = 1024
DEPTH = 2
GRID_W = 64
HEAD_DIM = 64
LANES = 128

NA_HEADS = 6
NA_WIN_ROWS = 8
NA_WIN_COLS = 16
NA_W = NA_HEADS * HEAD_DIM
NA_PAIRS = NA_W // LANES
NA_DR = 2 * NA_WIN_ROWS - 1
NA_DC = 2 * NA_WIN_COLS - 1
NA_ROWS_PER_STEP = 32

GLA_HEADS = 4
GLA_DK = 32
GLA_DV = 64
GLA_RANK = 16
GLA_GATE_NORM = 16.0
GLA_CHUNK = 64
GLA_KW = GLA_HEADS * GLA_DK
GLA_VW = GLA_HEADS * GLA_DV
GLA_LEVELS = 6
GLA_MILD_DECAY = 40.0
GLA_CHUNKS_PER_STEP = 8

GQA_Q_HEADS = 6
GQA_KV_HEADS = 2
GQA_REP = GQA_Q_HEADS // GQA_KV_HEADS
GQA_QW = GQA_Q_HEADS * HEAD_DIM
GQA_KVW = GQA_KV_HEADS * HEAD_DIM
GQA_PAIRS = GQA_QW // LANES
ROW_TILE = 1024
GQA_Q_TILE = 2048
CTX_ATTN_BATCH_BLOCK = 4
INPROJ_SUBTILES = 4
FFN_SUBTILES = 2
PAIR_ATTN_SUB_ROWS = 128
ROPE_THETA = 10000.0
ROPE_AXIS_PAIRS = HEAD_DIM // 4

FFN_HIDDEN = 2816
MIX_W = NA_W + GLA_VW + GQA_QW

DEEPNORM_ALPHA = (2.0 * DEPTH) ** 0.25
LN_EPS = 1e-5
RMS_EPS = 1e-6
NEG_BIG = -1e30
LOG2_E = 1.4426950408889634
Q_SCALE = HEAD_DIM ** -0.5 * LOG2_E

_O_NAQ, _O_NAK, _O_NAV = 0, 384, 768
_O_GLQ, _O_GLK, _O_GLV, _O_GLG, _O_GLLR = 1152, 1280, 1408, 1664, 1920
_O_GAQ, _O_GAK, _O_GAV, _O_END = 1952, 2336, 2464, 2592
_C_NAQ, _C_NAK, _C_NAV, _C_GAQ = 0, 384, 768, 1152
_C_GLQ, _C_GLK, _C_GLV, _C_GLG, _C_Z = 1536, 1664, 1792, 2048, 2304
_C_GAK, _C_GAV, _C_END = 2560, 2688, 2816

VMEM_LIMIT = 48 * 1024 * 1024


def _cparams(n_axes):
    return pltpu.CompilerParams(dimension_semantics=("arbitrary",) * n_axes,
                                vmem_limit_bytes=VMEM_LIMIT)


def _dot(a, b):
    return jnp.dot(a, b, preferred_element_type=F32)


def _dot_nt(a, b):
    return lax.dot_general(a, b, (((1,), (1,)), ((), ())), preferred_element_type=F32)


def _dot_tn(a, b):
    return lax.dot_general(a, b, (((0,), (0,)), ((), ())), preferred_element_type=F32)


def _silu(x):
    return x * jax.nn.sigmoid(x)


def _layer_norm(y, g, b):
    mu = jnp.mean(y, axis=-1, keepdims=True)
    d = y - mu
    var = jnp.mean(d * d, axis=-1, keepdims=True)
    return d * lax.rsqrt(var + LN_EPS) * g + b


def _skewed(stages, items):
    n, k = len(items), len(stages)
    live = {}
    for j in range(n + k - 1):
        for s in range(k):
            idx = j - s
            if 0 <= idx < n:
                live[s, idx] = stages[s](items[idx] if s == 0 else live.pop((s - 1, idx)))


def _half_mean_square(y, lo):
    s = y * y
    s_lo = jnp.sum(jnp.where(lo, s, 0.0), axis=-1, keepdims=True)
    s_hi = jnp.sum(jnp.where(lo, 0.0, s), axis=-1, keepdims=True)
    return jnp.where(lo, s_lo, s_hi) * (1.0 / HEAD_DIM)


_ADA_TN = 1024


def _ada_kernel(c_ref, w_ref, b_ref, o_ref):
    s = _silu(c_ref[...])
    o_ref[0] = jnp.dot(s, w_ref[0], precision=lax.Precision.HIGHEST,
                       preferred_element_type=F32) + b_ref[0]


def _ada(cvec, w_ada, b_ada):
    rows = cvec.shape[0]
    depth, d, n6 = w_ada.shape
    return pl.pallas_call(
        _ada_kernel,
        grid=(depth, n6 // _ADA_TN),
        in_specs=[pl.BlockSpec((rows, d), lambda l, j: (0, 0)),
                  pl.BlockSpec((1, d, _ADA_TN), lambda l, j: (l, 0, j)),
                  pl.BlockSpec((1, 1, _ADA_TN), lambda l, j: (l, 0, j))],
        out_specs=pl.BlockSpec((1, rows, _ADA_TN), lambda l, j: (l, 0, j)),
        out_shape=jax.ShapeDtypeStruct((depth, rows, n6), F32),
        compiler_params=_cparams(2),
        name="ada",
    )(cvec, w_ada, b_ada.reshape(depth, 1, n6))


def _fold_kernel(wlr_ref, wa2_ref, o_ref):
    for e in range(2):
        o_ref[0, :, e * GLA_KW:(e + 1) * GLA_KW] = jnp.dot(
            wlr_ref[0, e], wa2_ref[0, e], precision=lax.Precision.HIGHEST,
            preferred_element_type=F32)


def _fold_gate_weights(w_lr, wa2):
    depth = w_lr.shape[0]
    return pl.pallas_call(
        _fold_kernel,
        grid=(depth,),
        in_specs=[pl.BlockSpec((1, 2, D_MODEL, GLA_RANK), lambda l: (l, 0, 0, 0)),
                  pl.BlockSpec((1, 2, GLA_RANK, GLA_KW), lambda l: (l, 0, 0, 0))],
        out_specs=pl.BlockSpec((1, D_MODEL, 2 * GLA_KW), lambda l: (l, 0, 0)),
        out_shape=jax.ShapeDtypeStruct((depth, D_MODEL, 2 * GLA_KW), F32),
        compiler_params=_cparams(1),
        name="fold_gate",
    )(w_lr, wa2)


def _inproj_kernel(x_ref, sc_ref, sh_ref, w_ref, zb_ref, qnw_ref, knw_ref, cos_ref, sin_ref,
                   naq_ref, nak_ref, nav_ref, gaq_ref, glq_ref, glk_ref, glv_ref, glg_ref,
                   gla_ref, gak_ref, gav_ref):
    ts = x_ref.shape[1] // INPROJ_SUBTILES
    lane = lax.broadcasted_iota(jnp.int32, (ts, LANES), 1)
    lo = lane < HEAD_DIM
    first16 = (lane & 31) < 16
    u_of = {}

    def rows(sub):
        return slice(sub * ts, (sub + 1) * ts)

    def matmul(item):
        sub, (a, b, epilogue) = item
        if sub not in u_of:
            u_of[sub] = (x_ref[0, rows(sub), :] * (1.0 + sc_ref[0]) + sh_ref[0]).astype(BF16)
        return sub, epilogue, _dot(u_of[sub], w_ref[:, a:b])

    def norm_rope(y, w, r):
        yn = y * lax.rsqrt(_half_mean_square(y, lo) + RMS_EPS) * w
        rot = jnp.where(first16, pltpu.roll(yn, LANES - 16, 1), pltpu.roll(yn, 16, 1))
        return yn * cos_ref[r, :] + rot * sin_ref[r, :]

    def na_qk(r, y):
        naq_ref[0, r, :] = (y[:, :NA_W] * Q_SCALE).astype(BF16)
        nak_ref[0, r, :] = y[:, NA_W:].astype(BF16)

    def na_v_gqa_q(r, y):
        nav_ref[0, r, :] = y[:, :NA_W].astype(BF16)
        qnw = qnw_ref[...]
        for p in range(GQA_PAIRS):
            t = norm_rope(y[:, NA_W + p * LANES:NA_W + (p + 1) * LANES], qnw, r)
            gaq_ref[0, r, p * LANES:(p + 1) * LANES] = (t * Q_SCALE).astype(BF16)

    def gla_qkv(r, y):
        glq_ref[0, r, :] = y[:, :GLA_KW] * GLA_DK ** -0.5
        glk_ref[0, r, :] = y[:, GLA_KW:2 * GLA_KW]
        glv_ref[0, r, :] = y[:, 2 * GLA_KW:].astype(BF16)

    def gla_gates(r, y):
        glg_ref[0, r, :] = y[:, :GLA_VW]
        z = y[:, GLA_VW:] + zb_ref[...]
        log_sig = jnp.minimum(z, 0.0) - jnp.log1p(jnp.exp(-jnp.abs(z)))
        gla_ref[0, r, :] = log_sig * (1.0 / GLA_GATE_NORM)

    def gqa_kv(r, y):
        gak_ref[0, r, :] = norm_rope(y[:, :LANES], knw_ref[...], r).astype(BF16)
        gav_ref[0, r, :] = y[:, LANES:].astype(BF16)

    segments = [(_C_NAQ, _C_NAV, na_qk), (_C_NAV, _C_GLQ, na_v_gqa_q), (_C_GLQ, _C_GLG, gla_qkv),
                (_C_GLG, _C_GAK, gla_gates), (_C_GAK, _C_END, gqa_kv)]
    _skewed((matmul, lambda st: st[1](rows(st[0]), st[2])),
            [(sub, sg) for sub in range(INPROJ_SUBTILES) for sg in segments])


def _inproj(x, sc, sh, w, zb, qnw, knw, cos, sin, tm):
    B, N, D = x.shape
    tok = lambda width, dt: jax.ShapeDtypeStruct((B, N, width), dt)
    tspec = lambda width: pl.BlockSpec((1, tm, width), lambda b, i: (b, i, 0))
    vec = lambda width: pl.BlockSpec((1, width), lambda b, i: (0, 0))
    out_shapes = [tok(NA_W, BF16), tok(NA_W, BF16), tok(NA_W, BF16), tok(GQA_QW, BF16),
                  tok(GLA_KW, F32), tok(GLA_KW, F32), tok(GLA_VW, BF16), tok(GLA_VW, F32),
                  tok(2 * GLA_KW, F32), tok(GQA_KVW, BF16), tok(GQA_KVW, BF16)]
    return pl.pallas_call(
        _inproj_kernel,
        grid=(B, N // tm),
        in_specs=[tspec(D),
                  pl.BlockSpec((1, 1, D), lambda b, i: (b, 0, 0)),
                  pl.BlockSpec((1, 1, D), lambda b, i: (b, 0, 0)),
                  pl.BlockSpec((D, _C_END), lambda b, i: (0, 0)),
                  vec(2 * GLA_KW), vec(LANES), vec(LANES),
                  pl.BlockSpec((tm, LANES), lambda b, i: (i, 0)),
                  pl.BlockSpec((tm, LANES), lambda b, i: (i, 0))],
        out_specs=[tspec(s.shape[-1]) for s in out_shapes],
        out_shape=out_shapes,
        compiler_params=_cparams(2),
        name="inproj",
    )(x, sc, sh, w, zb, qnw, knw, cos, sin)


def _nabias_kernel(rpb_ref, o_ref):
    h = pl.program_id(0)
    qc = lax.broadcasted_iota(jnp.int32, (GRID_W, LANES), 0)
    lane = lax.broadcasted_iota(jnp.int32, (GRID_W, LANES), 1)
    kc = lane & (GRID_W - 1)
    hi = lane >= GRID_W
    idx = jnp.clip(kc - qc, -(NA_WIN_COLS - 1), NA_WIN_COLS - 1) + (NA_WIN_COLS - 1)
    start = jnp.clip(qc - NA_WIN_COLS // 2, 0, GRID_W - NA_WIN_COLS)
    col_in = jnp.logical_and(kc >= start, kc < start + NA_WIN_COLS)
    base = h * (NA_DR * NA_DC)

    pair_tables = []
    for dr0 in range(NA_DR - 1):

        def body(j, t, dr0=dr0):
            s0 = rpb_ref[base + dr0 * NA_DC + j]
            s1 = rpb_ref[base + (dr0 + 1) * NA_DC + j]
            return jnp.where(idx == j, jnp.where(hi, s1, s0), t)

        t = lax.fori_loop(0, NA_DC, body, jnp.zeros((GRID_W, LANES), F32), unroll=True)
        pair_tables.append(jnp.where(col_in, t * LOG2_E, NEG_BIG))
    for oi in range(NA_WIN_ROWS):
        for i in range(NA_WIN_ROWS // 2):
            o_ref[0, oi, :, i * LANES:(i + 1) * LANES] = pair_tables[oi + 2 * i]


def _na_bias(rpb_flat):
    return pl.pallas_call(
        _nabias_kernel,
        grid=(NA_HEADS,),
        in_specs=[pl.BlockSpec(memory_space=pltpu.SMEM)],
        out_specs=pl.BlockSpec((1, NA_WIN_ROWS, GRID_W, NA_WIN_ROWS * GRID_W),
                               lambda h: (h, 0, 0, 0)),
        out_shape=jax.ShapeDtypeStruct((NA_HEADS, NA_WIN_ROWS, GRID_W, NA_WIN_ROWS * GRID_W), F32),
        compiler_params=_cparams(1),
        name="na_bias",
    )(rpb_flat)


def _na_kernel(q_ref, k_ref, v_ref, kc_ref, vc_ref, b_ref, o_ref):
    n = q_ref.shape[1]
    rows = n // GRID_W
    win = NA_WIN_ROWS * GRID_W
    lane = lax.broadcasted_iota(jnp.int32, (GRID_W, LANES), 1)
    lo = lane < HEAD_DIM
    kc = kc_ref[0]
    vc = vc_ref[0]
    zero = jnp.zeros((GRID_W, LANES), BF16)

    def scores(r):
        rs = jnp.clip(r - NA_WIN_ROWS // 2, 0, rows - NA_WIN_ROWS)
        oi = rs - r + (NA_WIN_ROWS - 1)
        q0 = pl.multiple_of(r * GRID_W, GRID_W)
        k0 = pl.multiple_of(rs * GRID_W, GRID_W)
        q = q_ref[0, pl.ds(q0, GRID_W), :]
        kw = k_ref[0, pl.ds(k0, win), :]
        qs = jnp.concatenate([jnp.where(lo, q, zero), jnp.where(lo, zero, q)], axis=0)
        s_lat = _dot_nt(qs, kw) + jnp.concatenate([b_ref[0, oi], b_ref[1, oi]], axis=0)
        s_ctx = _dot_nt(qs, kc)
        return q0, k0, s_lat, s_ctx

    def softmax(st):
        q0, k0, s_lat, s_ctx = st
        m = jnp.maximum(jnp.max(s_lat, axis=-1, keepdims=True),
                        jnp.max(s_ctx, axis=-1, keepdims=True))
        p_lat = jnp.exp2(s_lat - m)
        p_ctx = jnp.exp2(s_ctx - m)
        den = jnp.sum(p_lat, axis=-1, keepdims=True) + jnp.sum(p_ctx, axis=-1, keepdims=True)
        return q0, k0, p_lat.astype(BF16), p_ctx.astype(BF16), den

    def values(st):
        q0, k0, p_lat, p_ctx, den = st
        vw = v_ref[0, pl.ds(k0, win), :]
        o = (_dot(p_lat, vw) + _dot(p_ctx, vc)) / den
        o_ref[0, pl.ds(q0, GRID_W), :] = jnp.where(lo, o[:GRID_W], o[GRID_W:]).astype(BF16)

    per = min(NA_ROWS_PER_STEP, rows)

    def body(i, carry):
        _skewed((scores, softmax, values), [i * per + j for j in range(per)])
        return carry

    lax.fori_loop(0, rows // per, body, 0)


def _na_attention(q, k, v, kc, vc, bias):
    B, N, _ = q.shape
    L = kc.shape[1]
    lat = pl.BlockSpec((1, N, LANES), lambda b, p: (b, 0, p))
    cx = pl.BlockSpec((1, L, LANES), lambda b, p: (b, 0, p))
    return pl.pallas_call(
        _na_kernel,
        grid=(B, NA_PAIRS),
        in_specs=[lat, lat, lat, cx, cx,
                  pl.BlockSpec((2, NA_WIN_ROWS, GRID_W, NA_WIN_ROWS * GRID_W),
                               lambda b, p: (p, 0, 0, 0))],
        out_specs=lat,
        out_shape=jax.ShapeDtypeStruct((B, N, NA_W), BF16),
        compiler_params=_cparams(2),
        name="na_attn",
    )(q, k, v, kc, vc, bias)


def _pair_attn_kernel(*refs, two_sources):
    if two_sources:
        q_ref, ka_ref, va_ref, kb_ref, vb_ref, o_ref = refs
    else:
        q_ref, ka_ref, va_ref, o_ref = refs
    tq = q_ref.shape[1]
    ts = min(tq, PAIR_ATTN_SUB_ROWS)
    lane = lax.broadcasted_iota(jnp.int32, (ts, LANES), 1)
    lo = lane < HEAD_DIM
    zero = jnp.zeros((ts, LANES), BF16)

    def with_ones(v):
        return jnp.concatenate([v, jnp.ones_like(v)], axis=1)

    n_batch = q_ref.shape[0]
    ka = [ka_ref[b] for b in range(n_batch)]
    va = [with_ones(va_ref[b]) for b in range(n_batch)]
    if two_sources:
        kb = [kb_ref[b] for b in range(n_batch)]
        vb = [with_ones(vb_ref[b]) for b in range(n_batch)]

    def scores(item):
        b, sub, h = item
        q = q_ref[b, sub * ts:(sub + 1) * ts, :]
        qm = jnp.where(lo, q, zero) if h == 0 else jnp.where(lo, zero, q)
        s_a = _dot_nt(qm, ka[b])
        s_b = _dot_nt(qm, kb[b]) if two_sources else None
        return item, s_a, s_b

    def softmax(st):
        item, s_a, s_b = st
        m = jnp.max(s_a, axis=-1, keepdims=True)
        if two_sources:
            m = jnp.maximum(m, jnp.max(s_b, axis=-1, keepdims=True))
        p_a = jnp.exp2(s_a - m).astype(BF16)
        p_b = jnp.exp2(s_b - m).astype(BF16) if two_sources else None
        return item, p_a, p_b

    done = {}

    def values(st):
        (b, sub, h), p_a, p_b = st
        o = _dot(p_a, va[b])
        if two_sources:
            o = o + _dot(p_b, vb[b])
        done[b, sub, h] = o[:, :LANES] / o[:, LANES:]
        if h == 1:
            o_ref[b, sub * ts:(sub + 1) * ts, :] = jnp.where(
                lo, done.pop((b, sub, 0)), done.pop((b, sub, 1))).astype(BF16)

    _skewed((scores, softmax, values),
            [(b, sub, h) for b in range(n_batch) for sub in range(tq // ts) for h in range(2)])


def _pair_attention(q, ka, va, kb=None, vb=None, *, shared_kv, tq, batch_block=1):
    B, Nq, W = q.shape
    pairs = W // LANES
    nb = batch_block
    kv_map = (lambda b, p, i: (b, 0, 0)) if shared_kv else (lambda b, p, i: (b, 0, p))
    qspec = pl.BlockSpec((nb, tq, LANES), lambda b, p, i: (b, i, p))
    in_specs = [qspec,
                pl.BlockSpec((nb, ka.shape[1], LANES), kv_map),
                pl.BlockSpec((nb, ka.shape[1], LANES), kv_map)]
    args = [q, ka, va]
    if kb is not None:
        in_specs += [pl.BlockSpec((nb, kb.shape[1], LANES), kv_map),
                     pl.BlockSpec((nb, kb.shape[1], LANES), kv_map)]
        args += [kb, vb]
    return pl.pallas_call(
        functools.partial(_pair_attn_kernel, two_sources=kb is not None),
        grid=(B // nb, pairs, Nq // tq),
        in_specs=in_specs,
        out_specs=qspec,
        out_shape=jax.ShapeDtypeStruct((B, Nq, W), BF16),
        compiler_params=_cparams(3),
        name="pair_attn",
    )(*args)


def _gla_constants():
    c = GLA_CHUNK
    t = np.arange(c)[:, None]
    i = np.arange(c)[None, :]
    mcat = np.zeros((2, (GLA_LEVELS + 1) * c, c), np.float32)
    masks = np.zeros((2, GLA_LEVELS + 2, c, c), np.float32)
    for d in range(2):
        masks[d, GLA_LEVELS + 1] = (i <= t) if d == 0 else (i >= t)
        for l in range(1, GLA_LEVELS + 1):
            m = 2 ** l
            same = (t // m) == (i // m)
            q_side = same & ((i <= t) if d == 0 else (i >= t))
            k_side = same & ((i > t) if d == 0 else (i < t))
            if l < GLA_LEVELS:
                is_q_row = ((t // m) % 2 == 1) if d == 0 else ((t // m) % 2 == 0)
                mcat[d, (l - 1) * c:l * c] = np.where(is_q_row, q_side, k_side)
            else:
                mcat[d, (l - 1) * c:l * c] = q_side
                mcat[d, l * c:(l + 1) * c] = k_side
        masks[d, 0] = (t == i)
        for l in range(GLA_LEVELS):
            m = 2 ** l
            tb, sb = t // m, i // m
            if d == 0:
                masks[d, l + 1] = (tb % 2 == 1) & (sb == tb - 1)
            else:
                masks[d, l + 1] = (tb % 2 == 0) & (sb == tb + 1)
    masks = np.tile(masks, (1, 1, 1, GLA_HEADS))
    rows = np.arange(GLA_VW)[:, None] // GLA_DV
    cols = np.arange(GLA_KW)[None, :] // GLA_DK
    stmask = (rows == cols).astype(np.float32)
    return jnp.asarray(mcat, BF16), jnp.asarray(masks, F32), jnp.asarray(stmask, F32)


def _gla_stages(q_ref, k_ref, v_ref, la_ref, acc_ref, st_ref, mcat_ref, lmask_ref, stmask_ref, mild):
    ck = GLA_CHUNK
    whole = slice((GLA_LEVELS - 1) * ck, (GLA_LEVELS + 1) * ck)
    lane_k = lax.broadcasted_iota(jnp.int32, (1, GLA_KW), 1) // GLA_DK
    lane_v = lax.broadcasted_iota(jnp.int32, (1, GLA_VW), 1) // GLA_DV
    row = lax.broadcasted_iota(jnp.int32, (ck, GLA_KW), 0)

    def block_rows(x, lane_head):
        return jnp.concatenate([jnp.where(lane_head == h, x, 0.0) for h in range(GLA_HEADS)], axis=0)

    def exponents(item):
        c, d = item
        r0 = pl.multiple_of(c * ck, ck)
        g = la_ref[0, pl.ds(r0, ck), d * GLA_KW:(d + 1) * GLA_KW]
        g1 = g.astype(BF16)
        g2 = (g - g1.astype(F32)).astype(BF16)
        mc = mcat_ref[d, whole, :] if mild else mcat_ref[d]
        e = _dot(mc, g1) + _dot(mc, g2)
        return r0, d, g, e

    def in_chunk(st):
        r0, d, g, e = st
        q = q_ref[0, pl.ds(r0, ck), :]
        k = k_ref[0, pl.ds(r0, ck), :]
        q_chunk = e[-2 * ck:-ck]
        k_chunk = e[-ck:]
        last = ck - 1 if d == 0 else 0
        total = q_chunk[last:last + 1]
        q_in = (q * jnp.exp(q_chunk)).astype(BF16)

        def level_scores(qt, kt, idx):
            kb = block_rows(kt, lane_k).astype(BF16)
            return jnp.where(lmask_ref[d, idx] != 0.0, _dot_nt(qt, kb), 0.0)

        if mild:
            a = level_scores(q_in, k * jnp.exp(-q_chunk), GLA_LEVELS + 1)
        else:
            q_row0 = (row & 1) == (1 - d)
            a = level_scores(q.astype(BF16), k, 0)
            for l in range(GLA_LEVELS):
                dec = jnp.exp(jnp.where(q_row0, g, 0.0) if l == 0 else e[(l - 1) * ck:l * ck])
                a = a + level_scores((q * dec).astype(BF16), k * dec, l + 1)
        return r0, d, a.astype(BF16), q_in, (k * jnp.exp(k_chunk)).astype(BF16), jnp.exp(total)

    def state(st):
        r0, d, a, q_in, k_out, decay = st
        v = v_ref[0, pl.ds(r0, ck), :]
        vbd = block_rows(v.astype(F32), lane_v).astype(BF16)
        s_prev = st_ref[d]
        return (r0, d, _dot(a, vbd), _dot_nt(q_in, s_prev.astype(BF16)), s_prev * decay, _dot_tn(v, k_out))

    def commit(st):
        r0, d, o_in, o_cross, s_decayed, update = st
        acc_ref[d, pl.ds(r0, ck), :] = o_in + o_cross
        st_ref[d] = s_decayed + stmask_ref[...] * update

    return exponents, in_chunk, state, commit


def _gla_finish(acc_ref, g_ref, nw_ref, o_ref, tile):
    n = acc_ref.shape[1]
    lane = lax.broadcasted_iota(jnp.int32, (tile, LANES), 1)
    lo = lane < GLA_DV
    nw = nw_ref[...]

    def body(i, carry):
        r0 = pl.multiple_of(i * tile, tile)
        for j in range(GLA_VW // LANES):
            cols = slice(j * LANES, (j + 1) * LANES)
            o = acc_ref[0, pl.ds(r0, tile), cols] + acc_ref[1, pl.ds(r0, tile), cols]
            on = o * lax.rsqrt(_half_mean_square(o, lo) + RMS_EPS) * nw
            gate = _silu(g_ref[0, pl.ds(r0, tile), cols])
            o_ref[0, pl.ds(r0, tile), cols] = (on * gate).astype(BF16)
        return carry

    lax.fori_loop(0, n // tile, body, 0)


def _gla_kernel(q_ref, k_ref, v_ref, la_ref, g_ref, qc_ref, kc_ref, vc_ref, lac_ref, gc_ref,
                nw_ref, mcat_ref, lmask_ref, stmask_ref, o_ref, oc_ref, acc_ref, accc_ref, st_ref):
    n_chunks = q_ref.shape[1] // GLA_CHUNK
    c_chunks = qc_ref.shape[1] // GLA_CHUNK
    consts = (mcat_ref, lmask_ref, stmask_ref)
    st_ref[...] = jnp.zeros_like(st_ref)

    def min_chunk_sum(ref):
        la = ref[0]
        sums = jnp.sum(la.reshape(la.shape[0] // GLA_CHUNK, GLA_CHUNK, la.shape[1]), axis=1)
        return jnp.min(sums)

    mild = jnp.minimum(min_chunk_sum(la_ref), min_chunk_sum(lac_ref)) >= -GLA_MILD_DECAY

    def steps(refs, acc, n, is_mild):
        stages = _gla_stages(*refs, acc, st_ref, *consts, is_mild)

        per = min(GLA_CHUNKS_PER_STEP, n)

        def body(i, carry):
            items = []
            for j in range(per):
                c = i * per + j
                items += [(c, 0), (n - 1 - c, 1)]
            _skewed(stages, items)
            return carry

        lax.fori_loop(0, n // per, body, 0)

    for is_mild in (True, False):
        @pl.when(mild if is_mild else jnp.logical_not(mild))
        def _():
            steps((qc_ref, kc_ref, vc_ref, lac_ref), accc_ref, c_chunks, is_mild)
            steps((q_ref, k_ref, v_ref, la_ref), acc_ref, n_chunks, is_mild)

    _gla_finish(acc_ref, g_ref, nw_ref, o_ref, 256)
    _gla_finish(accc_ref, gc_ref, nw_ref, oc_ref, 256)


def _gla(q, k, v, la, g, qc, kc, vc, lac, gc, nw, consts):
    B, N, _ = q.shape
    L = qc.shape[1]
    mcat, lmask, stmask = consts
    tok = lambda n, w: pl.BlockSpec((1, n, w), lambda b: (b, 0, 0))
    full = lambda a: pl.BlockSpec(a.shape, lambda b: (0,) * a.ndim)
    return pl.pallas_call(
        _gla_kernel,
        grid=(B,),
        in_specs=[tok(N, GLA_KW), tok(N, GLA_KW), tok(N, GLA_VW), tok(N, 2 * GLA_KW), tok(N, GLA_VW),
                  tok(L, GLA_KW), tok(L, GLA_KW), tok(L, GLA_VW), tok(L, 2 * GLA_KW), tok(L, GLA_VW),
                  full(nw), full(mcat), full(lmask), full(stmask)],
        out_specs=[tok(N, GLA_VW), tok(L, GLA_VW)],
        out_shape=[jax.ShapeDtypeStruct((B, N, GLA_VW), BF16), jax.ShapeDtypeStruct((B, L, GLA_VW), BF16)],
        scratch_shapes=[pltpu.VMEM((2, N, GLA_VW), F32), pltpu.VMEM((2, L, GLA_VW), F32),
                        pltpu.VMEM((2, GLA_VW, GLA_KW), F32)],
        compiler_params=_cparams(1),
        name="gla",
    )(q, k, v, la, g, qc, kc, vc, lac, gc, nw, mcat, lmask, stmask)


_FFN_CHUNKS = ((0, 768), (768, 1536), (1536, 2304), (2304, FFN_HIDDEN))


def _mix_ffn_kernel(x_ref, na_ref, gl_ref, ga_ref, wm_ref, g1_ref, l1g_ref, l1b_ref,
                    sc_ref, sh_ref, g2_ref, wi_ref, wo_ref, l2g_ref, l2b_ref, o_ref):
    ts = x_ref.shape[1] // FFN_SUBTILES
    x1_of, u_of, acc_of = {}, {}, {}

    def rows(sub):
        return slice(sub * ts, (sub + 1) * ts)

    def mixer_sublayer(sub):
        r = rows(sub)
        o = jnp.concatenate([na_ref[0, r, :], gl_ref[0, r, :], ga_ref[0, r, :]], axis=-1)
        y = DEEPNORM_ALPHA * x_ref[0, r, :] + g1_ref[0] * _dot(o, wm_ref[...])
        x1 = _layer_norm(y, l1g_ref[...], l1b_ref[...])
        x1_of[sub] = x1
        u_of[sub] = (x1 * (1.0 + sc_ref[0]) + sh_ref[0]).astype(BF16)

    def up(item):
        sub, (a0, a1) = item
        if sub not in u_of:
            mixer_sublayer(sub)
        u = u_of[sub]
        return item, _dot(u, wi_ref[:, a0:a1]), _dot(u, wi_ref[:, FFN_HIDDEN + a0:FFN_HIDDEN + a1])

    def gate(st):
        item, ha, hb = st
        return item, (_silu(ha) * hb).astype(BF16)

    def down(st):
        (sub, (a0, a1)), t = st
        part = _dot(t, wo_ref[a0:a1, :])
        acc_of[sub] = acc_of[sub] + part if sub in acc_of else part
        if a1 == FFN_HIDDEN:
            y = DEEPNORM_ALPHA * x1_of.pop(sub) + g2_ref[0] * acc_of.pop(sub)
            o_ref[0, rows(sub), :] = _layer_norm(y, l2g_ref[...], l2b_ref[...])

    _skewed((up, gate, down), [(sub, ch) for sub in range(FFN_SUBTILES) for ch in _FFN_CHUNKS])


def _mix_ffn(x, o_na, o_gl, o_ga, wm, g1, l1g, l1b, sc, sh, g2, wi, wo, l2g, l2b, tm):
    B, N, D = x.shape
    tspec = lambda width: pl.BlockSpec((1, tm, width), lambda b, i: (b, i, 0))
    mod = pl.BlockSpec((1, 1, D), lambda b, i: (b, 0, 0))
    vec = pl.BlockSpec((1, D), lambda b, i: (0, 0))
    resident = lambda a: pl.BlockSpec(a.shape, lambda b, i: (0, 0), pipeline_mode=pl.Buffered(1))
    return pl.pallas_call(
        _mix_ffn_kernel,
        grid=(B, N // tm),
        in_specs=[tspec(D), tspec(NA_W), tspec(GLA_VW), tspec(GQA_QW), resident(wm), mod, vec, vec,
                  mod, mod, mod, resident(wi), resident(wo), vec, vec],
        out_specs=tspec(D),
        out_shape=jax.ShapeDtypeStruct((B, N, D), F32),
        compiler_params=_cparams(2),
        name="mix_ffn",
    )(x, o_na, o_gl, o_ga, wm, g1, l1g, l1b, sc, sh, g2, wi, wo, l2g, l2b)


def _rope_tables(n):
    t = jnp.arange(n)
    row = (t // GRID_W).astype(F32)
    col = (t % GRID_W).astype(F32)
    inv_freq = ROPE_THETA ** (-jnp.arange(ROPE_AXIS_PAIRS, dtype=F32) / ROPE_AXIS_PAIRS)
    ang_r = row[:, None] * inv_freq
    ang_c = col[:, None] * inv_freq
    ang = jnp.concatenate([ang_r, ang_r, ang_c, ang_c], axis=-1)
    sign = jnp.where((jnp.arange(HEAD_DIM) % 32) < 16, -1.0, 1.0).astype(F32)
    cos = jnp.tile(jnp.cos(ang), (1, 2))
    sin = jnp.tile(jnp.sin(ang) * sign, (1, 2))
    return cos, sin


def _pair_major(w, axis):
    shape = w.shape
    lead, tail = shape[:axis], shape[axis + 1:]
    w = w.reshape(lead + (GQA_KV_HEADS, GQA_REP, HEAD_DIM) + tail)
    w = jnp.swapaxes(w, axis, axis + 1)
    return w.reshape(shape)


def kernel(x, c, ctx, c_ctx, w_ada, b_ada, w_in, na_rpb, gla_wa2, gla_ba, gla_norm_w, gqa_qnorm_w,
           gqa_knorm_w, w_out, ln1_g, ln1_b, w_ffn_in, w_ffn_out, ln2_g, ln2_b):
    B, N, D = x.shape
    L = ctx.shape[1]
    depth = w_in.shape[0]
    tm = min(N, ROW_TILE)
    tmc = min(B * L, ROW_TILE)
    assert D == D_MODEL and N % tm == 0 and (B * L) % tmc == 0
    assert N % (GRID_W * NA_WIN_ROWS) == 0 and N % GLA_CHUNK == 0 and L % GLA_CHUNK == 0

    pad = (-(B + 1)) % 8
    cvec = jnp.concatenate([c, c_ctx[None, :], jnp.zeros((pad, D), F32)], axis=0)
    mods = _ada(cvec, w_ada, b_ada)

    w_lr = w_in[:, :, _O_GLLR:_O_GAQ].reshape(depth, D, 2, GLA_RANK).transpose(0, 2, 1, 3)
    w_z = _fold_gate_weights(w_lr, gla_wa2)

    cos, sin = _rope_tables(N)
    cos_c = jnp.ones((B * L, LANES), F32)
    sin_c = jnp.zeros((B * L, LANES), F32)
    gla_consts = _gla_constants()
    per_batch = lambda t: t.reshape(B, L, t.shape[-1])
    flat = lambda t: t.reshape(1, B * L, t.shape[-1])

    xc = flat(ctx)
    for l in range(depth):
        ctx_out = l < depth - 1
        m_lat = mods[l, :B].reshape(B, 6, 1, D)
        m_ctx = mods[l, B].reshape(1, 6, 1, D)
        sh1, sc1, g1, sh2, sc2, g2 = (m_lat[:, i] for i in range(6))
        sh1c, sc1c, g1c, sh2c, sc2c, g2c = (m_ctx[:, i] for i in range(6))

        wl = w_in[l]
        w_proj = jnp.concatenate(
            [wl[:, _O_NAQ:_O_GLQ], _pair_major(wl[:, _O_GAQ:_O_GAK], 1), wl[:, _O_GLQ:_O_GLLR],
             w_z[l], wl[:, _O_GAK:_O_END]], axis=1).astype(BF16)
        zb = gla_ba[l].reshape(1, 2 * GLA_KW)
        qnw = jnp.tile(gqa_qnorm_w[l], 2).reshape(1, LANES)
        knw = jnp.tile(gqa_knorm_w[l], 2).reshape(1, LANES)
        glnw = jnp.tile(gla_norm_w[l], 2).reshape(1, LANES)
        wo_l = w_out[l]
        w_o = jnp.concatenate([wo_l[:NA_W + GLA_VW], _pair_major(wo_l[NA_W + GLA_VW:], 0)], axis=0).astype(BF16)
        lg1, lb1 = ln1_g[l].reshape(1, D), ln1_b[l].reshape(1, D)
        lg2, lb2 = ln2_g[l].reshape(1, D), ln2_b[l].reshape(1, D)
        wi = w_ffn_in[l].astype(BF16)
        wo = w_ffn_out[l].astype(BF16)

        (na_q, na_k, na_v, ga_q, gl_q, gl_k, gl_v, gl_g, gl_la, ga_k, ga_v) = _inproj(
            x, sc1, sh1, w_proj, zb, qnw, knw, cos, sin, tm)
        (na_qc, na_kc, na_vc, ga_qc, gl_qc, gl_kc, gl_vc, gl_gc, gl_lac, ga_kc, ga_vc) = map(
            per_batch, _inproj(xc, sc1c, sh1c, w_proj, zb, qnw, knw, cos_c, sin_c, tmc))

        bias = _na_bias(na_rpb[l].reshape(-1))
        o_na = _na_attention(na_q, na_k, na_v, na_kc, na_vc, bias)
        o_gl, oc_gl = _gla(gl_q, gl_k, gl_v, gl_la, gl_g, gl_qc, gl_kc, gl_vc, gl_lac, gl_gc, glnw, gla_consts)
        o_ga = _pair_attention(ga_q, ga_k, ga_v, ga_kc, ga_vc, shared_kv=True, tq=min(N, GQA_Q_TILE))

        x = _mix_ffn(x, o_na, o_gl, o_ga, w_o, g1, lg1, lb1, sc2, sh2, g2, wi, wo, lg2, lb2, tm)

        if ctx_out:
            nbc = math.gcd(B, CTX_ATTN_BATCH_BLOCK)
            oc_na = _pair_attention(na_qc, na_kc, na_vc, shared_kv=False, tq=L, batch_block=nbc)
            oc_ga = _pair_attention(ga_qc, ga_kc, ga_vc, shared_kv=True, tq=L, batch_block=nbc)
            xc = _mix_ffn(xc, flat(oc_na), flat(oc_gl), flat(oc_ga), w_o, g1c, lg1, lb1, sc2c, sh2c, g2c,
                          wi, wo, lg2, lb2, tmc)
    return x
```

```python
import functools
import math

import numpy as np
import jax
import jax.numpy as jnp
from jax import lax
from jax.experimental import pallas as pl
from jax.experimental.pallas import tpu as pltpu

F32 = jnp.float32
BF16 = jnp.bfloat16

D_MODEL = 1024
DEPTH = 2
GRID_W = 64
HEAD_DIM = 64
LANES = 128

NA_HEADS = 6
NA_WIN_ROWS = 8
NA_WIN_COLS = 16
NA_W = NA_HEADS * HEAD_DIM
NA_PAIRS = NA_W // LANES
NA_DR = 2 * NA_WIN_ROWS - 1
NA_DC = 2 * NA_WIN_COLS - 1
NA_ROWS_PER_STEP = 32

GLA_HEADS = 4
GLA_DK = 32
GLA_DV = 64
GLA_RANK = 16
GLA_GATE_NORM = 16.0
GLA_CHUNK = 64
GLA_KW = GLA_HEADS * GLA_DK
GLA_VW = GLA_HEADS * GLA_DV
GLA_LEVELS = 6
GLA_MILD_DECAY = 40.0
GLA_CHUNKS_PER_STEP = 8

GQA_Q_HEADS = 6
GQA_KV_HEADS = 2
GQA_REP = GQA_Q_HEADS // GQA_KV_HEADS
GQA_QW = GQA_Q_HEADS * HEAD_DIM
GQA_KVW = GQA_KV_HEADS * HEAD_DIM
GQA_PAIRS = GQA_QW // LANES
ROW_TILE = 1024
GQA_Q_TILE = 2048
CTX_ATTN_BATCH_BLOCK = 4
INPROJ_SUBTILES = 4
FFN_SUBTILES = 2
PAIR_ATTN_SUB_ROWS = 128
ROPE_THETA = 10000.0
ROPE_AXIS_PAIRS = HEAD_DIM // 4

FFN_HIDDEN = 2816
MIX_W = NA_W + GLA_VW + GQA_QW

DEEPNORM_ALPHA = (2.0 * DEPTH) ** 0.25
LN_EPS = 1e-5
RMS_EPS = 1e-6
NEG_BIG = -1e30
LOG2_E = 1.4426950408889634
Q_SCALE = HEAD_DIM ** -0.5 * LOG2_E

_O_NAQ, _O_NAK, _O_NAV = 0, 384, 768
_O_GLQ, _O_GLK, _O_GLV, _O_GLG, _O_GLLR = 1152, 1280, 1408, 1664, 1920
_O_GAQ, _O_GAK, _O_GAV, _O_END = 1952, 2336, 2464, 2592
_C_NAQ, _C_NAK, _C_NAV, _C_GAQ = 0, 384, 768, 1152
_C_GLQ, _C_GLK, _C_GLV, _C_GLG, _C_Z = 1536, 1664, 1792, 2048, 2304
_C_GAK, _C_GAV, _C_END = 2560, 2688, 2816

VMEM_LIMIT = 48 * 1024 * 1024


def _cparams(n_axes):
    return pltpu.CompilerParams(dimension_semantics=("arbitrary",) * n_axes,
                                vmem_limit_bytes=VMEM_LIMIT)


def _dot(a, b):
    return jnp.dot(a, b, preferred_element_type=F32)


def _dot_nt(a, b):
    return lax.dot_general(a, b, (((1,), (1,)), ((), ())), preferred_element_type=F32)


def _dot_tn(a, b):
    return lax.dot_general(a, b, (((0,), (0,)), ((), ())), preferred_element_type=F32)


def _silu(x):
    return x * jax.nn.sigmoid(x)


def _layer_norm(y, g, b):
    mu = jnp.mean(y, axis=-1, keepdims=True)
    d = y - mu
    var = jnp.mean(d * d, axis=-1, keepdims=True)
    return d * lax.rsqrt(var + LN_EPS) * g + b


def _skewed(stages, items):
    n, k = len(items), len(stages)
    live = {}
    for j in range(n + k - 1):
        for s in range(k):
            idx = j - s
            if 0 <= idx < n:
                live[s, idx] = stages[s](items[idx] if s == 0 else live.pop((s - 1, idx)))


def _half_mean_square(y, lo):
    s = y * y
    s_lo = jnp.sum(jnp.where(lo, s, 0.0), axis=-1, keepdims=True)
    s_hi = jnp.sum(jnp.where(lo, 0.0, s), axis=-1, keepdims=True)
    return jnp.where(lo, s_lo, s_hi) * (1.0 / HEAD_DIM)


_ADA_TN = 1024


def _ada_kernel(c_ref, w_ref, b_ref, o_ref):
    s = _silu(c_ref[...])
    o_ref[0] = jnp.dot(s, w_ref[0], precision=lax.Precision.HIGHEST,
                       preferred_element_type=F32) + b_ref[0]


def _ada(cvec, w_ada, b_ada):
    rows = cvec.shape[0]
    depth, d, n6 = w_ada.shape
    return pl.pallas_call(
        _ada_kernel,
        grid=(depth, n6 // _ADA_TN),
        in_specs=[pl.BlockSpec((rows, d), lambda l, j: (0, 0)),
                  pl.BlockSpec((1, d, _ADA_TN), lambda l, j: (l, 0, j)),
                  pl.BlockSpec((1, 1, _ADA_TN), lambda l, j: (l, 0, j))],
        out_specs=pl.BlockSpec((1, rows, _ADA_TN), lambda l, j: (l, 0, j)),
        out_shape=jax.ShapeDtypeStruct((depth, rows, n6), F32),
        compiler_params=_cparams(2),
        name="ada",
    )(cvec, w_ada, b_ada.reshape(depth, 1, n6))


def _fold_kernel(wlr_ref, wa2_ref, o_ref):
    for e in range(2):
        o_ref[0, :, e * GLA_KW:(e + 1) * GLA_KW] = jnp.dot(
            wlr_ref[0, e], wa2_ref[0, e], precision=lax.Precision.HIGHEST,
            preferred_element_type=F32)


def _fold_gate_weights(w_lr, wa2):
    depth = w_lr.shape[0]
    return pl.pallas_call(
        _fold_kernel,
        grid=(depth,),
        in_specs=[pl.BlockSpec((1, 2, D_MODEL, GLA_RANK), lambda l: (l, 0, 0, 0)),
                  pl.BlockSpec((1, 2, GLA_RANK, GLA_KW), lambda l: (l, 0, 0, 0))],
        out_specs=pl.BlockSpec((1, D_MODEL, 2 * GLA_KW), lambda l: (l, 0, 0)),
        out_shape=jax.ShapeDtypeStruct((depth, D_MODEL, 2 * GLA_KW), F32),
        compiler_params=_cparams(1),
        name="fold_gate",
    )(w_lr, wa2)


def _inproj_kernel(x_ref, sc_ref, sh_ref, w_ref, zb_ref, qnw_ref, knw_ref, cos_ref, sin_ref,
                   naq_ref, nak_ref, nav_ref, gaq_ref, glq_ref, glk_ref, glv_ref, glg_ref,
                   gla_ref, gak_ref, gav_ref):
    ts = x_ref.shape[1] // INPROJ_SUBTILES
    lane = lax.broadcasted_iota(jnp.int32, (ts, LANES), 1)
    lo = lane < HEAD_DIM
    first16 = (lane & 31) < 16
    u_of = {}

    def rows(sub):
        return slice(sub * ts, (sub + 1) * ts)

    def matmul(item):
        sub, (a, b, epilogue) = item
        if sub not in u_of:
            u_of[sub] = (x_ref[0, rows(sub), :] * (1.0 + sc_ref[0]) + sh_ref[0]).astype(BF16)
        return sub, epilogue, _dot(u_of[sub], w_ref[:, a:b])

    def norm_rope(y, w, r):
        yn = y * lax.rsqrt(_half_mean_square(y, lo) + RMS_EPS) * w
        rot = jnp.where(first16, pltpu.roll(yn, LANES - 16, 1), pltpu.roll(yn, 16, 1))
        return yn * cos_ref[r, :] + rot * sin_ref[r, :]

    def na_qk(r, y):
        naq_ref[0, r, :] = (y[:, :NA_W] * Q_SCALE).astype(BF16)
        nak_ref[0, r, :] = y[:, NA_W:].astype(BF16)

    def na_v_gqa_q(r, y):
        nav_ref[0, r, :] = y[:, :NA_W].astype(BF16)
        qnw = qnw_ref[...]
        for p in range(GQA_PAIRS):
            t = norm_rope(y[:, NA_W + p * LANES:NA_W + (p + 1) * LANES], qnw, r)
            gaq_ref[0, r, p * LANES:(p + 1) * LANES] = (t * Q_SCALE).astype(BF16)

    def gla_qkv(r, y):
        glq_ref[0, r, :] = y[:, :GLA_KW] * GLA_DK ** -0.5
        glk_ref[0, r, :] = y[:, GLA_KW:2 * GLA_KW]
        glv_ref[0, r, :] = y[:, 2 * GLA_KW:].astype(BF16)

    def gla_gates(r, y):
        glg_ref[0, r, :] = y[:, :GLA_VW]
        z = y[:, GLA_VW:] + zb_ref[...]
        log_sig = jnp.minimum(z, 0.0) - jnp.log1p(jnp.exp(-jnp.abs(z)))
        gla_ref[0, r, :] = log_sig * (1.0 / GLA_GATE_NORM)

    def gqa_kv(r, y):
        gak_ref[0, r, :] = norm_rope(y[:, :LANES], knw_ref[...], r).astype(BF16)
        gav_ref[0, r, :] = y[:, LANES:].astype(BF16)

    segments = [(_C_NAQ, _C_NAV, na_qk), (_C_NAV, _C_GLQ, na_v_gqa_q), (_C_GLQ, _C_GLG, gla_qkv),
                (_C_GLG, _C_GAK, gla_gates), (_C_GAK, _C_END, gqa_kv)]
    _skewed((matmul, lambda st: st[1](rows(st[0]), st[2])),
            [(sub, sg) for sub in range(INPROJ_SUBTILES) for sg in segments])


def _inproj(x, sc, sh, w, zb, qnw, knw, cos, sin, tm):
    B, N, D = x.shape
    tok = lambda width, dt: jax.ShapeDtypeStruct((B, N, width), dt)
    tspec = lambda width: pl.BlockSpec((1, tm, width), lambda b, i: (b, i, 0))
    vec = lambda width: pl.BlockSpec((1, width), lambda b, i: (0, 0))
    out_shapes = [tok(NA_W, BF16), tok(NA_W, BF16), tok(NA_W, BF16), tok(GQA_QW, BF16),
                  tok(GLA_KW, F32), tok(GLA_KW, F32), tok(GLA_VW, BF16), tok(GLA_VW, F32),
                  tok(2 * GLA_KW, F32), tok(GQA_KVW, BF16), tok(GQA_KVW, BF16)]
    return pl.pallas_call(
        _inproj_kernel,
        grid=(B, N // tm),
        in_specs=[tspec(D),
                  pl.BlockSpec((1, 1, D), lambda b, i: (b, 0, 0)),
                  pl.BlockSpec((1, 1, D), lambda b, i: (b, 0, 0)),
                  pl.BlockSpec((D, _C_END), lambda b, i: (0, 0)),
                  vec(2 * GLA_KW), vec(LANES), vec(LANES),
                  pl.BlockSpec((tm, LANES), lambda b, i: (i, 0)),
                  pl.BlockSpec((tm, LANES), lambda b, i: (i, 0))],
        out_specs=[tspec(s.shape[-1]) for s in out_shapes],
        out_shape=out_shapes,
        compiler_params=_cparams(2),
        name="inproj",
    )(x, sc, sh, w, zb, qnw, knw, cos, sin)


def _nabias_kernel(rpb_ref, o_ref):
    h = pl.program_id(0)
    qc = lax.broadcasted_iota(jnp.int32, (GRID_W, LANES), 0)
    lane = lax.broadcasted_iota(jnp.int32, (GRID_W, LANES), 1)
    kc = lane & (GRID_W - 1)
    hi = lane >= GRID_W
    idx = jnp.clip(kc - qc, -(NA_WIN_COLS - 1), NA_WIN_COLS - 1) + (NA_WIN_COLS - 1)
    start = jnp.clip(qc - NA_WIN_COLS // 2, 0, GRID_W - NA_WIN_COLS)
    col_in = jnp.logical_and(kc >= start, kc < start + NA_WIN_COLS)
    base = h * (NA_DR * NA_DC)

    pair_tables = []
    for dr0 in range(NA_DR - 1):

        def body(j, t, dr0=dr0):
            s0 = rpb_ref[base + dr0 * NA_DC + j]
            s1 = rpb_ref[base + (dr0 + 1) * NA_DC + j]
            return jnp.where(idx == j, jnp.where(hi, s1, s0), t)

        t = lax.fori_loop(0, NA_DC, body, jnp.zeros((GRID_W, LANES), F32), unroll=True)
        pair_tables.append(jnp.where(col_in, t * LOG2_E, NEG_BIG))
    for oi in range(NA_WIN_ROWS):
        for i in range(NA_WIN_ROWS // 2):
            o_ref[0, oi, :, i * LANES:(i + 1) * LANES] = pair_tables[oi + 2 * i]


def _na_bias(rpb_flat):
    return pl.pallas_call(
        _nabias_kernel,
        grid=(NA_HEADS,),
        in_specs=[pl.BlockSpec(memory_space=pltpu.SMEM)],
        out_specs=pl.BlockSpec((1, NA_WIN_ROWS, GRID_W, NA_WIN_ROWS * GRID_W),
                               lambda h: (h, 0, 0, 0)),
        out_shape=jax.ShapeDtypeStruct((NA_HEADS, NA_WIN_ROWS, GRID_W, NA_WIN_ROWS * GRID_W), F32),
        compiler_params=_cparams(1),
        name="na_bias",
    )(rpb_flat)


def _na_kernel(q_ref, k_ref, v_ref, kc_ref, vc_ref, b_ref, o_ref):
    n = q_ref.shape[1]
    rows = n // GRID_W
    win = NA_WIN_ROWS * GRID_W
    lane = lax.broadcasted_iota(jnp.int32, (GRID_W, LANES), 1)
    lo = lane < HEAD_DIM
    kc = kc_ref[0]
    vc = vc_ref[0]
    zero = jnp.zeros((GRID_W, LANES), BF16)

    def scores(r):
        rs = jnp.clip(r - NA_WIN_ROWS // 2, 0, rows - NA_WIN_ROWS)
        oi = rs - r + (NA_WIN_ROWS - 1)
        q0 = pl.multiple_of(r * GRID_W, GRID_W)
        k0 = pl.multiple_of(rs * GRID_W, GRID_W)
        q = q_ref[0, pl.ds(q0, GRID_W), :]
        kw = k_ref[0, pl.ds(k0, win), :]
        qs = jnp.concatenate([jnp.where(lo, q, zero), jnp.where(lo, zero, q)], axis=0)
        s_lat = _dot_nt(qs, kw) + jnp.concatenate([b_ref[0, oi], b_ref[1, oi]], axis=0)
        s_ctx = _dot_nt(qs, kc)
        return q0, k0, s_lat, s_ctx

    def softmax(st):
        q0, k0, s_lat, s_ctx = st
        m = jnp.maximum(jnp.max(s_lat, axis=-1, keepdims=True),
                        jnp.max(s_ctx, axis=-1, keepdims=True))
        p_lat = jnp.exp2(s_lat - m)
        p_ctx = jnp.exp2(s_ctx - m)
        den = jnp.sum(p_lat, axis=-1, keepdims=True) + jnp.sum(p_ctx, axis=-1, keepdims=True)
        return q0, k0, p_lat.astype(BF16), p_ctx.astype(BF16), den

    def values(st):
        q0, k0, p_lat, p_ctx, den = st
        vw = v_ref[0, pl.ds(k0, win), :]
        o = (_dot(p_lat, vw) + _dot(p_ctx, vc)) / den
        o_ref[0, pl.ds(q0, GRID_W), :] = jnp.where(lo, o[:GRID_W], o[GRID_W:]).astype(BF16)

    per = min(NA_ROWS_PER_STEP, rows)

    def body(i, carry):
        _skewed((scores, softmax, values), [i * per + j for j in range(per)])
        return carry

    lax.fori_loop(0, rows // per, body, 0)


def _na_attention(q, k, v, kc, vc, bias):
    B, N, _ = q.shape
    L = kc.shape[1]
    lat = pl.BlockSpec((1, N, LANES), lambda b, p: (b, 0, p))
    cx = pl.BlockSpec((1, L, LANES), lambda b, p: (b, 0, p))
    return pl.pallas_call(
        _na_kernel,
        grid=(B, NA_PAIRS),
        in_specs=[lat, lat, lat, cx, cx,
                  pl.BlockSpec((2, NA_WIN_ROWS, GRID_W, NA_WIN_ROWS * GRID_W),
                               lambda b, p: (p, 0, 0, 0))],
        out_specs=lat,
        out_shape=jax.ShapeDtypeStruct((B, N, NA_W), BF16),
        compiler_params=_cparams(2),
        name="na_attn",
    )(q, k, v, kc, vc, bias)


def _pair_attn_kernel(*refs, two_sources):
    if two_sources:
        q_ref, ka_ref, va_ref, kb_ref, vb_ref, o_ref = refs
    else:
        q_ref, ka_ref, va_ref, o_ref = refs
    tq = q_ref.shape[1]
    ts = min(tq, PAIR_ATTN_SUB_ROWS)
    lane = lax.broadcasted_iota(jnp.int32, (ts, LANES), 1)
    lo = lane < HEAD_DIM
    zero = jnp.zeros((ts, LANES), BF16)

    def with_ones(v):
        return jnp.concatenate([v, jnp.ones_like(v)], axis=1)

    n_batch = q_ref.shape[0]
    ka = [ka_ref[b] for b in range(n_batch)]
    va = [with_ones(va_ref[b]) for b in range(n_batch)]
    if two_sources:
        kb = [kb_ref[b] for b in range(n_batch)]
        vb = [with_ones(vb_ref[b]) for b in range(n_batch)]

    def scores(item):
        b, sub, h = item
        q = q_ref[b, sub * ts:(sub + 1) * ts, :]
        qm = jnp.where(lo, q, zero) if h == 0 else jnp.where(lo, zero, q)
        s_a = _dot_nt(qm, ka[b])
        s_b = _dot_nt(qm, kb[b]) if two_sources else None
        return item, s_a, s_b

    def softmax(st):
        item, s_a, s_b = st
        m = jnp.max(s_a, axis=-1, keepdims=True)
        if two_sources:
            m = jnp.maximum(m, jnp.max(s_b, axis=-1, keepdims=True))
        p_a = jnp.exp2(s_a - m).astype(BF16)
        p_b = jnp.exp2(s_b - m).astype(BF16) if two_sources else None
        return item, p_a, p_b

    done = {}

    def values(st):
        (b, sub, h), p_a, p_b = st
        o = _dot(p_a, va[b])
        if two_sources:
            o = o + _dot(p_b, vb[b])
        done[b, sub, h] = o[:, :LANES] / o[:, LANES:]
        if h == 1:
            o_ref[b, sub * ts:(sub + 1) * ts, :] = jnp.where(
                lo, done.pop((b, sub, 0)), done.pop((b, sub, 1))).astype(BF16)

    _skewed((scores, softmax, values),
            [(b, sub, h) for b in range(n_batch) for sub in range(tq // ts) for h in range(2)])


def _pair_attention(q, ka, va, kb=None, vb=None, *, shared_kv, tq, batch_block=1):
    B, Nq, W = q.shape
    pairs = W // LANES
    nb = batch_block
    kv_map = (lambda b, p, i: (b, 0, 0)) if shared_kv else (lambda b, p, i: (b, 0, p))
    qspec = pl.BlockSpec((nb, tq, LANES), lambda b, p, i: (b, i, p))
    in_specs = [qspec,
                pl.BlockSpec((nb, ka.shape[1], LANES), kv_map),
                pl.BlockSpec((nb, ka.shape[1], LANES), kv_map)]
    args = [q, ka, va]
    if kb is not None:
        in_specs += [pl.BlockSpec((nb, kb.shape[1], LANES), kv_map),
                     pl.BlockSpec((nb, kb.shape[1], LANES), kv_map)]
        args += [kb, vb]
    return pl.pallas_call(
        functools.partial(_pair_attn_kernel, two_sources=kb is not None),
        grid=(B // nb, pairs, Nq // tq),
        in_specs=in_specs,
        out_specs=qspec,
        out_shape=jax.ShapeDtypeStruct((B, Nq, W), BF16),
        compiler_params=_cparams(3),
        name="pair_attn",
    )(*args)


def _gla_constants():
    c = GLA_CHUNK
    t = np.arange(c)[:, None]
    i = np.arange(c)[None, :]
    mcat = np.zeros((2, (GLA_LEVELS + 1) * c, c), np.float32)
    masks = np.zeros((2, GLA_LEVELS + 2, c, c), np.float32)
    for d in range(2):
        masks[d, GLA_LEVELS + 1] = (i <= t) if d == 0 else (i >= t)
        for l in range(1, GLA_LEVELS + 1):
            m = 2 ** l
            same = (t // m) == (i // m)
            q_side = same & ((i <= t) if d == 0 else (i >= t))
            k_side = same & ((i > t) if d == 0 else (i < t))
            if l < GLA_LEVELS:
                is_q_row = ((t // m) % 2 == 1) if d == 0 else ((t // m) % 2 == 0)
                mcat[d, (l - 1) * c:l * c] = np.where(is_q_row, q_side, k_side)
            else:
                mcat[d, (l - 1) * c:l * c] = q_side
                mcat[d, l * c:(l + 1) * c] = k_side
        masks[d, 0] = (t == i)
        for l in range(GLA_LEVELS):
            m = 2 ** l
            tb, sb = t // m, i // m
            if d == 0:
                masks[d, l + 1] = (tb % 2 == 1) & (sb == tb - 1)
            else:
                masks[d, l + 1] = (tb % 2 == 0) & (sb == tb + 1)
    masks = np.tile(masks, (1, 1, 1, GLA_HEADS))
    rows = np.arange(GLA_VW)[:, None] // GLA_DV
    cols = np.arange(GLA_KW)[None, :] // GLA_DK
    stmask = (rows == cols).astype(np.float32)
    return jnp.asarray(mcat, BF16), jnp.asarray(masks, F32), jnp.asarray(stmask, F32)


def _gla_stages(q_ref, k_ref, v_ref, la_ref, acc_ref, st_ref, mcat_ref, lmask_ref, stmask_ref, mild):
    ck = GLA_CHUNK
    whole = slice((GLA_LEVELS - 1) * ck, (GLA_LEVELS + 1) * ck)
    lane_k = lax.broadcasted_iota(jnp.int32, (1, GLA_KW), 1) // GLA_DK
    lane_v = lax.broadcasted_iota(jnp.int32, (1, GLA_VW), 1) // GLA_DV
    row = lax.broadcasted_iota(jnp.int32, (ck, GLA_KW), 0)

    def block_rows(x, lane_head):
        return jnp.concatenate([jnp.where(lane_head == h, x, 0.0) for h in range(GLA_HEADS)], axis=0)

    def exponents(item):
        c, d = item
        r0 = pl.multiple_of(c * ck, ck)
        g = la_ref[0, pl.ds(r0, ck), d * GLA_KW:(d + 1) * GLA_KW]
        g1 = g.astype(BF16)
        g2 = (g - g1.astype(F32)).astype(BF16)
        mc = mcat_ref[d, whole, :] if mild else mcat_ref[d]
        e = _dot(mc, g1) + _dot(mc, g2)
        return r0, d, g, e

    def in_chunk(st):
        r0, d, g, e = st
        q = q_ref[0, pl.ds(r0, ck), :]
        k = k_ref[0, pl.ds(r0, ck), :]
        q_chunk = e[-2 * ck:-ck]
        k_chunk = e[-ck:]
        last = ck - 1 if d == 0 else 0
        total = q_chunk[last:last + 1]
        q_in = (q * jnp.exp(q_chunk)).astype(BF16)

        def level_scores(qt, kt, idx):
            kb = block_rows(kt, lane_k).astype(BF16)
            return jnp.where(lmask_ref[d, idx] != 0.0, _dot_nt(qt, kb), 0.0)

        if mild:
            a = level_scores(q_in, k * jnp.exp(-q_chunk), GLA_LEVELS + 1)
        else:
            q_row0 = (row & 1) == (1 - d)
            a = level_scores(q.astype(BF16), k, 0)
            for l in range(GLA_LEVELS):
                dec = jnp.exp(jnp.where(q_row0, g, 0.0) if l == 0 else e[(l - 1) * ck:l * ck])
                a = a + level_scores((q * dec).astype(BF16), k * dec, l + 1)
        return r0, d, a.astype(BF16), q_in, (k * jnp.exp(k_chunk)).astype(BF16), jnp.exp(total)

    def state(st):
        r0, d, a, q_in, k_out, decay = st
        v = v_ref[0, pl.ds(r0, ck), :]
        vbd = block_rows(v.astype(F32), lane_v).astype(BF16)
        s_prev = st_ref[d]
        return (r0, d, _dot(a, vbd), _dot_nt(q_in, s_prev.astype(BF16)), s_prev * decay, _dot_tn(v, k_out))

    def commit(st):
        r0, d, o_in, o_cross, s_decayed, update = st
        acc_ref[d, pl.ds(r0, ck), :] = o_in + o_cross
        st_ref[d] = s_decayed + stmask_ref[...] * update

    return exponents, in_chunk, state, commit


def _gla_finish(acc_ref, g_ref, nw_ref, o_ref, tile):
    n = acc_ref.shape[1]
    lane = lax.broadcasted_iota(jnp.int32, (tile, LANES), 1)
    lo = lane < GLA_DV
    nw = nw_ref[...]

    def body(i, carry):
        r0 = pl.multiple_of(i * tile, tile)
        for j in range(GLA_VW // LANES):
            cols = slice(j * LANES, (j + 1) * LANES)
            o = acc_ref[0, pl.ds(r0, tile), cols] + acc_ref[1, pl.ds(r0, tile), cols]
            on = o * lax.rsqrt(_half_mean_square(o, lo) + RMS_EPS) * nw
            gate = _silu(g_ref[0, pl.ds(r0, tile), cols])
            o_ref[0, pl.ds(r0, tile), cols] = (on * gate).astype(BF16)
        return carry

    lax.fori_loop(0, n // tile, body, 0, unroll=2)


def _gla_kernel(q_ref, k_ref, v_ref, la_ref, g_ref, qc_ref, kc_ref, vc_ref, lac_ref, gc_ref,
                nw_ref, mcat_ref, lmask_ref, stmask_ref, o_ref, oc_ref, acc_ref, accc_ref, st_ref):
    n_chunks = q_ref.shape[1] // GLA_CHUNK
    c_chunks = qc_ref.shape[1] // GLA_CHUNK
    consts = (mcat_ref, lmask_ref, stmask_ref)
    st_ref[...] = jnp.zeros_like(st_ref)

    def min_chunk_sum(ref):
        la = ref[0]
        sums = jnp.sum(la.reshape(la.shape[0] // GLA_CHUNK, GLA_CHUNK, la.shape[1]), axis=1)
        return jnp.min(sums)

    mild = jnp.minimum(min_chunk_sum(la_ref), min_chunk_sum(lac_ref)) >= -GLA_MILD_DECAY

    def steps(refs, acc, n, is_mild):
        stages = _gla_stages(*refs, acc, st_ref, *consts, is_mild)

        per = min(GLA_CHUNKS_PER_STEP, n)

        def body(i, carry):
            items = []
            for j in range(per):
                c = i * per + j
                items += [(c, 0), (n - 1 - c, 1)]
            _skewed(stages, items)
            return carry

        lax.fori_loop(0, n // per, body, 0)

    for is_mild in (True, False):
        @pl.when(mild if is_mild else jnp.logical_not(mild))
        def _():
            steps((qc_ref, kc_ref, vc_ref, lac_ref), accc_ref, c_chunks, is_mild)
            steps((q_ref, k_ref, v_ref, la_ref), acc_ref, n_chunks, is_mild)

    _gla_finish(acc_ref, g_ref, nw_ref, o_ref, 256)
    _gla_finish(accc_ref, gc_ref, nw_ref, oc_ref, 256)


def _gla(q, k, v, la, g, qc, kc, vc, lac, gc, nw, consts):
    B, N, _ = q.shape
    L = qc.shape[1]
    mcat, lmask, stmask = consts
    tok = lambda n, w: pl.BlockSpec((1, n, w), lambda b: (b, 0, 0))
    full = lambda a: pl.BlockSpec(a.shape, lambda b: (0,) * a.ndim)
    return pl.pallas_call(
        _gla_kernel,
        grid=(B,),
        in_specs=[tok(N, GLA_KW), tok(N, GLA_KW), tok(N, GLA_VW), tok(N, 2 * GLA_KW), tok(N, GLA_VW),
                  tok(L, GLA_KW), tok(L, GLA_KW), tok(L, GLA_VW), tok(L, 2 * GLA_KW), tok(L, GLA_VW),
                  full(nw), full(mcat), full(lmask), full(stmask)],
        out_specs=[tok(N, GLA_VW), tok(L, GLA_VW)],
        out_shape=[jax.ShapeDtypeStruct((B, N, GLA_VW), BF16), jax.ShapeDtypeStruct((B, L, GLA_VW), BF16)],
        scratch_shapes=[pltpu.VMEM((2, N, GLA_VW), F32), pltpu.VMEM((2, L, GLA_VW), F32),
                        pltpu.VMEM((2, GLA_VW, GLA_KW), F32)],
        compiler_params=_cparams(1),
        name="gla",
    )(q, k, v, la, g, qc, kc, vc, lac, gc, nw, mcat, lmask, stmask)


_FFN_CHUNKS = ((0, 768), (768, 1536), (1536, 2304), (2304, FFN_HIDDEN))


def _mix_ffn_kernel(x_ref, na_ref, gl_ref, ga_ref, wm_ref, g1_ref, l1g_ref, l1b_ref,
                    sc_ref, sh_ref, g2_ref, wi_ref, wo_ref, l2g_ref, l2b_ref, o_ref):
    ts = x_ref.shape[1] // FFN_SUBTILES
    x1_of, u_of, acc_of = {}, {}, {}

    def rows(sub):
        return slice(sub * ts, (sub + 1) * ts)

    def mixer_sublayer(sub):
        r = rows(sub)
        o = jnp.concatenate([na_ref[0, r, :], gl_ref[0, r, :], ga_ref[0, r, :]], axis=-1)
        y = DEEPNORM_ALPHA * x_ref[0, r, :] + g1_ref[0] * _dot(o, wm_ref[...])
        x1 = _layer_norm(y, l1g_ref[...], l1b_ref[...])
        x1_of[sub] = x1
        u_of[sub] = (x1 * (1.0 + sc_ref[0]) + sh_ref[0]).astype(BF16)

    def up(item):
        sub, (a0, a1) = item
        if sub not in u_of:
            mixer_sublayer(sub)
        u = u_of[sub]
        return item, _dot(u, wi_ref[:, a0:a1]), _dot(u, wi_ref[:, FFN_HIDDEN + a0:FFN_HIDDEN + a1])

    def gate(st):
        item, ha, hb = st
        return item, (_silu(ha) * hb).astype(BF16)

    def down(st):
        item, t = st
        return item, _dot(t, wo_ref[item[1][0]:item[1][1], :])

    def accumulate(st):
        (sub, (a0, a1)), part = st
        acc_of[sub] = acc_of[sub] + part if sub in acc_of else part
        if a1 == FFN_HIDDEN:
            y = DEEPNORM_ALPHA * x1_of.pop(sub) + g2_ref[0] * acc_of.pop(sub)
            o_ref[0, rows(sub), :] = _layer_norm(y, l2g_ref[...], l2b_ref[...])

    _skewed((up, gate, down, accumulate), [(sub, ch) for sub in range(FFN_SUBTILES) for ch in _FFN_CHUNKS])


def _mix_ffn(x, o_na, o_gl, o_ga, wm, g1, l1g, l1b, sc, sh, g2, wi, wo, l2g, l2b, tm):
    B, N, D = x.shape
    tspec = lambda width: pl.BlockSpec((1, tm, width), lambda b, i: (b, i, 0))
    mod = pl.BlockSpec((1, 1, D), lambda b, i: (b, 0, 0))
    vec = pl.BlockSpec((1, D), lambda b, i: (0, 0))
    resident = lambda a: pl.BlockSpec(a.shape, lambda b, i: (0, 0), pipeline_mode=pl.Buffered(1))
    return pl.pallas_call(
        _mix_ffn_kernel,
        grid=(B, N // tm),
        in_specs=[tspec(D), tspec(NA_W), tspec(GLA_VW), tspec(GQA_QW), resident(wm), mod, vec, vec,
                  mod, mod, mod, resident(wi), resident(wo), vec, vec],
        out_specs=tspec(D),
        out_shape=jax.ShapeDtypeStruct((B, N, D), F32),
        compiler_params=_cparams(2),
        name="mix_ffn",
    )(x, o_na, o_gl, o_ga, wm, g1, l1g, l1b, sc, sh, g2, wi, wo, l2g, l2b)


def _rope_tables(n):
    t = jnp.arange(n)
    row = (t // GRID_W).astype(F32)
    col = (t % GRID_W).astype(F32)
    inv_freq = ROPE_THETA ** (-jnp.arange(ROPE_AXIS_PAIRS, dtype=F32) / ROPE_AXIS_PAIRS)
    ang_r = row[:, None] * inv_freq
    ang_c = col[:, None] * inv_freq
    ang = jnp.concatenate([ang_r, ang_r, ang_c, ang_c], axis=-1)
    sign = jnp.where((jnp.arange(HEAD_DIM) % 32) < 16, -1.0, 1.0).astype(F32)
    cos = jnp.tile(jnp.cos(ang), (1, 2))
    sin = jnp.tile(jnp.sin(ang) * sign, (1, 2))
    return cos, sin


def _pair_major(w, axis):
    shape = w.shape
    lead, tail = shape[:axis], shape[axis + 1:]
    w = w.reshape(lead + (GQA_KV_HEADS, GQA_REP, HEAD_DIM) + tail)
    w = jnp.swapaxes(w, axis, axis + 1)
    return w.reshape(shape)


def kernel(x, c, ctx, c_ctx, w_ada, b_ada, w_in, na_rpb, gla_wa2, gla_ba, gla_norm_w, gqa_qnorm_w,
           gqa_knorm_w, w_out, ln1_g, ln1_b, w_ffn_in, w_ffn_out, ln2_g, ln2_b):
    B, N, D = x.shape
    L = ctx.shape[1]
    depth = w_in.shape[0]
    tm = min(N, ROW_TILE)
    tmc = min(B * L, ROW_TILE)
    assert D == D_MODEL and N % tm == 0 and (B * L) % tmc == 0
    assert N % (GRID_W * NA_WIN_ROWS) == 0 and N % GLA_CHUNK == 0 and L % GLA_CHUNK == 0

    pad = (-(B + 1)) % 8
    cvec = jnp.concatenate([c, c_ctx[None, :], jnp.zeros((pad, D), F32)], axis=0)
    mods = _ada(cvec, w_ada, b_ada)

    w_lr = w_in[:, :, _O_GLLR:_O_GAQ].reshape(depth, D, 2, GLA_RANK).transpose(0, 2, 1, 3)
    w_z = _fold_gate_weights(w_lr, gla_wa2)

    cos, sin = _rope_tables(N)
    cos_c = jnp.ones((B * L, LANES), F32)
    sin_c = jnp.zeros((B * L, LANES), F32)
    gla_consts = _gla_constants()
    per_batch = lambda t: t.reshape(B, L, t.shape[-1])
    flat = lambda t: t.reshape(1, B * L, t.shape[-1])

    xc = flat(ctx)
    for l in range(depth):
        ctx_out = l < depth - 1
        m_lat = mods[l, :B].reshape(B, 6, 1, D)
        m_ctx = mods[l, B].reshape(1, 6, 1, D)
        sh1, sc1, g1, sh2, sc2, g2 = (m_lat[:, i] for i in range(6))
        sh1c, sc1c, g1c, sh2c, sc2c, g2c = (m_ctx[:, i] for i in range(6))

        wl = w_in[l]
        w_proj = jnp.concatenate(
            [wl[:, _O_NAQ:_O_GLQ], _pair_major(wl[:, _O_GAQ:_O_GAK], 1), wl[:, _O_GLQ:_O_GLLR],
             w_z[l], wl[:, _O_GAK:_O_END]], axis=1).astype(BF16)
        zb = gla_ba[l].reshape(1, 2 * GLA_KW)
        qnw = jnp.tile(gqa_qnorm_w[l], 2).reshape(1, LANES)
        knw = jnp.tile(gqa_knorm_w[l], 2).reshape(1, LANES)
        glnw = jnp.tile(gla_norm_w[l], 2).reshape(1, LANES)
        wo_l = w_out[l]
        w_o = jnp.concatenate([wo_l[:NA_W + GLA_VW], _pair_major(wo_l[NA_W + GLA_VW:], 0)], axis=0).astype(BF16)
        lg1, lb1 = ln1_g[l].reshape(1, D), ln1_b[l].reshape(1, D)
        lg2, lb2 = ln2_g[l].reshape(1, D), ln2_b[l].reshape(1, D)
        wi = w_ffn_in[l].astype(BF16)
        wo = w_ffn_out[l].astype(BF16)

        (na_q, na_k, na_v, ga_q, gl_q, gl_k, gl_v, gl_g, gl_la, ga_k, ga_v) = _inproj(
            x, sc1, sh1, w_proj, zb, qnw, knw, cos, sin, tm)
        (na_qc, na_kc, na_vc, ga_qc, gl_qc, gl_kc, gl_vc, gl_gc, gl_lac, ga_kc, ga_vc) = map(
            per_batch, _inproj(xc, sc1c, sh1c, w_proj, zb, qnw, knw, cos_c, sin_c, tmc))

        bias = _na_bias(na_rpb[l].reshape(-1))
        o_na = _na_attention(na_q, na_k, na_v, na_kc, na_vc, bias)
        o_gl, oc_gl = _gla(gl_q, gl_k, gl_v, gl_la, gl_g, gl_qc, gl_kc, gl_vc, gl_lac, gl_gc, glnw, gla_consts)
        o_ga = _pair_attention(ga_q, ga_k, ga_v, ga_kc, ga_vc, shared_kv=True, tq=min(N, GQA_Q_TILE))

        x = _mix_ffn(x, o_na, o_gl, o_ga, w_o, g1, lg1, lb1, sc2, sh2, g2, wi, wo, lg2, lb2, tm)

        if ctx_out:
            nbc = math.gcd(B, CTX_ATTN_BATCH_BLOCK)
            oc_na = _pair_attention(na_qc, na_kc, na_vc, shared_kv=False, tq=L, batch_block=nbc)
            oc_ga = _pair_attention(ga_qc, ga_kc, ga_vc, shared_kv=True, tq=L, batch_block=nbc)
            xc = _mix_ffn(xc, flat(oc_na), flat(oc_gl), flat(oc_ga), w_o, g1c, lg1, lb1, sc2c, sh2c, g2c,
                          wi, wo, lg2, lb2, tmc)
    return x
```

```python
import functools
import math

import numpy as np
import jax
import jax.numpy as jnp
from jax import lax
from jax.experimental import pallas as pl
from jax.experimental.pallas import tpu as pltpu

F32 = jnp.float32
BF16 = jnp.bfloat16

D_MODEL = 1024
DEPTH = 2
GRID_W = 64
HEAD_DIM = 64
LANES = 128

NA_HEADS = 6
NA_WIN_ROWS = 8
NA_WIN_COLS = 16
NA_W = NA_HEADS * HEAD_DIM
NA_PAIRS = NA_W // LANES
NA_DR = 2 * NA_WIN_ROWS - 1
NA_DC = 2 * NA_WIN_COLS - 1
NA_ROWS_PER_STEP = 32

GLA_HEADS = 4
GLA_DK = 32
GLA_DV = 64
GLA_RANK = 16
GLA_GATE_NORM = 16.0
GLA_CHUNK = 64
GLA_KW = GLA_HEADS * GLA_DK
GLA_VW = GLA_HEADS * GLA_DV
GLA_LEVELS = 6
GLA_MILD_DECAY = 40.0
GLA_CHUNKS_PER_STEP = 8

GQA_Q_HEADS = 6
GQA_KV_HEADS = 2
GQA_REP = GQA_Q_HEADS // GQA_KV_HEADS
GQA_QW = GQA_Q_HEADS * HEAD_DIM
GQA_KVW = GQA_KV_HEADS * HEAD_DIM
GQA_PAIRS = GQA_QW // LANES
ROW_TILE = 1024
GQA_Q_TILE = 2048
CTX_ATTN_BATCH_BLOCK = 4
INPROJ_SUBTILES = 4
FFN_SUBTILES = 2
PAIR_ATTN_SUB_ROWS = 128
ROPE_THETA = 10000.0
ROPE_AXIS_PAIRS = HEAD_DIM // 4

FFN_HIDDEN = 2816
MIX_W = NA_W + GLA_VW + GQA_QW

DEEPNORM_ALPHA = (2.0 * DEPTH) ** 0.25
LN_EPS = 1e-5
RMS_EPS = 1e-6
NEG_BIG = -1e30
LOG2_E = 1.4426950408889634
Q_SCALE = HEAD_DIM ** -0.5 * LOG2_E

_O_NAQ, _O_NAK, _O_NAV = 0, 384, 768
_O_GLQ, _O_GLK, _O_GLV, _O_GLG, _O_GLLR = 1152, 1280, 1408, 1664, 1920
_O_GAQ, _O_GAK, _O_GAV, _O_END = 1952, 2336, 2464, 2592
_C_NAQ, _C_NAK, _C_NAV, _C_GAQ = 0, 384, 768, 1152
_C_GLQ, _C_GLK, _C_GLV, _C_GLG, _C_Z = 1536, 1664, 1792, 2048, 2304
_C_GAK, _C_GAV, _C_END = 2560, 2688, 2816

VMEM_LIMIT = 48 * 1024 * 1024


def _cparams(n_axes):
    return pltpu.CompilerParams(dimension_semantics=("arbitrary",) * n_axes,
                                vmem_limit_bytes=VMEM_LIMIT)


def _dot(a, b):
    return jnp.dot(a, b, preferred_element_type=F32)


def _dot_nt(a, b):
    return lax.dot_general(a, b, (((1,), (1,)), ((), ())), preferred_element_type=F32)


def _dot_tn(a, b):
    return lax.dot_general(a, b, (((0,), (0,)), ((), ())), preferred_element_type=F32)


def _silu(x):
    return x * jax.nn.sigmoid(x)


def _layer_norm(y, g, b):
    mu = jnp.mean(y, axis=-1, keepdims=True)
    d = y - mu
    var = jnp.mean(d * d, axis=-1, keepdims=True)
    return d * lax.rsqrt(var + LN_EPS) * g + b


def _skewed(stages, items):
    n, k = len(items), len(stages)
    live = {}
    for j in range(n + k - 1):
        for s in range(k):
            idx = j - s
            if 0 <= idx < n:
                live[s, idx] = stages[s](items[idx] if s == 0 else live.pop((s - 1, idx)))


def _half_mean_square(y, lo):
    s = y * y
    s_lo = jnp.sum(jnp.where(lo, s, 0.0), axis=-1, keepdims=True)
    s_hi = jnp.sum(jnp.where(lo, 0.0, s), axis=-1, keepdims=True)
    return jnp.where(lo, s_lo, s_hi) * (1.0 / HEAD_DIM)


_ADA_TN = 1024


def _ada_kernel(c_ref, w_ref, b_ref, o_ref):
    s = _silu(c_ref[...])
    w = w_ref[0]
    s_hi, w_hi = s.astype(BF16), w.astype(BF16)
    s_lo = (s - s_hi.astype(F32)).astype(BF16)
    w_lo = (w - w_hi.astype(F32)).astype(BF16)
    o_ref[0] = _dot(s_hi, w_hi) + _dot(s_hi, w_lo) + _dot(s_lo, w_hi) + b_ref[0]


def _ada(cvec, w_ada, b_ada):
    rows = cvec.shape[0]
    depth, d, n6 = w_ada.shape
    return pl.pallas_call(
        _ada_kernel,
        grid=(depth, n6 // _ADA_TN),
        in_specs=[pl.BlockSpec((rows, d), lambda l, j: (0, 0)),
                  pl.BlockSpec((1, d, _ADA_TN), lambda l, j: (l, 0, j)),
                  pl.BlockSpec((1, 1, _ADA_TN), lambda l, j: (l, 0, j))],
        out_specs=pl.BlockSpec((1, rows, _ADA_TN), lambda l, j: (l, 0, j)),
        out_shape=jax.ShapeDtypeStruct((depth, rows, n6), F32),
        compiler_params=_cparams(2),
        name="ada",
    )(cvec, w_ada, b_ada.reshape(depth, 1, n6))


def _fold_kernel(wlr_ref, wa2_ref, o_ref):
    for e in range(2):
        o_ref[0, :, e * GLA_KW:(e + 1) * GLA_KW] = jnp.dot(
            wlr_ref[0, e], wa2_ref[0, e], precision=lax.Precision.HIGHEST,
            preferred_element_type=F32)


def _fold_gate_weights(w_lr, wa2):
    depth = w_lr.shape[0]
    return pl.pallas_call(
        _fold_kernel,
        grid=(depth,),
        in_specs=[pl.BlockSpec((1, 2, D_MODEL, GLA_RANK), lambda l: (l, 0, 0, 0)),
                  pl.BlockSpec((1, 2, GLA_RANK, GLA_KW), lambda l: (l, 0, 0, 0))],
        out_specs=pl.BlockSpec((1, D_MODEL, 2 * GLA_KW), lambda l: (l, 0, 0)),
        out_shape=jax.ShapeDtypeStruct((depth, D_MODEL, 2 * GLA_KW), F32),
        compiler_params=_cparams(1),
        name="fold_gate",
    )(w_lr, wa2)


def _inproj_kernel(x_ref, sc_ref, sh_ref, w_ref, zb_ref, qnw_ref, knw_ref, cos_ref, sin_ref,
                   naq_ref, nak_ref, nav_ref, gaq_ref, glq_ref, glk_ref, glv_ref, glg_ref,
                   gla_ref, gak_ref, gav_ref):
    ts = x_ref.shape[1] // INPROJ_SUBTILES
    lane = lax.broadcasted_iota(jnp.int32, (ts, LANES), 1)
    lo = lane < HEAD_DIM
    first16 = (lane & 31) < 16
    u_of = {}

    def rows(sub):
        return slice(sub * ts, (sub + 1) * ts)

    def matmul(item):
        sub, (a, b, epilogue) = item
        if sub not in u_of:
            u_of[sub] = (x_ref[0, rows(sub), :] * (1.0 + sc_ref[0]) + sh_ref[0]).astype(BF16)
        return sub, epilogue, _dot(u_of[sub], w_ref[:, a:b])

    def norm_rope(y, w, r):
        yn = y * lax.rsqrt(_half_mean_square(y, lo) + RMS_EPS) * w
        rot = jnp.where(first16, pltpu.roll(yn, LANES - 16, 1), pltpu.roll(yn, 16, 1))
        return yn * cos_ref[r, :] + rot * sin_ref[r, :]

    def na_qk(r, y):
        naq_ref[0, r, :] = (y[:, :NA_W] * Q_SCALE).astype(BF16)
        nak_ref[0, r, :] = y[:, NA_W:].astype(BF16)

    def na_v_gqa_q(r, y):
        nav_ref[0, r, :] = y[:, :NA_W].astype(BF16)
        qnw = qnw_ref[...]
        for p in range(GQA_PAIRS):
            t = norm_rope(y[:, NA_W + p * LANES:NA_W + (p + 1) * LANES], qnw, r)
            gaq_ref[0, r, p * LANES:(p + 1) * LANES] = (t * Q_SCALE).astype(BF16)

    def gla_qkv(r, y):
        glq_ref[0, r, :] = y[:, :GLA_KW] * GLA_DK ** -0.5
        glk_ref[0, r, :] = y[:, GLA_KW:2 * GLA_KW]
        glv_ref[0, r, :] = y[:, 2 * GLA_KW:].astype(BF16)

    def gla_gates(r, y):
        glg_ref[0, r, :] = y[:, :GLA_VW]
        z = y[:, GLA_VW:] + zb_ref[...]
        log_sig = jnp.minimum(z, 0.0) - jnp.log1p(jnp.exp(-jnp.abs(z)))
        gla_ref[0, r, :] = log_sig * (1.0 / GLA_GATE_NORM)

    def gqa_kv(r, y):
        gak_ref[0, r, :] = norm_rope(y[:, :LANES], knw_ref[...], r).astype(BF16)
        gav_ref[0, r, :] = y[:, LANES:].astype(BF16)

    segments = [(_C_NAQ, _C_NAV, na_qk), (_C_NAV, _C_GLQ, na_v_gqa_q), (_C_GLQ, _C_GLG, gla_qkv),
                (_C_GLG, _C_GAK, gla_gates), (_C_GAK, _C_END, gqa_kv)]
    _skewed((matmul, lambda st: st[1](rows(st[0]), st[2])),
            [(sub, sg) for sub in range(INPROJ_SUBTILES) for sg in segments])


def _inproj(x, sc, sh, w, zb, qnw, knw, cos, sin, tm):
    B, N, D = x.shape
    tok = lambda width, dt: jax.ShapeDtypeStruct((B, N, width), dt)
    tspec = lambda width: pl.BlockSpec((1, tm, width), lambda b, i: (b, i, 0))
    vec = lambda width: pl.BlockSpec((1, width), lambda b, i: (0, 0))
    out_shapes = [tok(NA_W, BF16), tok(NA_W, BF16), tok(NA_W, BF16), tok(GQA_QW, BF16),
                  tok(GLA_KW, F32), tok(GLA_KW, F32), tok(GLA_VW, BF16), tok(GLA_VW, F32),
                  tok(2 * GLA_KW, F32), tok(GQA_KVW, BF16), tok(GQA_KVW, BF16)]
    return pl.pallas_call(
        _inproj_kernel,
        grid=(B, N // tm),
        in_specs=[tspec(D),
                  pl.BlockSpec((1, 1, D), lambda b, i: (b, 0, 0)),
                  pl.BlockSpec((1, 1, D), lambda b, i: (b, 0, 0)),
                  pl.BlockSpec((D, _C_END), lambda b, i: (0, 0)),
                  vec(2 * GLA_KW), vec(LANES), vec(LANES),
                  pl.BlockSpec((tm, LANES), lambda b, i: (i, 0)),
                  pl.BlockSpec((tm, LANES), lambda b, i: (i, 0))],
        out_specs=[tspec(s.shape[-1]) for s in out_shapes],
        out_shape=out_shapes,
        compiler_params=_cparams(2),
        name="inproj",
    )(x, sc, sh, w, zb, qnw, knw, cos, sin)


def _nabias_kernel(rpb_ref, o_ref):
    h = pl.program_id(0)
    qc = lax.broadcasted_iota(jnp.int32, (GRID_W, LANES), 0)
    lane = lax.broadcasted_iota(jnp.int32, (GRID_W, LANES), 1)
    kc = lane & (GRID_W - 1)
    hi = lane >= GRID_W
    idx = jnp.clip(kc - qc, -(NA_WIN_COLS - 1), NA_WIN_COLS - 1) + (NA_WIN_COLS - 1)
    start = jnp.clip(qc - NA_WIN_COLS // 2, 0, GRID_W - NA_WIN_COLS)
    col_in = jnp.logical_and(kc >= start, kc < start + NA_WIN_COLS)
    base = h * (NA_DR * NA_DC)

    pair_tables = []
    for dr0 in range(NA_DR - 1):

        def body(j, t, dr0=dr0):
            s0 = rpb_ref[base + dr0 * NA_DC + j]
            s1 = rpb_ref[base + (dr0 + 1) * NA_DC + j]
            return jnp.where(idx == j, jnp.where(hi, s1, s0), t)

        t = lax.fori_loop(0, NA_DC, body, jnp.zeros((GRID_W, LANES), F32), unroll=True)
        pair_tables.append(jnp.where(col_in, t * LOG2_E, NEG_BIG))
    for oi in range(NA_WIN_ROWS):
        for i in range(NA_WIN_ROWS // 2):
            o_ref[0, oi, :, i * LANES:(i + 1) * LANES] = pair_tables[oi + 2 * i]


def _na_bias(rpb_flat):
    return pl.pallas_call(
        _nabias_kernel,
        grid=(NA_HEADS,),
        in_specs=[pl.BlockSpec(memory_space=pltpu.SMEM)],
        out_specs=pl.BlockSpec((1, NA_WIN_ROWS, GRID_W, NA_WIN_ROWS * GRID_W),
                               lambda h: (h, 0, 0, 0)),
        out_shape=jax.ShapeDtypeStruct((NA_HEADS, NA_WIN_ROWS, GRID_W, NA_WIN_ROWS * GRID_W), F32),
        compiler_params=_cparams(1),
        name="na_bias",
    )(rpb_flat)


def _na_kernel(q_ref, k_ref, v_ref, kc_ref, vc_ref, b_ref, o_ref):
    n = q_ref.shape[1]
    rows = n // GRID_W
    win = NA_WIN_ROWS * GRID_W
    lane = lax.broadcasted_iota(jnp.int32, (GRID_W, LANES), 1)
    lo = lane < HEAD_DIM
    kc = kc_ref[0]
    vc = vc_ref[0]
    zero = jnp.zeros((GRID_W, LANES), BF16)

    def scores(r):
        rs = jnp.clip(r - NA_WIN_ROWS // 2, 0, rows - NA_WIN_ROWS)
        oi = rs - r + (NA_WIN_ROWS - 1)
        q0 = pl.multiple_of(r * GRID_W, GRID_W)
        k0 = pl.multiple_of(rs * GRID_W, GRID_W)
        q = q_ref[0, pl.ds(q0, GRID_W), :]
        kw = k_ref[0, pl.ds(k0, win), :]
        qs = jnp.concatenate([jnp.where(lo, q, zero), jnp.where(lo, zero, q)], axis=0)
        s_lat = _dot_nt(qs, kw) + jnp.concatenate([b_ref[0, oi], b_ref[1, oi]], axis=0)
        s_ctx = _dot_nt(qs, kc)
        return q0, k0, s_lat, s_ctx

    def softmax(st):
        q0, k0, s_lat, s_ctx = st
        m = jnp.maximum(jnp.max(s_lat, axis=-1, keepdims=True),
                        jnp.max(s_ctx, axis=-1, keepdims=True))
        p_lat = jnp.exp2(s_lat - m)
        p_ctx = jnp.exp2(s_ctx - m)
        den = jnp.sum(p_lat, axis=-1, keepdims=True) + jnp.sum(p_ctx, axis=-1, keepdims=True)
        return q0, k0, p_lat.astype(BF16), p_ctx.astype(BF16), den

    def values(st):
        q0, k0, p_lat, p_ctx, den = st
        vw = v_ref[0, pl.ds(k0, win), :]
        o = (_dot(p_lat, vw) + _dot(p_ctx, vc)) / den
        o_ref[0, pl.ds(q0, GRID_W), :] = jnp.where(lo, o[:GRID_W], o[GRID_W:]).astype(BF16)

    per = min(NA_ROWS_PER_STEP, rows)

    def body(i, carry):
        _skewed((scores, softmax, values), [i * per + j for j in range(per)])
        return carry

    lax.fori_loop(0, rows // per, body, 0)


def _na_attention(q, k, v, kc, vc, bias):
    B, N, _ = q.shape
    L = kc.shape[1]
    lat = pl.BlockSpec((1, N, LANES), lambda b, p: (b, 0, p))
    cx = pl.BlockSpec((1, L, LANES), lambda b, p: (b, 0, p))
    return pl.pallas_call(
        _na_kernel,
        grid=(B, NA_PAIRS),
        in_specs=[lat, lat, lat, cx, cx,
                  pl.BlockSpec((2, NA_WIN_ROWS, GRID_W, NA_WIN_ROWS * GRID_W),
                               lambda b, p: (p, 0, 0, 0))],
        out_specs=lat,
        out_shape=jax.ShapeDtypeStruct((B, N, NA_W), BF16),
        compiler_params=_cparams(2),
        name="na_attn",
    )(q, k, v, kc, vc, bias)


def _pair_attn_kernel(*refs, two_sources):
    if two_sources:
        q_ref, ka_ref, va_ref, kb_ref, vb_ref, o_ref = refs
    else:
        q_ref, ka_ref, va_ref, o_ref = refs
    tq = q_ref.shape[1]
    ts = min(tq, PAIR_ATTN_SUB_ROWS)
    lane = lax.broadcasted_iota(jnp.int32, (ts, LANES), 1)
    lo = lane < HEAD_DIM
    zero = jnp.zeros((ts, LANES), BF16)

    def with_ones(v):
        return jnp.concatenate([v, jnp.ones_like(v)], axis=1)

    n_batch = q_ref.shape[0]
    ka = [ka_ref[b] for b in range(n_batch)]
    va = [with_ones(va_ref[b]) for b in range(n_batch)]
    if two_sources:
        kb = [kb_ref[b] for b in range(n_batch)]
        vb = [with_ones(vb_ref[b]) for b in range(n_batch)]

    def scores(item):
        b, sub, h = item
        q = q_ref[b, sub * ts:(sub + 1) * ts, :]
        qm = jnp.where(lo, q, zero) if h == 0 else jnp.where(lo, zero, q)
        s_a = _dot_nt(qm, ka[b])
        s_b = _dot_nt(qm, kb[b]) if two_sources else None
        return item, s_a, s_b

    def softmax(st):
        item, s_a, s_b = st
        m = jnp.max(s_a, axis=-1, keepdims=True)
        if two_sources:
            m = jnp.maximum(m, jnp.max(s_b, axis=-1, keepdims=True))
        p_a = jnp.exp2(s_a - m).astype(BF16)
        p_b = jnp.exp2(s_b - m).astype(BF16) if two_sources else None
        return item, p_a, p_b

    done = {}

    def values(st):
        (b, sub, h), p_a, p_b = st
        o = _dot(p_a, va[b])
        if two_sources:
            o = o + _dot(p_b, vb[b])
        done[b, sub, h] = o[:, :LANES] / o[:, LANES:]
        if h == 1:
            o_ref[b, sub * ts:(sub + 1) * ts, :] = jnp.where(
                lo, done.pop((b, sub, 0)), done.pop((b, sub, 1))).astype(BF16)

    _skewed((scores, softmax, values),
            [(b, sub, h) for b in range(n_batch) for sub in range(tq // ts) for h in range(2)])


def _pair_attention(q, ka, va, kb=None, vb=None, *, shared_kv, tq, batch_block=1):
    B, Nq, W = q.shape
    pairs = W // LANES
    nb = batch_block
    kv_map = (lambda b, p, i: (b, 0, 0)) if shared_kv else (lambda b, p, i: (b, 0, p))
    qspec = pl.BlockSpec((nb, tq, LANES), lambda b, p, i: (b, i, p))
    in_specs = [qspec,
                pl.BlockSpec((nb, ka.shape[1], LANES), kv_map),
                pl.BlockSpec((nb, ka.shape[1], LANES), kv_map)]
    args = [q, ka, va]
    if kb is not None:
        in_specs += [pl.BlockSpec((nb, kb.shape[1], LANES), kv_map),
                     pl.BlockSpec((nb, kb.shape[1], LANES), kv_map)]
        args += [kb, vb]
    return pl.pallas_call(
        functools.partial(_pair_attn_kernel, two_sources=kb is not None),
        grid=(B // nb, pairs, Nq // tq),
        in_specs=in_specs,
        out_specs=qspec,
        out_shape=jax.ShapeDtypeStruct((B, Nq, W), BF16),
        compiler_params=_cparams(3),
        name="pair_attn",
    )(*args)


def _gla_constants():
    c = GLA_CHUNK
    t = np.arange(c)[:, None]
    i = np.arange(c)[None, :]
    mcat = np.zeros((2, (GLA_LEVELS + 1) * c, c), np.float32)
    masks = np.zeros((2, GLA_LEVELS + 2, c, c), np.float32)
    for d in range(2):
        masks[d, GLA_LEVELS + 1] = (i <= t) if d == 0 else (i >= t)
        for l in range(1, GLA_LEVELS + 1):
            m = 2 ** l
            same = (t // m) == (i // m)
            q_side = same & ((i <= t) if d == 0 else (i >= t))
            k_side = same & ((i > t) if d == 0 else (i < t))
            if l < GLA_LEVELS:
                is_q_row = ((t // m) % 2 == 1) if d == 0 else ((t // m) % 2 == 0)
                mcat[d, (l - 1) * c:l * c] = np.where(is_q_row, q_side, k_side)
            else:
                mcat[d, (l - 1) * c:l * c] = q_side
                mcat[d, l * c:(l + 1) * c] = k_side
        masks[d, 0] = (t == i)
        for l in range(GLA_LEVELS):
            m = 2 ** l
            tb, sb = t // m, i // m
            if d == 0:
                masks[d, l + 1] = (tb % 2 == 1) & (sb == tb - 1)
            else:
                masks[d, l + 1] = (tb % 2 == 0) & (sb == tb + 1)
    masks = np.tile(masks, (1, 1, 1, GLA_HEADS))
    rows = np.arange(GLA_VW)[:, None] // GLA_DV
    cols = np.arange(GLA_KW)[None, :] // GLA_DK
    stmask = (rows == cols).astype(np.float32)
    return jnp.asarray(mcat, BF16), jnp.asarray(masks, F32), jnp.asarray(stmask, F32)


def _gla_stages(q_ref, k_ref, v_ref, la_ref, acc_ref, st_ref, mcat_ref, lmask_ref, stmask_ref, mild):
    ck = GLA_CHUNK
    whole = slice((GLA_LEVELS - 1) * ck, (GLA_LEVELS + 1) * ck)
    lane_k = lax.broadcasted_iota(jnp.int32, (1, GLA_KW), 1) // GLA_DK
    lane_v = lax.broadcasted_iota(jnp.int32, (1, GLA_VW), 1) // GLA_DV
    row = lax.broadcasted_iota(jnp.int32, (ck, GLA_KW), 0)

    def block_rows(x, lane_head):
        return jnp.concatenate([jnp.where(lane_head == h, x, 0.0) for h in range(GLA_HEADS)], axis=0)

    def exponents(item):
        c, d = item
        r0 = pl.multiple_of(c * ck, ck)
        g = la_ref[0, pl.ds(r0, ck), d * GLA_KW:(d + 1) * GLA_KW]
        g1 = g.astype(BF16)
        g2 = (g - g1.astype(F32)).astype(BF16)
        mc = mcat_ref[d, whole, :] if mild else mcat_ref[d]
        e = _dot(mc, g1) + _dot(mc, g2)
        return r0, d, g, e

    def in_chunk(st):
        r0, d, g, e = st
        q = q_ref[0, pl.ds(r0, ck), :]
        k = k_ref[0, pl.ds(r0, ck), :]
        q_chunk = e[-2 * ck:-ck]
        k_chunk = e[-ck:]
        last = ck - 1 if d == 0 else 0
        total = q_chunk[last:last + 1]
        q_in = (q * jnp.exp(q_chunk)).astype(BF16)

        def level_scores(qt, kt, idx):
            kb = block_rows(kt, lane_k).astype(BF16)
            return jnp.where(lmask_ref[d, idx] != 0.0, _dot_nt(qt, kb), 0.0)

        if mild:
            a = level_scores(q_in, k * jnp.exp(-q_chunk), GLA_LEVELS + 1)
        else:
            q_row0 = (row & 1) == (1 - d)
            a = level_scores(q.astype(BF16), k, 0)
            for l in range(GLA_LEVELS):
                dec = jnp.exp(jnp.where(q_row0, g, 0.0) if l == 0 else e[(l - 1) * ck:l * ck])
                a = a + level_scores((q * dec).astype(BF16), k * dec, l + 1)
        return r0, d, a.astype(BF16), q_in, (k * jnp.exp(k_chunk)).astype(BF16), jnp.exp(total)

    def state(st):
        r0, d, a, q_in, k_out, decay = st
        v = v_ref[0, pl.ds(r0, ck), :]
        vbd = block_rows(v.astype(F32), lane_v).astype(BF16)
        s_prev = st_ref[d]
        return (r0, d, _dot(a, vbd), _dot_nt(q_in, s_prev.astype(BF16)), s_prev * decay, _dot_tn(v, k_out))

    def commit(st):
        r0, d, o_in, o_cross, s_decayed, update = st
        acc_ref[d, pl.ds(r0, ck), :] = o_in + o_cross
        st_ref[d] = s_decayed + stmask_ref[...] * update

    return exponents, in_chunk, state, commit


def _gla_finish(acc_ref, g_ref, nw_ref, o_ref, tile):
    n = acc_ref.shape[1]
    lane = lax.broadcasted_iota(jnp.int32, (tile, LANES), 1)
    lo = lane < GLA_DV
    nw = nw_ref[...]

    def body(i, carry):
        r0 = pl.multiple_of(i * tile, tile)
        for j in range(GLA_VW // LANES):
            cols = slice(j * LANES, (j + 1) * LANES)
            o = acc_ref[0, pl.ds(r0, tile), cols] + acc_ref[1, pl.ds(r0, tile), cols]
            on = o * lax.rsqrt(_half_mean_square(o, lo) + RMS_EPS) * nw
            gate = _silu(g_ref[0, pl.ds(r0, tile), cols])
            o_ref[0, pl.ds(r0, tile), cols] = (on * gate).astype(BF16)
        return carry

    lax.fori_loop(0, n // tile, body, 0, unroll=2)


def _gla_kernel(q_ref, k_ref, v_ref, la_ref, g_ref, qc_ref, kc_ref, vc_ref, lac_ref, gc_ref,
                nw_ref, mcat_ref, lmask_ref, stmask_ref, o_ref, oc_ref, acc_ref, accc_ref, st_ref):
    n_chunks = q_ref.shape[1] // GLA_CHUNK
    c_chunks = qc_ref.shape[1] // GLA_CHUNK
    consts = (mcat_ref, lmask_ref, stmask_ref)
    st_ref[...] = jnp.zeros_like(st_ref)

    def min_chunk_sum(ref):
        la = ref[0]
        sums = jnp.sum(la.reshape(la.shape[0] // GLA_CHUNK, GLA_CHUNK, la.shape[1]), axis=1)
        return jnp.min(sums)

    mild = jnp.minimum(min_chunk_sum(la_ref), min_chunk_sum(lac_ref)) >= -GLA_MILD_DECAY

    def steps(refs, acc, n, is_mild):
        stages = _gla_stages(*refs, acc, st_ref, *consts, is_mild)

        per = min(GLA_CHUNKS_PER_STEP, n)

        def body(i, carry):
            items = []
            for j in range(per):
                c = i * per + j
                items += [(c, 0), (n - 1 - c, 1)]
            _skewed(stages, items)
            return carry

        lax.fori_loop(0, n // per, body, 0)

    for is_mild in (True, False):
        @pl.when(mild if is_mild else jnp.logical_not(mild))
        def _():
            steps((qc_ref, kc_ref, vc_ref, lac_ref), accc_ref, c_chunks, is_mild)
            steps((q_ref, k_ref, v_ref, la_ref), acc_ref, n_chunks, is_mild)

    _gla_finish(acc_ref, g_ref, nw_ref, o_ref, 256)
    _gla_finish(accc_ref, gc_ref, nw_ref, oc_ref, 256)


def _gla(q, k, v, la, g, qc, kc, vc, lac, gc, nw, consts):
    B, N, _ = q.shape
    L = qc.shape[1]
    mcat, lmask, stmask = consts
    tok = lambda n, w: pl.BlockSpec((1, n, w), lambda b: (b, 0, 0))
    full = lambda a: pl.BlockSpec(a.shape, lambda b: (0,) * a.ndim)
    return pl.pallas_call(
        _gla_kernel,
        grid=(B,),
        in_specs=[tok(N, GLA_KW), tok(N, GLA_KW), tok(N, GLA_VW), tok(N, 2 * GLA_KW), tok(N, GLA_VW),
                  tok(L, GLA_KW), tok(L, GLA_KW), tok(L, GLA_VW), tok(L, 2 * GLA_KW), tok(L, GLA_VW),
                  full(nw), full(mcat), full(lmask), full(stmask)],
        out_specs=[tok(N, GLA_VW), tok(L, GLA_VW)],
        out_shape=[jax.ShapeDtypeStruct((B, N, GLA_VW), BF16), jax.ShapeDtypeStruct((B, L, GLA_VW), BF16)],
        scratch_shapes=[pltpu.VMEM((2, N, GLA_VW), F32), pltpu.VMEM((2, L, GLA_VW), F32),
                        pltpu.VMEM((2, GLA_VW, GLA_KW), F32)],
        compiler_params=_cparams(1),
        name="gla",
    )(q, k, v, la, g, qc, kc, vc, lac, gc, nw, mcat, lmask, stmask)


_FFN_CHUNKS = ((0, 768), (768, 1536), (1536, 2304), (2304, FFN_HIDDEN))


def _mix_ffn_kernel(x_ref, na_ref, gl_ref, ga_ref, wm_ref, g1_ref, l1g_ref, l1b_ref,
                    sc_ref, sh_ref, g2_ref, wi_ref, wo_ref, l2g_ref, l2b_ref, o_ref):
    ts = x_ref.shape[1] // FFN_SUBTILES
    x1_of, u_of, acc_of = {}, {}, {}

    def rows(sub):
        return slice(sub * ts, (sub + 1) * ts)

    def mixer_sublayer(sub):
        r = rows(sub)
        o = jnp.concatenate([na_ref[0, r, :], gl_ref[0, r, :], ga_ref[0, r, :]], axis=-1)
        y = DEEPNORM_ALPHA * x_ref[0, r, :] + g1_ref[0] * _dot(o, wm_ref[...])
        x1 = _layer_norm(y, l1g_ref[...], l1b_ref[...])
        x1_of[sub] = x1
        u_of[sub] = (x1 * (1.0 + sc_ref[0]) + sh_ref[0]).astype(BF16)

    def up(item):
        sub, (a0, a1) = item
        if sub not in u_of:
            mixer_sublayer(sub)
        u = u_of[sub]
        return item, _dot(u, wi_ref[:, a0:a1]), _dot(u, wi_ref[:, FFN_HIDDEN + a0:FFN_HIDDEN + a1])

    def gate(st):
        item, ha, hb = st
        return item, (_silu(ha) * hb).astype(BF16)

    def down(st):
        item, t = st
        return item, _dot(t, wo_ref[item[1][0]:item[1][1], :])

    def accumulate(st):
        (sub, (a0, a1)), part = st
        acc_of[sub] = acc_of[sub] + part if sub in acc_of else part
        if a1 == FFN_HIDDEN:
            y = DEEPNORM_ALPHA * x1_of.pop(sub) + g2_ref[0] * acc_of.pop(sub)
            o_ref[0, rows(sub), :] = _layer_norm(y, l2g_ref[...], l2b_ref[...])

    _skewed((up, gate, down, accumulate), [(sub, ch) for sub in range(FFN_SUBTILES) for ch in _FFN_CHUNKS])


def _mix_ffn(x, o_na, o_gl, o_ga, wm, g1, l1g, l1b, sc, sh, g2, wi, wo, l2g, l2b, tm):
    B, N, D = x.shape
    tspec = lambda width: pl.BlockSpec((1, tm, width), lambda b, i: (b, i, 0))
    mod = pl.BlockSpec((1, 1, D), lambda b, i: (b, 0, 0))
    vec = pl.BlockSpec((1, D), lambda b, i: (0, 0))
    resident = lambda a: pl.BlockSpec(a.shape, lambda b, i: (0, 0), pipeline_mode=pl.Buffered(1))
    return pl.pallas_call(
        _mix_ffn_kernel,
        grid=(B, N // tm),
        in_specs=[tspec(D), tspec(NA_W), tspec(GLA_VW), tspec(GQA_QW), resident(wm), mod, vec, vec,
                  mod, mod, mod, resident(wi), resident(wo), vec, vec],
        out_specs=tspec(D),
        out_shape=jax.ShapeDtypeStruct((B, N, D), F32),
        compiler_params=_cparams(2),
        name="mix_ffn",
    )(x, o_na, o_gl, o_ga, wm, g1, l1g, l1b, sc, sh, g2, wi, wo, l2g, l2b)


def _rope_tables(n):
    t = jnp.arange(n)
    row = (t // GRID_W).astype(F32)
    col = (t % GRID_W).astype(F32)
    inv_freq = ROPE_THETA ** (-jnp.arange(ROPE_AXIS_PAIRS, dtype=F32) / ROPE_AXIS_PAIRS)
    ang_r = row[:, None] * inv_freq
    ang_c = col[:, None] * inv_freq
    ang = jnp.concatenate([ang_r, ang_r, ang_c, ang_c], axis=-1)
    sign = jnp.where((jnp.arange(HEAD_DIM) % 32) < 16, -1.0, 1.0).astype(F32)
    cos = jnp.tile(jnp.cos(ang), (1, 2))
    sin = jnp.tile(jnp.sin(ang) * sign, (1, 2))
    return cos, sin


def _pair_major(w, axis):
    shape = w.shape
    lead, tail = shape[:axis], shape[axis + 1:]
    w = w.reshape(lead + (GQA_KV_HEADS, GQA_REP, HEAD_DIM) + tail)
    w = jnp.swapaxes(w, axis, axis + 1)
    return w.reshape(shape)


def kernel(x, c, ctx, c_ctx, w_ada, b_ada, w_in, na_rpb, gla_wa2, gla_ba, gla_norm_w, gqa_qnorm_w,
           gqa_knorm_w, w_out, ln1_g, ln1_b, w_ffn_in, w_ffn_out, ln2_g, ln2_b):
    B, N, D = x.shape
    L = ctx.shape[1]
    depth = w_in.shape[0]
    tm = min(N, ROW_TILE)
    tmc = min(B * L, ROW_TILE)
    assert D == D_MODEL and N % tm == 0 and (B * L) % tmc == 0
    assert N % (GRID_W * NA_WIN_ROWS) == 0 and N % GLA_CHUNK == 0 and L % GLA_CHUNK == 0

    pad = (-(B + 1)) % 16
    cvec = jnp.concatenate([c, c_ctx[None, :], jnp.zeros((pad, D), F32)], axis=0)
    mods = _ada(cvec, w_ada, b_ada)

    w_lr = w_in[:, :, _O_GLLR:_O_GAQ].reshape(depth, D, 2, GLA_RANK).transpose(0, 2, 1, 3)
    w_z = _fold_gate_weights(w_lr, gla_wa2)

    cos, sin = _rope_tables(N)
    cos_c = jnp.ones((B * L, LANES), F32)
    sin_c = jnp.zeros((B * L, LANES), F32)
    gla_consts = _gla_constants()
    per_batch = lambda t: t.reshape(B, L, t.shape[-1])
    flat = lambda t: t.reshape(1, B * L, t.shape[-1])

    xc = flat(ctx)
    for l in range(depth):
        ctx_out = l < depth - 1
        m_lat = mods[l, :B].reshape(B, 6, 1, D)
        m_ctx = mods[l, B].reshape(1, 6, 1, D)
        sh1, sc1, g1, sh2, sc2, g2 = (m_lat[:, i] for i in range(6))
        sh1c, sc1c, g1c, sh2c, sc2c, g2c = (m_ctx[:, i] for i in range(6))

        wl = w_in[l]
        w_proj = jnp.concatenate(
            [wl[:, _O_NAQ:_O_GLQ], _pair_major(wl[:, _O_GAQ:_O_GAK], 1), wl[:, _O_GLQ:_O_GLLR],
             w_z[l], wl[:, _O_GAK:_O_END]], axis=1).astype(BF16)
        zb = gla_ba[l].reshape(1, 2 * GLA_KW)
        qnw = jnp.tile(gqa_qnorm_w[l], 2).reshape(1, LANES)
        knw = jnp.tile(gqa_knorm_w[l], 2).reshape(1, LANES)
        glnw = jnp.tile(gla_norm_w[l], 2).reshape(1, LANES)
        wo_l = w_out[l]
        w_o = jnp.concatenate([wo_l[:NA_W + GLA_VW], _pair_major(wo_l[NA_W + GLA_VW:], 0)], axis=0).astype(BF16)
        lg1, lb1 = ln1_g[l].reshape(1, D), ln1_b[l].reshape(1, D)
        lg2, lb2 = ln2_g[l].reshape(1, D), ln2_b[l].reshape(1, D)
        wi = w_ffn_in[l].astype(BF16)
        wo = w_ffn_out[l].astype(BF16)

        (na_q, na_k, na_v, ga_q, gl_q, gl_k, gl_v, gl_g, gl_la, ga_k, ga_v) = _inproj(
            x, sc1, sh1, w_proj, zb, qnw, knw, cos, sin, tm)
        (na_qc, na_kc, na_vc, ga_qc, gl_qc, gl_kc, gl_vc, gl_gc, gl_lac, ga_kc, ga_vc) = map(
            per_batch, _inproj(xc, sc1c, sh1c, w_proj, zb, qnw, knw, cos_c, sin_c, tmc))

        bias = _na_bias(na_rpb[l].reshape(-1))
        o_na = _na_attention(na_q, na_k, na_v, na_kc, na_vc, bias)
        o_gl, oc_gl = _gla(gl_q, gl_k, gl_v, gl_la, gl_g, gl_qc, gl_kc, gl_vc, gl_lac, gl_gc, glnw, gla_consts)
        o_ga = _pair_attention(ga_q, ga_k, ga_v, ga_kc, ga_vc, shared_kv=True, tq=min(N, GQA_Q_TILE))

        x = _mix_ffn(x, o_na, o_gl, o_ga, w_o, g1, lg1, lb1, sc2, sh2, g2, wi, wo, lg2, lb2, tm)

        if ctx_out:
            nbc = math.gcd(B, CTX_ATTN_BATCH_BLOCK)
            oc_na = _pair_attention(na_qc, na_kc, na_vc, shared_kv=False, tq=L, batch_block=nbc)
            oc_ga = _pair_attention(ga_qc, ga_kc, ga_vc, shared_kv=True, tq=L, batch_block=nbc)
            xc = _mix_ffn(xc, flat(oc_na), flat(oc_gl), flat(oc_ga), w_o, g1c, lg1, lb1, sc2c, sh2c, g2c,
                          wi, wo, lg2, lb2, tmc)
    return x
```
